```python
import math
import jax, jax.numpy as jnp
from jax import lax
import numpy as np

D_MODEL = 1024
BATCH = 4
SEQ = 8192
DEPTH = 2

CTX_LEN = 256
GRID_W = 64
N_BRANCH = 3
BRANCH_WIDTH = D_MODEL // 2
GDN_HEADS = 4
GDN_DK = 128
GDN_DV = 128
GDN_CHUNK = 64
SCONV_WIDTH = BRANCH_WIDTH
ATT_HEADS = 8
ATT_KV_HEADS = 2
ATT_GROUP = ATT_HEADS // ATT_KV_HEADS
ATT_HD = 64
ATT_BLOCK = 128
ROPE_THETA = 10000.0
ROPE_AXIS_PAIRS = ATT_HD // 4
PEER_HEADS = 8
PEER_DK = 128
PEER_NKEYS = 128
PEER_EXPERTS = PEER_NKEYS * PEER_NKEYS
PEER_TOPK = 16
PEER_CHUNK = 128
NORM_EPS = 1e-6
PROJ_SIZES = (
    2 * GDN_HEADS * GDN_DK + GDN_HEADS * GDN_DV,
    GDN_HEADS * GDN_DV,
    4 * GDN_HEADS,
    3 * SCONV_WIDTH,
    ATT_HEADS * ATT_HD,
    2 * ATT_KV_HEADS * ATT_HD,
    N_BRANCH * D_MODEL,
)
D_IN = sum(PROJ_SIZES)

kernel_name = 'hybrid_diffusion_block'


def _rmsnorm(x, w):
    xf = x.astype(jnp.float32)
    y = xf * lax.rsqrt(jnp.mean(xf * xf, axis=-1, keepdims=True) + NORM_EPS)
    return (y * w.astype(jnp.float32)).astype(x.dtype)


def _l2norm(x):
    return x * lax.rsqrt(jnp.sum(x * x, axis=-1, keepdims=True) + NORM_EPS)


def _dwconv3(x, w):
    return lax.conv_general_dilated(
        x, w[:, None, :].astype(x.dtype), window_strides=(1,), padding=((1, 1),),
        dimension_numbers=('NWC', 'WIO', 'NWC'), feature_group_count=x.shape[-1])


def _split_proj(p):
    out, start = [], 0
    for size in PROJ_SIZES:
        out.append(p[..., start:start + size])
        start += size
    return out


def _gdn_prep(qkv, ab, conv_w, a_log, dt_bias):
    b, t, _ = qkv.shape
    qkv = jax.nn.silu(_dwconv3(qkv, conv_w)).astype(jnp.float32)
    q = qkv[..., :GDN_HEADS * GDN_DK].reshape(b, t, GDN_HEADS, GDN_DK)
    k = qkv[..., GDN_HEADS * GDN_DK:2 * GDN_HEADS * GDN_DK].reshape(b, t, GDN_HEADS, GDN_DK)
    v = qkv[..., 2 * GDN_HEADS * GDN_DK:].reshape(b, t, GDN_HEADS, GDN_DV)
    q = _l2norm(q) * (GDN_DK ** -0.5)
    k = _l2norm(k)
    ab = ab.astype(jnp.float32).reshape(b, t, 2, 2, GDN_HEADS)
    g = -jnp.exp(a_log.astype(jnp.float32)) * jax.nn.softplus(ab[:, :, 0] + dt_bias.astype(jnp.float32))
    beta = jax.nn.sigmoid(ab[:, :, 1])
    to_bh = lambda z: jnp.transpose(z, (0, 2, 1, 3))
    to_dbh = lambda z: jnp.transpose(z, (2, 0, 3, 1))
    return to_bh(q), to_bh(k), to_bh(v), to_dbh(g), to_dbh(beta)


def _chunk_gated_delta(q, k, v, beta, g, s0, emit):
    b, h, t, dk = q.shape
    dv = v.shape[-1]
    n = t // GDN_CHUNK
    q = q.reshape(b, h, n, GDN_CHUNK, dk)
    k = k.reshape(b, h, n, GDN_CHUNK, dk)
    v = v.reshape(b, h, n, GDN_CHUNK, dv)
    beta = beta.reshape(b, h, n, GDN_CHUNK)
    gc = jnp.cumsum(g.reshape(b, h, n, GDN_CHUNK), axis=-1)
    pos = jnp.arange(GDN_CHUNK)
    incl = pos[:, None] >= pos[None, :]
    strict = pos[:, None] > pos[None, :]
    diff = gc[..., :, None] - gc[..., None, :]
    ldec = jnp.where(incl, jnp.exp(jnp.where(incl, diff, 0.0)), 0.0)
    kb = k * beta[..., None]
    a = jnp.where(strict, jnp.einsum('bhnid,bhnjd->bhnij', kb, k) * ldec, 0.0)
    rhs = jnp.concatenate([v * beta[..., None], kb * jnp.exp(gc)[..., None]], axis=-1)
    sol = lax.linalg.triangular_solve(a + jnp.eye(GDN_CHUNK, dtype=a.dtype), rhs,
                                      left_side=True, lower=True, unit_diagonal=True)
    g_last = gc[..., -1]
    k_dec = k * jnp.exp(g_last[..., None] - gc)[..., None]
    xs = [sol[..., :dv], sol[..., dv:], k_dec, g_last]
    if emit:
        qk = jnp.einsum('bhnid,bhnjd->bhnij', q, k) * ldec
        xs = xs + [qk, q * jnp.exp(gc)[..., None]]
    xs = tuple(jnp.moveaxis(z, 2, 0) for z in xs)

    def step(s, inp):
        u_n, w_n, kd_n, gl_n = inp[:4]
        v_new = u_n - jnp.einsum('bhik,bhkv->bhiv', w_n, s)
        s_next = s * jnp.exp(gl_n)[..., None, None] + jnp.einsum('bhik,bhiv->bhkv', kd_n, v_new)
        if not emit:
            return s_next, None
        qk_n, qd_n = inp[4:]
        o_n = jnp.einsum('bhik,bhkv->bhiv', qd_n, s) + jnp.einsum('bhij,bhjv->bhiv', qk_n, v_new)
        return s_next, o_n

    s_fin, o = lax.scan(step, s0, xs)
    if not emit:
        return None, s_fin
    return jnp.moveaxis(o, 0, 2).reshape(b, h, t, dv), s_fin


def _gated_out(o, z, norm_w):
    b, h, t, dv = o.shape
    o = jnp.transpose(o, (0, 2, 1, 3))
    o = o * lax.rsqrt(jnp.mean(o * o, axis=-1, keepdims=True) + NORM_EPS) * norm_w.astype(jnp.float32)
    zg = jax.nn.silu(z.astype(jnp.float32)).reshape(b, t, h, dv)
    return (o * zg).reshape(b, t, h * dv).astype(z.dtype)


def _gdn_branch(qkv_l, z_l, ab_l, qkv_c, z_c, ab_c, conv_w, a_log, dt_bias, norm_w, emit_ctx):
    ql, kl, vl, gl, bl = _gdn_prep(qkv_l, ab_l, conv_w, a_log, dt_bias)
    qc, kc, vc, gcx, bc = _gdn_prep(qkv_c, ab_c, conv_w, a_log, dt_bias)
    s0 = jnp.zeros((ql.shape[0], GDN_HEADS, GDN_DK, GDN_DV), jnp.float32)
    rev = lambda z: jnp.flip(z, axis=2)
    oc_f, sc_f = _chunk_gated_delta(qc, kc, vc, bc[0], gcx[0], s0, emit_ctx)
    ol_f, _ = _chunk_gated_delta(ql, kl, vl, bl[0], gl[0], sc_f, True)
    oc_b, sc_b = _chunk_gated_delta(rev(qc), rev(kc), rev(vc), rev(bc[1]), rev(gcx[1]), s0, emit_ctx)
    ol_b, _ = _chunk_gated_delta(rev(ql), rev(kl), rev(vl), rev(bl[1]), rev(gl[1]), sc_b, True)
    o_l = _gated_out(ol_f + rev(ol_b), z_l, norm_w)
    if not emit_ctx:
        return o_l, None
    return o_l, _gated_out(oc_f + rev(oc_b), z_c, norm_w)


def _shortconv_branch(p, conv_w):
    bg, cg, xin = jnp.split(p, 3, axis=-1)
    return bg * _dwconv3(cg * xin, conv_w)


def _axial_rope(x, row, col):
    freqs = ROPE_THETA ** (-jnp.arange(ROPE_AXIS_PAIRS, dtype=jnp.float32) / ROPE_AXIS_PAIRS)
    ang = jnp.stack([row[:, None] * freqs, col[:, None] * freqs], axis=1)
    shape = (1, x.shape[1]) + (1,) * (x.ndim - 3) + (2, ROPE_AXIS_PAIRS)
    cos = jnp.cos(ang).reshape(shape).astype(x.dtype)
    sin = jnp.sin(ang).reshape(shape).astype(x.dtype)
    xr = x.reshape(x.shape[:-1] + (2, 2, ROPE_AXIS_PAIRS))
    x1, x2 = xr[..., 0, :], xr[..., 1, :]
    return jnp.stack([x1 * cos - x2 * sin, x2 * cos + x1 * sin], axis=-2).reshape(x.shape)


def _q_heads(q, w):
    b, n = q.shape[:2]
    return _rmsnorm(q.reshape(b, n, ATT_KV_HEADS, ATT_GROUP, ATT_HD), w)


def _kv_heads(kv, w):
    b, n = kv.shape[:2]
    k, v = jnp.split(kv, 2, axis=-1)
    k = _rmsnorm(k.reshape(b, n, ATT_KV_HEADS, ATT_HD), w)
    return k, v.reshape(b, n, ATT_KV_HEADS, ATT_HD)


def _attend(q, keys, vals):
    s = jnp.einsum('bqhgd,bkhd->bhgqk', q, keys).astype(jnp.float32) * (ATT_HD ** -0.5)
    p = jax.nn.softmax(s, axis=-1).astype(vals.dtype)
    return jnp.einsum('bhgqk,bkhd->bqhgd', p, vals)


def _attention_branch(q_l, kv_l, q_c, kv_c, q_norm_w, k_norm_w, row, col, emit_ctx):
    b, t, _ = q_l.shape
    ql = _axial_rope(_q_heads(q_l, q_norm_w), row, col)
    kl, vl = _kv_heads(kv_l, k_norm_w)
    kl = _axial_rope(kl, row, col)
    kc, vc = _kv_heads(kv_c, k_norm_w)
    keys = jnp.concatenate([kl, kc], axis=1)
    vals = jnp.concatenate([vl, vc], axis=1)
    nb = t // ATT_BLOCK
    qb = jnp.moveaxis(ql.reshape(b, nb, ATT_BLOCK, ATT_KV_HEADS, ATT_GROUP, ATT_HD), 1, 0)
    ol = lax.map(lambda qblk: _attend(qblk, keys, vals), qb)
    ol = jnp.moveaxis(ol, 0, 1).reshape(b, t, ATT_HEADS * ATT_HD)
    if not emit_ctx:
        return ol, None
    qc = _q_heads(q_c, q_norm_w)
    oc = _attend(qc, kc, vc).reshape(b, kv_c.shape[1], ATT_HEADS * ATT_HD)
    return ol, oc


def _merge(branches, gates, w_branch, w_out):
    b, t = branches.shape[:2]
    g = jax.nn.sigmoid(gates.reshape(b, t, N_BRANCH, D_MODEL))
    mixed = jnp.einsum('btnw,nwd,btnd->btd', branches, w_branch, g)
    return mixed @ w_out


def _token_mixers(u, uc, row, col, w_in, conv_a_w, a_log, dt_bias, gdn_norm_w, conv_b_w,
                  q_norm_w, k_norm_w, w_branch, w_out, emit_ctx):
    pl = _split_proj(u @ w_in)
    pc = _split_proj(uc @ w_in)
    oa_l, oa_c = _gdn_branch(pl[0], pl[1], pl[2], pc[0], pc[1], pc[2],
                             conv_a_w, a_log, dt_bias, gdn_norm_w, emit_ctx)
    ob_l = _shortconv_branch(pl[3], conv_b_w)
    oc_l, oc_c = _attention_branch(pl[4], pl[5], pc[4], pc[5], q_norm_w, k_norm_w, row, col, emit_ctx)
    y_l = _merge(jnp.stack([oa_l, ob_l, oc_l], axis=2), pl[6], w_branch, w_out)
    if not emit_ctx:
        return y_l, None
    ob_c = _shortconv_branch(pc[3], conv_b_w)
    y_c = _merge(jnp.stack([oa_c, ob_c, oc_c], axis=2), pc[6], w_branch, w_out)
    return y_l, y_c


def _peer(u, w_query, sub_keys, expert_u, expert_v):
    b, t, d = u.shape
    chunks = u.reshape(-1, PEER_CHUNK, d)

    def chunk_fn(xc):
        qh = (xc @ w_query).reshape(PEER_CHUNK, PEER_HEADS, 2, PEER_DK // 2)
        s = jnp.einsum('chpk,hpnk->chpn', qh, sub_keys).astype(jnp.float32)
        top_s, top_i = lax.top_k(s, PEER_TOPK)
        cand_s = (top_s[:, :, 0, :, None] + top_s[:, :, 1, None, :]).reshape(PEER_CHUNK, PEER_HEADS, PEER_TOPK * PEER_TOPK)
        cand_i = (top_i[:, :, 0, :, None] * PEER_NKEYS + top_i[:, :, 1, None, :]).reshape(PEER_CHUNK, PEER_HEADS, PEER_TOPK * PEER_TOPK)
        best_s, best_j = lax.top_k(cand_s, PEER_TOPK)
        idx = jnp.take_along_axis(cand_i, best_j, axis=-1)
        gate = jax.nn.softmax(best_s, axis=-1).astype(xc.dtype)
        u_sel = jnp.take(expert_u, idx, axis=0)
        act = jax.nn.gelu(jnp.einsum('chkd,cd->chk', u_sel, xc), approximate=False)
        v_sel = jnp.take(expert_v, idx, axis=0)
        return jnp.einsum('chk,chkd->cd', gate * act, v_sel)

    return lax.map(chunk_fn, chunks).reshape(b, t, d)


def setup_inputs(seed: int = 0) -> dict:
    key = jax.random.key(seed)
    ks = jax.random.split(key, 26)
    L, D = DEPTH, D_MODEL
    nrm = lambda k, shape, s: jax.random.normal(k, shape, jnp.float32) * s
    dt = jnp.exp(jax.random.uniform(ks[8], (L, 2, GDN_HEADS), jnp.float32, math.log(1e-3), math.log(1e-1)))
    return {
        'x': nrm(ks[0], (BATCH, SEQ, D), 1.0),
        'c': nrm(ks[1], (BATCH, D), 1.0),
        'ctx': nrm(ks[2], (BATCH, CTX_LEN, D), 1.0),
        'c_ctx': nrm(ks[3], (D,), 1.0),
        'w_mod': nrm(ks[4], (L, D, 6 * D), 0.5 * D ** -0.5),
        'b_mod': nrm(ks[5], (L, 6 * D), 0.01),
        'norm1_w': 1.0 + nrm(ks[6], (L, D), 0.02),
        'w_in': nrm(ks[7], (L, D, D_IN), D ** -0.5),
        'conv_a_w': nrm(ks[9], (L, 3, PROJ_SIZES[0]), 0.5),
        'a_log': jnp.log(jax.random.uniform(ks[10], (L, 2, GDN_HEADS), jnp.float32, 1.0, 16.0)),
        'dt_bias': dt + jnp.log(-jnp.expm1(-dt)),
        'gdn_norm_w': 1.0 + nrm(ks[11], (L, GDN_DV), 0.02),
        'conv_b_w': nrm(ks[12], (L, 3, SCONV_WIDTH), 0.5),
        'q_norm_w': 1.0 + nrm(ks[13], (L, ATT_HD), 0.02),
        'k_norm_w': 1.0 + nrm(ks[14], (L, ATT_HD), 0.02),
        'w_branch': nrm(ks[15], (L, N_BRANCH, BRANCH_WIDTH, D), BRANCH_WIDTH ** -0.5),
        'w_out': nrm(ks[16], (L, D, D), D ** -0.5),
        'norm2_w': 1.0 + nrm(ks[17], (L, D), 0.02),
        'w_query': nrm(ks[18], (L, D, PEER_HEADS * PEER_DK), D ** -0.5),
        'sub_keys': nrm(ks[19], (L, PEER_HEADS, 2, PEER_NKEYS, PEER_DK // 2), (PEER_DK // 2) ** -0.5),
        'expert_u': nrm(ks[20], (L, PEER_EXPERTS, D), D ** -0.5),
        'expert_v': nrm(ks[21], (L, PEER_EXPERTS, D), PEER_HEADS ** -0.5),
        'final_norm_w': 1.0 + nrm(ks[22], (D,), 0.02),
    }


def reference(x, c, ctx, c_ctx, w_mod, b_mod, norm1_w, w_in, conv_a_w, a_log, dt_bias, gdn_norm_w,
              conv_b_w, q_norm_w, k_norm_w, w_branch, w_out, norm2_w, w_query, sub_keys,
              expert_u, expert_v, final_norm_w):
    seq = x.shape[1]
    n_rows = seq // GRID_W
    row = jnp.repeat(jnp.arange(n_rows, dtype=jnp.float32), GRID_W)
    col = jnp.tile(jnp.arange(GRID_W, dtype=jnp.float32), n_rows)
    h, hc = x, ctx
    for layer in range(DEPTH):
        last = layer == DEPTH - 1
        mod = (jax.nn.silu(c) @ w_mod[layer] + b_mod[layer])[:, None, :]
        mod_c = jax.nn.silu(c_ctx) @ w_mod[layer] + b_mod[layer]
        sh1, sc1, g1, sh2, sc2, g2 = jnp.split(mod, 6, axis=-1)
        csh1, csc1, cg1, csh2, csc2, cg2 = jnp.split(mod_c, 6, axis=-1)
        u = _rmsnorm(h, norm1_w[layer]) * (1 + sc1) + sh1
        uc = _rmsnorm(hc, norm1_w[layer]) * (1 + csc1) + csh1
        y, yc = _token_mixers(u, uc, row, col, w_in[layer], conv_a_w[layer], a_log[layer], dt_bias[layer],
                              gdn_norm_w[layer], conv_b_w[layer], q_norm_w[layer], k_norm_w[layer],
                              w_branch[layer], w_out[layer], not last)
        h = h + g1 * y
        h = h + g2 * _peer(_rmsnorm(h, norm2_w[layer]) * (1 + sc2) + sh2,
                           w_query[layer], sub_keys[layer], expert_u[layer], expert_v[layer])
        if not last:
            hc = hc + cg1 * yc
            hc = hc + cg2 * _peer(_rmsnorm(hc, norm2_w[layer]) * (1 + csc2) + csh2,
                                  w_query[layer], sub_keys[layer], expert_u[layer], expert_v[layer])
    return _rmsnorm(h, final_norm_w)
```

```python
import functools
import math

import jax
import jax.numpy as jnp
from jax import lax
from jax.experimental import pallas as pl
from jax.experimental.pallas import tpu as pltpu

F32 = jnp.float32
BF16 = jnp.bfloat16

D_MODEL = 1024
GRID_W = 64
GDN_HEADS = 4
GDN_DK = 128
GDN_DV = 128
GDN_CHUNK = 64
GDN_W = GDN_HEADS * GDN_DV
SCONV_W = 512
ATT_HEADS = 8
ATT_KV_HEADS = 2
ATT_GROUP = 4
ATT_HD = 64
ROPE_THETA = 10000.0
ROPE_PAIRS = 16
PEER_HEADS = 8
PEER_NKEYS = 128
PEER_TOPK = 16
PEER_TB = 256
NORM_EPS = 1e-6

W_QKV = 3 * GDN_W
W_Z = GDN_W
W_SC = 3 * SCONV_W
W_Q = ATT_HEADS * ATT_HD
W_KV = 2 * ATT_KV_HEADS * ATT_HD
W_G = 3 * D_MODEL
W_AB = 128
INPROJ_WIDTHS = (W_QKV, W_Z, W_SC, W_Q, W_KV, W_G, W_AB)

VMEM_LIMIT = 56 * 1024 * 1024


def _cparams(sem):
    return pltpu.CompilerParams(dimension_semantics=sem, vmem_limit_bytes=VMEM_LIMIT)


def _split_bf16(x):
    hi = x.astype(BF16)
    lo = (x - hi.astype(F32)).astype(BF16)
    return hi, lo


def _dot(a, b):
    return jnp.dot(a, b, preferred_element_type=F32)


def _dot_nt(a, b):
    return lax.dot_general(a, b, (((1,), (1,)), ((), ())), preferred_element_type=F32)


def _dot_tn(a, b):
    return lax.dot_general(a, b, (((0,), (0,)), ((), ())), preferred_element_type=F32)


def _dot3(a, b):
    ah, al = _split_bf16(a)
    bh, bl = _split_bf16(b)
    return _dot(ah, bh) + _dot(ah, bl) + _dot(al, bh)


def _rms_mod(x, nw, shift, scale):
    ms = jnp.mean(x * x, axis=-1, keepdims=True)
    return (x * lax.rsqrt(ms + NORM_EPS) * nw) * (1.0 + scale) + shift


def _mod_body(c_ref, w_ref, b_ref, o_ref):
    c = c_ref[...]
    a = c * jax.nn.sigmoid(c)
    o_ref[...] = _dot3(a, w_ref[...]) + b_ref[...]


def _modulation(c_all, w_mod, b_mod):
    rows, d = c_all.shape
    n = w_mod.shape[1]
    tn = 1536
    return pl.pallas_call(
        _mod_body,
        grid=(n // tn,),
        in_specs=[pl.BlockSpec((rows, d), lambda j: (0, 0)),
                  pl.BlockSpec((d, tn), lambda j: (0, j)),
                  pl.BlockSpec((1, tn), lambda j: (0, j))],
        out_specs=pl.BlockSpec((rows, tn), lambda j: (0, j)),
        out_shape=jax.ShapeDtypeStruct((rows, n), F32),
        compiler_params=_cparams(("parallel",)),
        name="modulation",
    )(c_all, w_mod, b_mod.reshape(1, n))


def _inproj_body(h_ref, mod_ref, nw_ref, w_ref, *outs):
    u = _rms_mod(h_ref[...], nw_ref[...], mod_ref[0, 0:1, :], mod_ref[0, 1:2, :])
    ub = u.astype(BF16)
    off = 0
    for o_ref, wd in zip(outs, INPROJ_WIDTHS):
        o_ref[...] = _dot(ub, w_ref[:, off:off + wd])
        off += wd


def _inproj(h, mod, mod_idx, nw, w):
    nt, d = h.shape
    tm = 256
    wtot = w.shape[1]
    return pl.pallas_call(
        _inproj_body,
        grid=(nt // tm,),
        in_specs=[pl.BlockSpec((tm, d), lambda i: (i, 0)),
                  pl.BlockSpec((1, 6, d), lambda i: (mod_idx(i, tm), 0, 0)),
                  pl.BlockSpec((1, d), lambda i: (0, 0)),
                  pl.BlockSpec((d, wtot), lambda i: (0, 0), pipeline_mode=pl.Buffered(1))],
        out_specs=[pl.BlockSpec((tm, wd), lambda i: (i, 0)) for wd in INPROJ_WIDTHS],
        out_shape=[jax.ShapeDtypeStruct((nt, wd), F32) for wd in INPROJ_WIDTHS],
        compiler_params=_cparams(("parallel",)),
        name="inproj",
    )(h, mod, nw.reshape(1, d), w)


def _tri_inverse_minus_eye(n):
    nb = n.astype(BF16)
    p = _dot(nb, nb)
    y = -n
    for step in range(5):
        y = y + p + _dot(y.astype(BF16), p.astype(BF16))
        if step < 4:
            pb = p.astype(BF16)
            p = _dot(pb, pb)
    return y


def _gdn_prep_body(x_ref, xp_ref, xn_ref, ab_ref, cw_ref, alog_ref, dtb_ref, *outs, cps):
    i = pl.program_id(0)
    first = (i % cps) == 0
    last = (i % cps) == cps - 1
    x = x_ref[...]
    csz = x.shape[0]
    rows = lax.broadcasted_iota(jnp.int32, x.shape, 0)
    prev_row = jnp.where(first, 0.0, xp_ref[7:8, :])
    next_row = jnp.where(last, 0.0, xn_ref[0:1, :])
    xprev = jnp.where(rows == 0, prev_row, pltpu.roll(x, 1, 0))
    xnext = jnp.where(rows == csz - 1, next_row, pltpu.roll(x, csz - 1, 0))
    y = cw_ref[0:1, :] * xprev + cw_ref[1:2, :] * x + cw_ref[2:3, :] * xnext
    y = y * jax.nn.sigmoid(y)

    ab = ab_ref[...]
    zz = ab + dtb_ref[...]
    sp = jnp.maximum(zz, 0.0) + jnp.log1p(jnp.exp(-jnp.abs(zz)))
    g = -jnp.exp(alog_ref[...]) * sp
    beta = jax.nn.sigmoid(ab)

    ri = lax.broadcasted_iota(jnp.int32, (csz, csz), 0)
    ci = lax.broadcasted_iota(jnp.int32, (csz, csz), 1)
    lower = (ri >= ci).astype(BF16)
    g1 = g.astype(BF16)
    r1 = g - g1.astype(F32)
    g2 = r1.astype(BF16)
    g3 = (r1 - g2.astype(F32)).astype(BF16)
    gc_f = _dot(lower, g1) + _dot(lower, g2) + _dot(lower, g3)
    total = gc_f[csz - 1:csz, :]
    gc_b = total - gc_f + g
    col = lax.broadcasted_iota(jnp.int32, gc_f.shape, 1)
    gc = jnp.where(col < GDN_HEADS, gc_f, gc_b)
    gc_t = gc.T
    etot = jnp.exp(total)

    for d in range(2):
        outs[6 * d + 5][...] = jnp.zeros(outs[6 * d + 5].shape, F32)

    for h in range(GDN_HEADS):
        q = y[:, h * GDN_DK:(h + 1) * GDN_DK]
        k = y[:, GDN_W + h * GDN_DK:GDN_W + (h + 1) * GDN_DK]
        v = y[:, 2 * GDN_W + h * GDN_DV:2 * GDN_W + (h + 1) * GDN_DV]
        q = q * lax.rsqrt(jnp.sum(q * q, axis=-1, keepdims=True) + NORM_EPS) * (GDN_DK ** -0.5)
        k = k * lax.rsqrt(jnp.sum(k * k, axis=-1, keepdims=True) + NORM_EPS)
        qb = q.astype(BF16)
        kb = k.astype(BF16)
        kk = _dot_nt(kb, kb)
        qk0 = _dot_nt(qb, kb)
        for d in range(2):
            u_ref, w_ref, kd_ref, qd_ref, qk_ref, eg_ref = outs[6 * d:6 * d + 6]
            c = d * GDN_HEADS + h
            incl = (ri >= ci) if d == 0 else (ri <= ci)
            strict = (ri > ci) if d == 0 else (ri < ci)
            gcol = gc[:, c:c + 1]
            grow = gc_t[c:c + 1, :]
            ldec = jnp.where(incl, jnp.exp(jnp.where(incl, gcol - grow, 0.0)), 0.0)
            bcol = beta[:, 2 * GDN_HEADS + c:2 * GDN_HEADS + c + 1]
            n = jnp.where(strict, kk * bcol * ldec, 0.0)
            yinv = _tri_inverse_minus_eye(n)
            egc = jnp.exp(gcol)
            rhs = jnp.concatenate([v * bcol, k * (bcol * egc)], axis=1)
            sol = rhs + _dot(yinv.astype(BF16), rhs.astype(BF16))
            sl = slice(h * GDN_DV, (h + 1) * GDN_DV)
            u_ref[:, sl] = sol[:, :GDN_DV]
            w_ref[:, sl] = sol[:, GDN_DV:]
            kd_ref[:, sl] = k * jnp.exp(total[:, c:c + 1] - gcol)
            qd_ref[:, sl] = q * egc
            qk_ref[:, h * csz:(h + 1) * csz] = qk0 * ldec
            eg_ref[0, h:h + 1, :] = jnp.broadcast_to(etot[:, c:c + 1], (1, 128))


def _gdn_prep(qkv, ab, conv_w, alog_row, dtb_row, seq_len):
    nt = qkv.shape[0]
    c = GDN_CHUNK
    nch = nt // c
    cps = seq_len // c
    nblk8 = nt // 8
    out_shape, out_specs = [], []
    for _ in range(2):
        for wd in (GDN_W, GDN_W, GDN_W, GDN_W, GDN_HEADS * c):
            out_shape.append(jax.ShapeDtypeStruct((nt, wd), F32))
            out_specs.append(pl.BlockSpec((c, wd), lambda i: (i, 0)))
        out_shape.append(jax.ShapeDtypeStruct((nch, 8, 128), F32))
        out_specs.append(pl.BlockSpec((1, 8, 128), lambda i: (i, 0, 0)))
    return pl.pallas_call(
        functools.partial(_gdn_prep_body, cps=cps),
        grid=(nch,),
        in_specs=[pl.BlockSpec((c, W_QKV), lambda i: (i, 0)),
                  pl.BlockSpec((8, W_QKV), lambda i: (jnp.maximum(i * (c // 8) - 1, 0), 0)),
                  pl.BlockSpec((8, W_QKV), lambda i: (jnp.minimum((i + 1) * (c // 8), nblk8 - 1), 0)),
                  pl.BlockSpec((c, W_AB), lambda i: (i, 0)),
                  pl.BlockSpec((3, W_QKV), lambda i: (0, 0)),
                  pl.BlockSpec((1, 128), lambda i: (0, 0)),
                  pl.BlockSpec((1, 128), lambda i: (0, 0))],
        out_specs=out_specs,
        out_shape=out_shape,
        compiler_params=_cparams(("parallel",)),
        name="gdn_prep",
    )(qkv, qkv, qkv, ab, conv_w, alog_row, dtb_row)


def _gdn_scan_body(*refs, nch):
    ins = refs[:12]
    s0_ref = refs[12]
    o_refs = refs[13:15]
    sfin_ref = refs[15]
    s_scr = refs[16]
    n = pl.program_id(1)
    c = GDN_CHUNK

    @pl.when(n == 0)
    def _():
        s_scr[...] = s0_ref[0]

    for d in range(2):
        u_ref, w_ref, kd_ref, qd_ref, qk_ref, eg_ref = ins[6 * d:6 * d + 6]
        for h in range(GDN_HEADS):
            sl = slice(h * GDN_DV, (h + 1) * GDN_DV)
            s = s_scr[d, h]
            wq = jnp.concatenate([w_ref[:, sl], qd_ref[:, sl]], axis=0).astype(BF16)
            r = _dot(wq, s.astype(BF16))
            v_new = u_ref[:, sl] - r[:c]
            vb = v_new.astype(BF16)
            o = r[c:] + _dot(qk_ref[:, h * c:(h + 1) * c].astype(BF16), vb)
            o_refs[d][:, sl] = o
            s_scr[d, h] = s * eg_ref[0, h:h + 1, :] + _dot_tn(kd_ref[:, sl].astype(BF16), vb)

    @pl.when(n == nch - 1)
    def _():
        sfin_ref[0] = s_scr[...]


def _gdn_scan(prep, s0, batch, seq_len):
    c = GDN_CHUNK
    nch = seq_len // c
    nt = batch * seq_len
    fwd = lambda b, n: (b * nch + n, 0)
    bwd = lambda b, n: (b * nch + (nch - 1 - n), 0)
    in_specs = []
    for d, im in enumerate((fwd, bwd)):
        for wd in (GDN_W, GDN_W, GDN_W, GDN_W, GDN_HEADS * c):
            in_specs.append(pl.BlockSpec((c, wd), im))
        in_specs.append(pl.BlockSpec((1, 8, 128), (lambda im_: (lambda b, n: im_(b, n) + (0,)))(im)))
    st_spec = pl.BlockSpec((1, 2, GDN_HEADS, GDN_DK, GDN_DV), lambda b, n: (b, 0, 0, 0, 0))
    in_specs.append(st_spec)
    o_f, o_b, s_fin = pl.pallas_call(
        functools.partial(_gdn_scan_body, nch=nch),
        grid=(batch, nch),
        in_specs=in_specs,
        out_specs=[pl.BlockSpec((c, GDN_W), fwd), pl.BlockSpec((c, GDN_W), bwd), st_spec],
        out_shape=[jax.ShapeDtypeStruct((nt, GDN_W), F32), jax.ShapeDtypeStruct((nt, GDN_W), F32),
                   jax.ShapeDtypeStruct(s0.shape, F32)],
        scratch_shapes=[pltpu.VMEM((2, GDN_HEADS, GDN_DK, GDN_DV), F32)],
        compiler_params=_cparams(("arbitrary", "arbitrary")),
        name="gdn_scan",
    )(*prep, s0)
    return o_f, o_b, s_fin


def _group_norm_rope(x, w, bd, cos, sin):
    xx = x * x
    hi, lo = _split_bf16(xx)
    ss = _dot(hi, bd) + _dot(lo, bd)
    xn = x * lax.rsqrt(ss * (1.0 / ATT_HD) + NORM_EPS) * w
    if cos is not None:
        width = x.shape[1]
        lane = lax.broadcasted_iota(jnp.int32, x.shape, 1)
        from_below = pltpu.roll(xn, ROPE_PAIRS, 1)
        from_above = pltpu.roll(xn, width - ROPE_PAIRS, 1)
        partner = jnp.where((lane % (2 * ROPE_PAIRS)) < ROPE_PAIRS, from_above, from_below)
        xn = xn * cos + partner * sin
    return xn


def _attn_prep_body(*refs, rope):
    if rope:
        q_ref, kv_ref, cos_ref, sin_ref, qw_ref, kw_ref, bd_ref, qh_ref, kh_ref, vh_ref = refs
        cos, sin = cos_ref[...], sin_ref[...]
        kcos, ksin = cos[:, :ATT_KV_HEADS * ATT_HD], sin[:, :ATT_KV_HEADS * ATT_HD]
    else:
        q_ref, kv_ref, qw_ref, kw_ref, bd_ref, qh_ref, kh_ref, vh_ref = refs
        cos = sin = kcos = ksin = None
    nk = ATT_KV_HEADS * ATT_HD
    bd = bd_ref[...]
    q = _group_norm_rope(q_ref[...], qw_ref[...], bd, cos, sin) * (ATT_HD ** -0.5)
    kv = kv_ref[...]
    k = _group_norm_rope(kv[:, :nk], kw_ref[...], bd[:nk, :nk], kcos, ksin)
    v = kv[:, nk:]
    for j in range(ATT_HEADS):
        qh_ref[j] = q[:, j * ATT_HD:(j + 1) * ATT_HD].astype(BF16)
    for j in range(ATT_KV_HEADS):
        kh_ref[j] = k[:, j * ATT_HD:(j + 1) * ATT_HD].astype(BF16)
        vh_ref[j] = v[:, j * ATT_HD:(j + 1) * ATT_HD].astype(BF16)


def _attn_prep(q, kv, qw_row, kw_row, bd, cos=None, sin=None, seq_len=None):
    nt = q.shape[0]
    tm = 256
    rope = cos is not None
    in_specs = [pl.BlockSpec((tm, W_Q), lambda i: (i, 0)),
                pl.BlockSpec((tm, W_KV), lambda i: (i, 0))]
    args = [q, kv]
    if rope:
        tps = seq_len // tm
        in_specs += [pl.BlockSpec((tm, W_Q), lambda i: (i % tps, 0)),
                     pl.BlockSpec((tm, W_Q), lambda i: (i % tps, 0))]
        args += [cos, sin]
    in_specs += [pl.BlockSpec((1, W_Q), lambda i: (0, 0)),
                 pl.BlockSpec((1, ATT_KV_HEADS * ATT_HD), lambda i: (0, 0)),
                 pl.BlockSpec((W_Q, W_Q), lambda i: (0, 0))]
    args += [qw_row, kw_row, bd]
    return pl.pallas_call(
        functools.partial(_attn_prep_body, rope=rope),
        grid=(nt // tm,),
        in_specs=in_specs,
        out_specs=[pl.BlockSpec((ATT_HEADS, tm, ATT_HD), lambda i: (0, i, 0)),
                   pl.BlockSpec((ATT_KV_HEADS, tm, ATT_HD), lambda i: (0, i, 0)),
                   pl.BlockSpec((ATT_KV_HEADS, tm, ATT_HD), lambda i: (0, i, 0))],
        out_shape=[jax.ShapeDtypeStruct((ATT_HEADS, nt, ATT_HD), BF16),
                   jax.ShapeDtypeStruct((ATT_KV_HEADS, nt, ATT_HD), BF16),
                   jax.ShapeDtypeStruct((ATT_KV_HEADS, nt, ATT_HD), BF16)],
        compiler_params=_cparams(("parallel",)),
        name="attn_prep",
    )(*args)


def _attention_body(q_ref, *refs, src_lens, tk, tq):
    n_src = len(src_lens)
    o_ref = refs[2 * n_src]
    rows = ATT_GROUP * tq
    q = q_ref[...].reshape(rows, ATT_HD)
    m = jnp.full((rows, 1), -jnp.inf, F32)
    l = jnp.zeros((rows, 1), F32)
    acc = jnp.zeros((rows, ATT_HD), F32)
    for s in range(n_src):
        k_ref, v_ref = refs[2 * s], refs[2 * s + 1]

        def body(j, carry, k_ref=k_ref, v_ref=v_ref):
            m, l, acc = carry
            start = pl.multiple_of(j * tk, tk)
            k = k_ref[0, pl.ds(start, tk), :]
            v = v_ref[0, pl.ds(start, tk), :]
            sc = _dot_nt(q, k)
            m_new = jnp.maximum(m, jnp.max(sc, axis=1, keepdims=True))
            alpha = jnp.exp(m - m_new)
            p = jnp.exp(sc - m_new)
            l = alpha * l + jnp.sum(p, axis=1, keepdims=True)
            acc = alpha * acc + _dot(p.astype(BF16), v)
            return m_new, l, acc

        m, l, acc = lax.fori_loop(0, src_lens[s] // tk, body, (m, l, acc))
    o = (acc / l).reshape(ATT_GROUP, tq, ATT_HD)
    o_ref[...] = jnp.concatenate([o[j] for j in range(ATT_GROUP)], axis=1)


def _attention(qh, sources, q_len):
    nq = qh.shape[1]
    tq = 256
    tk = 256
    tpb = q_len // tq
    in_specs = [pl.BlockSpec((ATT_GROUP, tq, ATT_HD), lambda i, g: (g, i, 0))]
    args = [qh]
    for kh, vh, sl in sources:
        spec = pl.BlockSpec((1, sl, ATT_HD), lambda i, g: (g, i // tpb, 0))
        in_specs += [spec, spec]
        args += [kh, vh]
    return pl.pallas_call(
        functools.partial(_attention_body, src_lens=tuple(s[2] for s in sources), tk=tk, tq=tq),
        grid=(nq // tq, ATT_KV_HEADS),
        in_specs=in_specs,
        out_specs=pl.BlockSpec((tq, ATT_GROUP * ATT_HD), lambda i, g: (i, g)),
        out_shape=jax.ShapeDtypeStruct((nq, W_Q), F32),
        compiler_params=_cparams(("parallel", "parallel")),
        name="attention",
    )(*args)


def _merge_body(of_ref, ob_ref, z_ref, p_ref, pp_ref, pn_ref, oc_ref, gt_ref, h_ref, mod_ref,
                gnw_ref, cw_ref, wb_ref, wo_ref, o_ref, *, tps):
    i = pl.program_id(0)
    first = (i % tps) == 0
    last = (i % tps) == tps - 1
    o = of_ref[...] + ob_ref[...]
    z = z_ref[...]
    parts = []
    for h in range(GDN_HEADS):
        sl = slice(h * GDN_DV, (h + 1) * GDN_DV)
        oh = o[:, sl]
        oh = oh * lax.rsqrt(jnp.mean(oh * oh, axis=-1, keepdims=True) + NORM_EPS) * gnw_ref[...]
        zh = z[:, sl]
        parts.append(oh * (zh * jax.nn.sigmoid(zh)))
    br_a = jnp.concatenate(parts, axis=1)
    p = p_ref[...]
    tm = p.shape[0]
    bg = p[:, :SCONV_W]
    cx = p[:, SCONV_W:2 * SCONV_W] * p[:, 2 * SCONV_W:]
    pp = pp_ref[7:8, :]
    pn = pn_ref[0:1, :]
    prev_row = jnp.where(first, 0.0, pp[:, SCONV_W:2 * SCONV_W] * pp[:, 2 * SCONV_W:])
    next_row = jnp.where(last, 0.0, pn[:, SCONV_W:2 * SCONV_W] * pn[:, 2 * SCONV_W:])
    rows = lax.broadcasted_iota(jnp.int32, cx.shape, 0)
    cprev = jnp.where(rows == 0, prev_row, pltpu.roll(cx, 1, 0))
    cnext = jnp.where(rows == tm - 1, next_row, pltpu.roll(cx, tm - 1, 0))
    br_b = bg * (cw_ref[0:1, :] * cprev + cw_ref[1:2, :] * cx + cw_ref[2:3, :] * cnext)
    br_c = oc_ref[...]
    mixed = None
    for n, br in enumerate((br_a, br_b, br_c)):
        t = _dot(br.astype(BF16), wb_ref[n]) * jax.nn.sigmoid(gt_ref[:, n * D_MODEL:(n + 1) * D_MODEL])
        mixed = t if mixed is None else mixed + t
    y = _dot(mixed.astype(BF16), wo_ref[...])
    o_ref[...] = h_ref[...] + mod_ref[0, 2:3, :] * y


def _merge(o_f, o_b, z, psc, oc, gates, h, mod, mod_idx, gnw_row, conv_b_w, wb, wo, seq_len):
    nt, d = h.shape
    tm = 256
    tps = seq_len // tm
    nblk8 = nt // 8
    row = lambda i: (i, 0)
    return pl.pallas_call(
        functools.partial(_merge_body, tps=tps),
        grid=(nt // tm,),
        in_specs=[pl.BlockSpec((tm, GDN_W), row),
                  pl.BlockSpec((tm, GDN_W), row),
                  pl.BlockSpec((tm, W_Z), row),
                  pl.BlockSpec((tm, W_SC), row),
                  pl.BlockSpec((8, W_SC), lambda i: (jnp.maximum(i * (tm // 8) - 1, 0), 0)),
                  pl.BlockSpec((8, W_SC), lambda i: (jnp.minimum((i + 1) * (tm // 8), nblk8 - 1), 0)),
                  pl.BlockSpec((tm, W_Q), row),
                  pl.BlockSpec((tm, W_G), row),
                  pl.BlockSpec((tm, d), row),
                  pl.BlockSpec((1, 6, d), lambda i: (mod_idx(i, tm), 0, 0)),
                  pl.BlockSpec((1, GDN_DV), lambda i: (0, 0)),
                  pl.BlockSpec((3, SCONV_W), lambda i: (0, 0)),
                  pl.BlockSpec((3, SCONV_W, d), lambda i: (0, 0, 0)),
                  pl.BlockSpec((d, d), lambda i: (0, 0))],
        out_specs=pl.BlockSpec((tm, d), row),
        out_shape=jax.ShapeDtypeStruct((nt, d), F32),
        compiler_params=_cparams(("parallel",)),
        name="merge",
    )(o_f, o_b, z, psc, psc, psc, oc, gates, h, mod, gnw_row, conv_b_w, wb, wo)


def _top_rows(s, count):
    rows = []
    for _ in range(count):
        m = jnp.max(s, axis=0, keepdims=True)
        rows.append(m)
        s = jnp.where(s == m, -jnp.inf, s)
    return rows


def _peer_body(h_ref, mod_ref, nw_ref, wqh_ref, wql_ref, skh_ref, skl_ref, u_ref, vt_ref, o_ref,
               xt_scr, out_scr, s0_scr, s1_scr, ea_scr, eb_scr, tau_scr, cand_scr, gw_scr,
               *, tm, ec, nchunks):
    c = pl.program_id(1)
    nk = PEER_NKEYS
    half = nk // 2
    tb_w = PEER_TB
    ntb = tm // tb_w

    @pl.when(c == 0)
    def _():
        u = _rms_mod(h_ref[...], nw_ref[...], mod_ref[0, 3:4, :], mod_ref[0, 4:5, :])
        ut = u.T
        uh, ul = _split_bf16(ut)
        wqh = wqh_ref[...]
        qt = _dot(wqh, uh) + _dot(wqh, ul) + _dot(wql_ref[...], uh)
        for tb in range(ntb):
            xt_scr[tb] = uh[:, tb * tb_w:(tb + 1) * tb_w]
            out_scr[tb] = jnp.zeros(out_scr.shape[1:], F32)
        for h in range(PEER_HEADS):
            tops, scores = [], []
            for p in range(2):
                qs = qt[h * nk + p * half:h * nk + (p + 1) * half, :]
                qh_, ql_ = _split_bf16(qs)
                s = _dot(skh_ref[h, p], qh_) + _dot(skh_ref[h, p], ql_) + _dot(skl_ref[h, p], qh_)
                scores.append(s)
                tops.append(_top_rows(s, PEER_TOPK))
            cand_scr[...] = jnp.full(cand_scr.shape, -jnp.inf, F32)
            r = 0
            for k in range(PEER_TOPK):
                for l in range(PEER_TOPK):
                    if (k + 1) * (l + 1) <= PEER_TOPK:
                        cand_scr[r:r + 1, :] = tops[0][k] + tops[1][l]
                        r += 1
            best = _top_rows(cand_scr[...], PEER_TOPK)
            zsum = jnp.zeros_like(best[0])
            for b_ in best:
                zsum = zsum + jnp.exp(b_ - best[0])
            tau = jnp.broadcast_to(best[PEER_TOPK - 1], (8, tm))
            ea = jnp.exp(scores[0] - tops[0][0]) / zsum
            eb = jnp.exp(scores[1] - tops[1][0])
            for tb in range(ntb):
                ls = slice(tb * tb_w, (tb + 1) * tb_w)
                s0_scr[h, tb] = scores[0][:, ls]
                s1_scr[h, tb] = scores[1][:, ls]
                ea_scr[h, tb] = ea[:, ls]
                eb_scr[h, tb] = eb[:, ls]
                tau_scr[h, tb] = tau[:, ls]

    nblk = ec // nk
    i0 = pl.multiple_of(c * nblk, nblk)
    piece = 32

    def token_block(tb, carry):
        a = _dot(u_ref[...], xt_scr[tb])
        s0r = [s0_scr[h, tb, pl.ds(i0, nblk), :] for h in range(PEER_HEADS)]
        ear = [ea_scr[h, tb, pl.ds(i0, nblk), :] for h in range(PEER_HEADS)]
        for ii in range(nblk):
            for jp in range(nk // piece):
                js = slice(jp * piece, (jp + 1) * piece)
                wm = jnp.zeros((piece, tb_w), F32)
                for h in range(PEER_HEADS):
                    sel = (s1_scr[h, tb, js, :] + s0r[h][ii:ii + 1, :]) >= tau_scr[h, tb, 0:1, :]
                    wm = wm + jnp.where(sel, eb_scr[h, tb, js, :] * ear[h][ii:ii + 1, :], 0.0)
                r0 = ii * nk + jp * piece
                ap = a[r0:r0 + piece, :]
                gl = 0.5 * ap * (1.0 + lax.erf(ap * (0.5 ** 0.5)))
                gw_scr[r0:r0 + piece, :] = (gl * wm).astype(BF16)
        out_scr[tb] += _dot(vt_ref[...], gw_scr[...])
        return carry

    lax.fori_loop(0, ntb, token_block, 0)

    @pl.when(c == nchunks - 1)
    def _():
        out = jnp.concatenate([out_scr[tb] for tb in range(ntb)], axis=1)
        o_ref[...] = h_ref[...] + mod_ref[0, 5:6, :] * out.T


def _peer(h, mod, mod_idx, nw, wq_hi, wq_lo, sk_hi, sk_lo, u_tab, vt_tab):
    nt, d = h.shape
    tm = 512
    ec = 1024
    ne = u_tab.shape[0]
    nchunks = ne // ec
    nk = PEER_NKEYS
    ntb = tm // PEER_TB
    const2 = lambda i, c: (0, 0)
    per_head = pltpu.VMEM((PEER_HEADS, ntb, nk, PEER_TB), F32)
    return pl.pallas_call(
        functools.partial(_peer_body, tm=tm, ec=ec, nchunks=nchunks),
        grid=(nt // tm, nchunks),
        in_specs=[pl.BlockSpec((tm, d), lambda i, c: (i, 0)),
                  pl.BlockSpec((1, 6, d), lambda i, c: (mod_idx(i, tm), 0, 0)),
                  pl.BlockSpec((1, d), const2),
                  pl.BlockSpec((d, d), const2),
                  pl.BlockSpec((d, d), const2),
                  pl.BlockSpec((PEER_HEADS, 2, nk, nk // 2), lambda i, c: (0, 0, 0, 0)),
                  pl.BlockSpec((PEER_HEADS, 2, nk, nk // 2), lambda i, c: (0, 0, 0, 0)),
                  pl.BlockSpec((ec, d), lambda i, c: (c, 0)),
                  pl.BlockSpec((d, ec), lambda i, c: (0, c))],
        out_specs=pl.BlockSpec((tm, d), lambda i, c: (i, 0)),
        out_shape=jax.ShapeDtypeStruct((nt, d), F32),
        scratch_shapes=[pltpu.VMEM((ntb, d, PEER_TB), BF16),
                        pltpu.VMEM((ntb, d, PEER_TB), F32),
                        per_head,
                        per_head,
                        per_head,
                        per_head,
                        pltpu.VMEM((PEER_HEADS, ntb, 8, PEER_TB), F32),
                        pltpu.VMEM((64, tm), F32),
                        pltpu.VMEM((ec, PEER_TB), BF16)],
        compiler_params=_cparams(("parallel", "arbitrary")),
        name="peer",
    )(h, mod, nw.reshape(1, d), wq_hi, wq_lo, sk_hi, sk_lo, u_tab, vt_tab)


def _final_norm_body(h_ref, w_ref, o_ref):
    x = h_ref[...]
    ms = jnp.mean(x * x, axis=-1, keepdims=True)
    o_ref[...] = x * lax.rsqrt(ms + NORM_EPS) * w_ref[...]


def _final_norm(h, w):
    nt, d = h.shape
    tm = 512
    return pl.pallas_call(
        _final_norm_body,
        grid=(nt // tm,),
        in_specs=[pl.BlockSpec((tm, d), lambda i: (i, 0)), pl.BlockSpec((1, d), lambda i: (0, 0))],
        out_specs=pl.BlockSpec((tm, d), lambda i: (i, 0)),
        out_shape=jax.ShapeDtypeStruct((nt, d), F32),
        compiler_params=_cparams(("parallel",)),
        name="final_norm",
    )(h, w.reshape(1, d))


def _rope_tables(seq):
    t = jnp.arange(seq)
    row = (t // GRID_W).astype(F32)
    col = (t % GRID_W).astype(F32)
    freqs = ROPE_THETA ** (-jnp.arange(ROPE_PAIRS, dtype=F32) / ROPE_PAIRS)
    ar = row[:, None] * freqs
    ac = col[:, None] * freqs
    cos = jnp.concatenate([jnp.cos(ar), jnp.cos(ar), jnp.cos(ac), jnp.cos(ac)], axis=1)
    sin = jnp.concatenate([-jnp.sin(ar), jnp.sin(ar), -jnp.sin(ac), jnp.sin(ac)], axis=1)
    return jnp.tile(cos, (1, ATT_HEADS)), jnp.tile(sin, (1, ATT_HEADS))


def _reorder_w_in(w):
    sizes = (W_QKV, W_Z, 4 * GDN_HEADS, W_SC, W_Q, W_KV, W_G)
    parts, start = [], 0
    for s in sizes:
        parts.append(w[:, start:start + s])
        start += s
    qkv, z, ab, sc, q, kv, g = parts
    ab = jnp.pad(ab, ((0, 0), (0, W_AB - ab.shape[1])))
    return jnp.concatenate([qkv, z, sc, q, kv, g, ab], axis=1).astype(BF16)


def _pad_row(v, n=128):
    v = v.reshape(1, -1)
    return jnp.pad(v, ((0, 0), (0, n - v.shape[1])))


def kernel(x, c, ctx, c_ctx, w_mod, b_mod, norm1_w, w_in, conv_a_w, a_log, dt_bias, gdn_norm_w, conv_b_w,
           q_norm_w, k_norm_w, w_branch, w_out, norm2_w, w_query, sub_keys, expert_u, expert_v, final_norm_w):
    batch, seq, d = x.shape
    ctx_len = ctx.shape[1]
    depth = w_mod.shape[0]
    h = x.reshape(batch * seq, d)
    hc = ctx.reshape(batch * ctx_len, d)

    mod_rows = -(-(batch + 1) // 8) * 8
    c_all = jnp.zeros((mod_rows, d), F32).at[:batch].set(c).at[batch].set(c_ctx)
    cos, sin = _rope_tables(seq)
    hd_i = jnp.arange(W_Q) // ATT_HD
    bd = (hd_i[:, None] == hd_i[None, :]).astype(BF16)
    lat_idx = lambda i, tm: i // (seq // tm)
    ctx_idx = lambda i, tm: batch

    for layer in range(depth):
        last = layer == depth - 1
        mod = _modulation(c_all, w_mod[layer], b_mod[layer]).reshape(mod_rows, 6, d)
        w_in_r = _reorder_w_in(w_in[layer])
        alog_row = _pad_row(a_log[layer])
        dtb_row = _pad_row(dt_bias[layer])
        qw_row = jnp.tile(q_norm_w[layer], ATT_HEADS).reshape(1, W_Q)
        kw_row = jnp.tile(k_norm_w[layer], ATT_KV_HEADS).reshape(1, ATT_KV_HEADS * ATT_HD)
        gnw_row = gdn_norm_w[layer].reshape(1, GDN_DV)
        wb = w_branch[layer].astype(BF16)
        wo = w_out[layer].astype(BF16)
        wq_hi, wq_lo = _split_bf16(w_query[layer].T)
        sk_hi, sk_lo = _split_bf16(sub_keys[layer])
        u_tab = expert_u[layer].astype(BF16)
        vt_tab = expert_v[layer].T.astype(BF16)

        pl_ = _inproj(h, mod, lat_idx, norm1_w[layer], w_in_r)
        pc_ = _inproj(hc, mod, ctx_idx, norm1_w[layer], w_in_r)

        prep_c = _gdn_prep(pc_[0], pc_[6], conv_a_w[layer], alog_row, dtb_row, ctx_len)
        prep_l = _gdn_prep(pl_[0], pl_[6], conv_a_w[layer], alog_row, dtb_row, seq)
        s_zero = jnp.zeros((batch, 2, GDN_HEADS, GDN_DK, GDN_DV), F32)
        ocf, ocb, s_ctx = _gdn_scan(prep_c, s_zero, batch, ctx_len)
        olf, olb, _ = _gdn_scan(prep_l, s_ctx, batch, seq)

        qh_c, kh_c, vh_c = _attn_prep(pc_[3], pc_[4], qw_row, kw_row, bd)
        qh_l, kh_l, vh_l = _attn_prep(pl_[3], pl_[4], qw_row, kw_row, bd, cos, sin, seq)
        att_l = _attention(qh_l, [(kh_l, vh_l, seq), (kh_c, vh_c, ctx_len)], seq)

        h = _merge(olf, olb, pl_[1], pl_[2], att_l, pl_[5], h, mod, lat_idx, gnw_row, conv_b_w[layer],
                   wb, wo, seq)
        h = _peer(h, mod, lat_idx, norm2_w[layer], wq_hi, wq_lo, sk_hi, sk_lo, u_tab, vt_tab)
        if not last:
            att_c = _attention(qh_c, [(kh_c, vh_c, ctx_len)], ctx_len)
            hc = _merge(ocf, ocb, pc_[1], pc_[2], att_c, pc_[5], hc, mod, ctx_idx, gnw_row, conv_b_w[layer],
                        wb, wo, ctx_len)
            hc = _peer(hc, mod, ctx_idx, norm2_w[layer], wq_hi, wq_lo, sk_hi, sk_lo, u_tab, vt_tab)
    return _final_norm(h, final_norm_w).reshape(batch, seq, d)
```

```python
import functools
import math

import jax
import jax.numpy as jnp
from jax import lax
from jax.experimental import pallas as pl
from jax.experimental.pallas import tpu as pltpu

F32 = jnp.float32
BF16 = jnp.bfloat16

D_MODEL = 1024
GRID_W = 64
GDN_HEADS = 4
GDN_DK = 128
GDN_DV = 128
GDN_CHUNK = 64
GDN_TILE = 256
GDN_W = GDN_HEADS * GDN_DV
SCONV_W = 512
ATT_HEADS = 8
ATT_KV_HEADS = 2
ATT_GROUP = 4
ATT_HD = 64
ROPE_THETA = 10000.0
ROPE_PAIRS = 16
PEER_HEADS = 8
PEER_NKEYS = 128
PEER_TOPK = 16
PEER_TB = 256
NORM_EPS = 1e-6

W_QKV = 3 * GDN_W
W_Z = GDN_W
W_SC = 3 * SCONV_W
W_Q = ATT_HEADS * ATT_HD
W_KV = 2 * ATT_KV_HEADS * ATT_HD
W_G = 3 * D_MODEL
W_AB = 128
INPROJ_WIDTHS = (W_QKV, W_Z, W_SC, W_Q, W_KV, W_G, W_AB)

VMEM_LIMIT = 56 * 1024 * 1024


def _cparams(sem):
    return pltpu.CompilerParams(dimension_semantics=sem, vmem_limit_bytes=VMEM_LIMIT)


def _split_bf16(x):
    hi = x.astype(BF16)
    lo = (x - hi.astype(F32)).astype(BF16)
    return hi, lo


def _dot(a, b):
    return jnp.dot(a, b, preferred_element_type=F32)


def _dot_nt(a, b):
    return lax.dot_general(a, b, (((1,), (1,)), ((), ())), preferred_element_type=F32)


def _dot_tn(a, b):
    return lax.dot_general(a, b, (((0,), (0,)), ((), ())), preferred_element_type=F32)


def _dot3(a, b):
    ah, al = _split_bf16(a)
    bh, bl = _split_bf16(b)
    return _dot(ah, bh) + _dot(ah, bl) + _dot(al, bh)


def _rms_mod(x, nw, shift, scale):
    ms = jnp.mean(x * x, axis=-1, keepdims=True)
    return (x * lax.rsqrt(ms + NORM_EPS) * nw) * (1.0 + scale) + shift


def _mod_body(c_ref, w_ref, b_ref, o_ref):
    c = c_ref[...]
    a = c * jax.nn.sigmoid(c)
    o_ref[...] = _dot3(a, w_ref[...]) + b_ref[...]


def _modulation(c_all, w_mod, b_mod):
    rows, d = c_all.shape
    n = w_mod.shape[1]
    tn = 1536
    return pl.pallas_call(
        _mod_body,
        grid=(n // tn,),
        in_specs=[pl.BlockSpec((rows, d), lambda j: (0, 0)),
                  pl.BlockSpec((d, tn), lambda j: (0, j)),
                  pl.BlockSpec((1, tn), lambda j: (0, j))],
        out_specs=pl.BlockSpec((rows, tn), lambda j: (0, j)),
        out_shape=jax.ShapeDtypeStruct((rows, n), F32),
        compiler_params=_cparams(("parallel",)),
        name="modulation",
    )(c_all, w_mod, b_mod.reshape(1, n))


def _inproj_body(h_ref, mod_ref, nw_ref, w_ref, *outs):
    u = _rms_mod(h_ref[...], nw_ref[...], mod_ref[0, 0:1, :], mod_ref[0, 1:2, :])
    ub = u.astype(BF16)
    off = 0
    for o_ref, wd in zip(outs, INPROJ_WIDTHS):
        o_ref[...] = _dot(ub, w_ref[:, off:off + wd])
        off += wd


def _inproj(h, mod, mod_idx, nw, w):
    nt, d = h.shape
    tm = 256
    wtot = w.shape[1]
    return pl.pallas_call(
        _inproj_body,
        grid=(nt // tm,),
        in_specs=[pl.BlockSpec((tm, d), lambda i: (i, 0)),
                  pl.BlockSpec((1, 6, d), lambda i: (mod_idx(i, tm), 0, 0)),
                  pl.BlockSpec((1, d), lambda i: (0, 0)),
                  pl.BlockSpec((d, wtot), lambda i: (0, 0), pipeline_mode=pl.Buffered(1))],
        out_specs=[pl.BlockSpec((tm, wd), lambda i: (i, 0)) for wd in INPROJ_WIDTHS],
        out_shape=[jax.ShapeDtypeStruct((nt, wd), F32) for wd in INPROJ_WIDTHS],
        compiler_params=_cparams(("parallel",)),
        name="inproj",
    )(h, mod, nw.reshape(1, d), w)


def _bdot(a, b):
    return lax.dot_general(a, b, (((2,), (1,)), ((0,), (0,))), preferred_element_type=F32)


def _tri_inverse_minus_eye(n):
    nb = n.astype(BF16)
    p = _bdot(nb, nb)
    y = -n
    for step in range(5):
        y = y + p + _bdot(y.astype(BF16), p.astype(BF16))
        if step < 4:
            pb = p.astype(BF16)
            p = _bdot(pb, pb)
    return y


def _gdn_prep_body(x_ref, xp_ref, xn_ref, ab_ref, cw_ref, alog_ref, dtb_ref, *outs, tps):
    i = pl.program_id(0)
    first = (i % tps) == 0
    last = (i % tps) == tps - 1
    x = x_ref[...]
    nrow = x.shape[0]
    c = GDN_CHUNK
    cpt = nrow // c
    rows = lax.broadcasted_iota(jnp.int32, x.shape, 0)
    prev_row = jnp.where(first, 0.0, xp_ref[7:8, :])
    next_row = jnp.where(last, 0.0, xn_ref[0:1, :])
    xprev = jnp.where(rows == 0, prev_row, pltpu.roll(x, 1, 0))
    xnext = jnp.where(rows == nrow - 1, next_row, pltpu.roll(x, nrow - 1, 0))
    y = cw_ref[0:1, :] * xprev + cw_ref[1:2, :] * x + cw_ref[2:3, :] * xnext
    y = y * jax.nn.sigmoid(y)

    ab = ab_ref[...]
    zz = ab + dtb_ref[...]
    sp = jnp.maximum(zz, 0.0) + jnp.log1p(jnp.exp(-jnp.abs(zz)))
    g = -jnp.exp(alog_ref[...]) * sp
    beta = jax.nn.sigmoid(ab)

    ri = lax.broadcasted_iota(jnp.int32, (nrow, nrow), 0)
    ci = lax.broadcasted_iota(jnp.int32, (nrow, nrow), 1)
    same = (ri // c) == (ci // c)
    lower = (same & (ri >= ci)).astype(BF16)
    ones_blk = same.astype(BF16)
    g1 = g.astype(BF16)
    r1 = g - g1.astype(F32)
    g2 = r1.astype(BF16)
    g3 = (r1 - g2.astype(F32)).astype(BF16)
    gc_f = _dot(lower, g1) + _dot(lower, g2) + _dot(lower, g3)
    tot = _dot(ones_blk, g1) + _dot(ones_blk, g2) + _dot(ones_blk, g3)
    gc_b = tot - gc_f + g
    gcf_t = gc_f.T
    gcb_t = gc_b.T
    egc_f = jnp.exp(gc_f)
    egc_b = jnp.exp(gc_b)
    kdec_f = jnp.exp(tot - gc_f)
    kdec_b = jnp.exp(tot - gc_b)
    etot = jnp.exp(tot)

    pi = lax.broadcasted_iota(jnp.int32, (2 * c, 2 * c), 0)
    pj = lax.broadcasted_iota(jnp.int32, (2 * c, 2 * c), 1)
    top = (pi < c) & (pj < c)
    bot = (pi >= c) & (pj >= c)
    incl = (top & (pi >= pj)) | (bot & (pi <= pj))
    strict = (top & (pi > pj)) | (bot & (pi < pj))

    for d in range(2):
        outs[6 * d + 5][...] = jnp.zeros(outs[6 * d + 5].shape, F32)

    n_list, rhs_list, qk_list = [], [], []
    for h in range(GDN_HEADS):
        q = y[:, h * GDN_DK:(h + 1) * GDN_DK]
        k = y[:, GDN_W + h * GDN_DK:GDN_W + (h + 1) * GDN_DK]
        v = y[:, 2 * GDN_W + h * GDN_DV:2 * GDN_W + (h + 1) * GDN_DV]
        q = q * lax.rsqrt(jnp.sum(q * q, axis=-1, keepdims=True) + NORM_EPS) * (GDN_DK ** -0.5)
        k = k * lax.rsqrt(jnp.sum(k * k, axis=-1, keepdims=True) + NORM_EPS)
        cf, cb = h, GDN_HEADS + h
        bf_, bb_ = 2 * GDN_HEADS + h, 3 * GDN_HEADS + h
        sl = slice(h * GDN_DV, (h + 1) * GDN_DV)
        outs[2][:, sl] = k * kdec_f[:, cf:cf + 1]
        outs[8][:, sl] = k * kdec_b[:, cb:cb + 1]
        outs[3][:, sl] = q * egc_f[:, cf:cf + 1]
        outs[9][:, sl] = q * egc_b[:, cb:cb + 1]
        for ch in range(cpt):
            r = slice(ch * c, (ch + 1) * c)
            k2 = jnp.concatenate([k[r], k[r]], axis=0)
            q2 = jnp.concatenate([q[r], q[r]], axis=0)
            v2 = jnp.concatenate([v[r], v[r]], axis=0)
            kb2 = k2.astype(BF16)
            kk2 = _dot_nt(kb2, kb2)
            qk2 = _dot_nt(q2.astype(BF16), kb2)
            gcol = jnp.concatenate([gc_f[r, cf:cf + 1], gc_b[r, cb:cb + 1]], axis=0)
            grow = jnp.concatenate([gcf_t[cf:cf + 1, r], gcb_t[cb:cb + 1, r]], axis=1)
            bcol = jnp.concatenate([beta[r, bf_:bf_ + 1], beta[r, bb_:bb_ + 1]], axis=0)
            ecol = jnp.concatenate([egc_f[r, cf:cf + 1], egc_b[r, cb:cb + 1]], axis=0)
            ldec = jnp.where(incl, jnp.exp(jnp.where(incl, gcol - grow, 0.0)), 0.0)
            n_list.append(jnp.where(strict, kk2 * bcol * ldec, 0.0))
            rhs_list.append(jnp.concatenate([v2 * bcol, k2 * (bcol * ecol)], axis=1))
            qk_list.append(qk2 * ldec)
            for d, col in ((0, cf), (1, cb)):
                outs[6 * d + 5][ch, h:h + 1, :] = jnp.broadcast_to(etot[ch * c:ch * c + 1, col:col + 1], (1, 128))

    yinv = _tri_inverse_minus_eye(jnp.stack(n_list, axis=0))
    rhs = jnp.stack(rhs_list, axis=0)
    sol = rhs + _bdot(yinv.astype(BF16), rhs.astype(BF16))
    for h in range(GDN_HEADS):
        sl = slice(h * GDN_DV, (h + 1) * GDN_DV)
        for ch in range(cpt):
            r = slice(ch * c, (ch + 1) * c)
            s = sol[h * cpt + ch]
            qk = qk_list[h * cpt + ch]
            outs[0][r, sl] = s[:c, :GDN_DV]
            outs[1][r, sl] = s[:c, GDN_DV:]
            outs[6][r, sl] = s[c:, :GDN_DV]
            outs[7][r, sl] = s[c:, GDN_DV:]
            outs[4][r, h * c:(h + 1) * c] = qk[:c, :c]
            outs[10][r, h * c:(h + 1) * c] = qk[c:, c:]


def _gdn_prep(qkv, ab, conv_w, alog_row, dtb_row, seq_len):
    nt = qkv.shape[0]
    c = GDN_CHUNK
    tr = GDN_TILE
    cpt = tr // c
    tps = seq_len // tr
    nblk8 = nt // 8
    out_shape, out_specs = [], []
    for _ in range(2):
        for wd in (GDN_W, GDN_W, GDN_W, GDN_W, GDN_HEADS * c):
            out_shape.append(jax.ShapeDtypeStruct((nt, wd), F32))
            out_specs.append(pl.BlockSpec((tr, wd), lambda i: (i, 0)))
        out_shape.append(jax.ShapeDtypeStruct((nt // c, 8, 128), F32))
        out_specs.append(pl.BlockSpec((cpt, 8, 128), lambda i: (i, 0, 0)))
    return pl.pallas_call(
        functools.partial(_gdn_prep_body, tps=tps),
        grid=(nt // tr,),
        in_specs=[pl.BlockSpec((tr, W_QKV), lambda i: (i, 0)),
                  pl.BlockSpec((8, W_QKV), lambda i: (jnp.maximum(i * (tr // 8) - 1, 0), 0)),
                  pl.BlockSpec((8, W_QKV), lambda i: (jnp.minimum((i + 1) * (tr // 8), nblk8 - 1), 0)),
                  pl.BlockSpec((tr, W_AB), lambda i: (i, 0)),
                  pl.BlockSpec((3, W_QKV), lambda i: (0, 0)),
                  pl.BlockSpec((1, 128), lambda i: (0, 0)),
                  pl.BlockSpec((1, 128), lambda i: (0, 0))],
        out_specs=out_specs,
        out_shape=out_shape,
        compiler_params=_cparams(("parallel",)),
        name="gdn_prep",
    )(qkv, qkv, qkv, ab, conv_w, alog_row, dtb_row)


def _gdn_scan_body(*refs, nch):
    ins = refs[:12]
    s0_ref = refs[12]
    o_refs = refs[13:15]
    sfin_ref = refs[15]
    s_scr = refs[16]
    n = pl.program_id(1)
    c = GDN_CHUNK

    @pl.when(n == 0)
    def _():
        s_scr[...] = s0_ref[0]

    for d in range(2):
        u_ref, w_ref, kd_ref, qd_ref, qk_ref, eg_ref = ins[6 * d:6 * d + 6]
        for h in range(GDN_HEADS):
            sl = slice(h * GDN_DV, (h + 1) * GDN_DV)
            s = s_scr[d, h]
            wq = jnp.concatenate([w_ref[:, sl], qd_ref[:, sl]], axis=0).astype(BF16)
            r = _dot(wq, s.astype(BF16))
            v_new = u_ref[:, sl] - r[:c]
            vb = v_new.astype(BF16)
            o = r[c:] + _dot(qk_ref[:, h * c:(h + 1) * c].astype(BF16), vb)
            o_refs[d][:, sl] = o
            s_scr[d, h] = s * eg_ref[0, h:h + 1, :] + _dot_tn(kd_ref[:, sl].astype(BF16), vb)

    @pl.when(n == nch - 1)
    def _():
        sfin_ref[0] = s_scr[...]


def _gdn_scan(prep, s0, batch, seq_len):
    c = GDN_CHUNK
    nch = seq_len // c
    nt = batch * seq_len
    fwd = lambda b, n: (b * nch + n, 0)
    bwd = lambda b, n: (b * nch + (nch - 1 - n), 0)
    in_specs = []
    for d, im in enumerate((fwd, bwd)):
        for wd in (GDN_W, GDN_W, GDN_W, GDN_W, GDN_HEADS * c):
            in_specs.append(pl.BlockSpec((c, wd), im))
        in_specs.append(pl.BlockSpec((1, 8, 128), (lambda im_: (lambda b, n: im_(b, n) + (0,)))(im)))
    st_spec = pl.BlockSpec((1, 2, GDN_HEADS, GDN_DK, GDN_DV), lambda b, n: (b, 0, 0, 0, 0))
    in_specs.append(st_spec)
    o_f, o_b, s_fin = pl.pallas_call(
        functools.partial(_gdn_scan_body, nch=nch),
        grid=(batch, nch),
        in_specs=in_specs,
        out_specs=[pl.BlockSpec((c, GDN_W), fwd), pl.BlockSpec((c, GDN_W), bwd), st_spec],
        out_shape=[jax.ShapeDtypeStruct((nt, GDN_W), F32), jax.ShapeDtypeStruct((nt, GDN_W), F32),
                   jax.ShapeDtypeStruct(s0.shape, F32)],
        scratch_shapes=[pltpu.VMEM((2, GDN_HEADS, GDN_DK, GDN_DV), F32)],
        compiler_params=_cparams(("arbitrary", "arbitrary")),
        name="gdn_scan",
    )(*prep, s0)
    return o_f, o_b, s_fin


def _group_norm_rope(x, w, bd, cos, sin):
    xx = x * x
    hi, lo = _split_bf16(xx)
    ss = _dot(hi, bd) + _dot(lo, bd)
    xn = x * lax.rsqrt(ss * (1.0 / ATT_HD) + NORM_EPS) * w
    if cos is not None:
        width = x.shape[1]
        lane = lax.broadcasted_iota(jnp.int32, x.shape, 1)
        from_below = pltpu.roll(xn, ROPE_PAIRS, 1)
        from_above = pltpu.roll(xn, width - ROPE_PAIRS, 1)
        partner = jnp.where((lane % (2 * ROPE_PAIRS)) < ROPE_PAIRS, from_above, from_below)
        xn = xn * cos + partner * sin
    return xn


def _attn_prep_body(*refs, rope):
    if rope:
        q_ref, kv_ref, cos_ref, sin_ref, qw_ref, kw_ref, bd_ref, qh_ref, kh_ref, vh_ref = refs
        cos, sin = cos_ref[...], sin_ref[...]
        kcos, ksin = cos[:, :ATT_KV_HEADS * ATT_HD], sin[:, :ATT_KV_HEADS * ATT_HD]
    else:
        q_ref, kv_ref, qw_ref, kw_ref, bd_ref, qh_ref, kh_ref, vh_ref = refs
        cos = sin = kcos = ksin = None
    nk = ATT_KV_HEADS * ATT_HD
    bd = bd_ref[...]
    q = _group_norm_rope(q_ref[...], qw_ref[...], bd, cos, sin) * (ATT_HD ** -0.5 * math.log2(math.e))
    kv = kv_ref[...]
    k = _group_norm_rope(kv[:, :nk], kw_ref[...], bd[:nk, :nk], kcos, ksin)
    v = kv[:, nk:]
    for j in range(ATT_HEADS):
        qh_ref[j] = q[:, j * ATT_HD:(j + 1) * ATT_HD].astype(BF16)
    for j in range(ATT_KV_HEADS):
        kh_ref[j] = k[:, j * ATT_HD:(j + 1) * ATT_HD].astype(BF16)
        vh_ref[j] = v[:, j * ATT_HD:(j + 1) * ATT_HD].astype(BF16)


def _attn_prep(q, kv, qw_row, kw_row, bd, cos=None, sin=None, seq_len=None):
    nt = q.shape[0]
    tm = 256
    rope = cos is not None
    in_specs = [pl.BlockSpec((tm, W_Q), lambda i: (i, 0)),
                pl.BlockSpec((tm, W_KV), lambda i: (i, 0))]
    args = [q, kv]
    if rope:
        tps = seq_len // tm
        in_specs += [pl.BlockSpec((tm, W_Q), lambda i: (i % tps, 0)),
                     pl.BlockSpec((tm, W_Q), lambda i: (i % tps, 0))]
        args += [cos, sin]
    in_specs += [pl.BlockSpec((1, W_Q), lambda i: (0, 0)),
                 pl.BlockSpec((1, ATT_KV_HEADS * ATT_HD), lambda i: (0, 0)),
                 pl.BlockSpec((W_Q, W_Q), lambda i: (0, 0))]
    args += [qw_row, kw_row, bd]
    return pl.pallas_call(
        functools.partial(_attn_prep_body, rope=rope),
        grid=(nt // tm,),
        in_specs=in_specs,
        out_specs=[pl.BlockSpec((ATT_HEADS, tm, ATT_HD), lambda i: (0, i, 0)),
                   pl.BlockSpec((ATT_KV_HEADS, tm, ATT_HD), lambda i: (0, i, 0)),
                   pl.BlockSpec((ATT_KV_HEADS, tm, ATT_HD), lambda i: (0, i, 0))],
        out_shape=[jax.ShapeDtypeStruct((ATT_HEADS, nt, ATT_HD), BF16),
                   jax.ShapeDtypeStruct((ATT_KV_HEADS, nt, ATT_HD), BF16),
                   jax.ShapeDtypeStruct((ATT_KV_HEADS, nt, ATT_HD), BF16)],
        compiler_params=_cparams(("parallel",)),
        name="attn_prep",
    )(*args)


def _attention_body(q_ref, *refs, src_lens, tk, tq):
    n_src = len(src_lens)
    o_ref = refs[2 * n_src]
    rows = ATT_GROUP * tq
    q = q_ref[...].reshape(rows, ATT_HD)

    def block(k, v, carry):
        m, l, acc = carry
        sc = _dot_nt(q, k)
        m_new = jnp.maximum(m, jnp.max(sc, axis=1, keepdims=True))
        alpha = jnp.exp2(m - m_new)
        p = jnp.exp2(sc - m_new)
        l = alpha * l + jnp.sum(p, axis=1, keepdims=True)
        acc = alpha * acc + _dot(p.astype(BF16), v)
        return m_new, l, acc

    carry = (jnp.full((rows, 1), -jnp.inf, F32), jnp.zeros((rows, 1), F32), jnp.zeros((rows, ATT_HD), F32))
    for s in range(n_src):
        k_ref, v_ref = refs[2 * s], refs[2 * s + 1]
        blk = min(tk, src_lens[s])
        nblk = src_lens[s] // blk
        for j in range(nblk):
            carry = block(k_ref[0, j * blk:(j + 1) * blk, :], v_ref[0, j * blk:(j + 1) * blk, :], carry)
    m, l, acc = carry
    o = (acc / l).reshape(ATT_GROUP, tq, ATT_HD)
    o_ref[...] = jnp.concatenate([o[j] for j in range(ATT_GROUP)], axis=1)


def _attention(qh, sources, q_len):
    nq = qh.shape[1]
    tq = 128
    tk = 2048
    tpb = q_len // tq
    in_specs = [pl.BlockSpec((ATT_GROUP, tq, ATT_HD), lambda i, g: (g, i, 0))]
    args = [qh]
    for kh, vh, sl in sources:
        spec = pl.BlockSpec((1, sl, ATT_HD), lambda i, g: (g, i // tpb, 0))
        in_specs += [spec, spec]
        args += [kh, vh]
    return pl.pallas_call(
        functools.partial(_attention_body, src_lens=tuple(s[2] for s in sources), tk=tk, tq=tq),
        grid=(nq // tq, ATT_KV_HEADS),
        in_specs=in_specs,
        out_specs=pl.BlockSpec((tq, ATT_GROUP * ATT_HD), lambda i, g: (i, g)),
        out_shape=jax.ShapeDtypeStruct((nq, W_Q), F32),
        compiler_params=_cparams(("parallel", "parallel")),
        name="attention",
    )(*args)


def _merge_body(of_ref, ob_ref, z_ref, p_ref, pp_ref, pn_ref, oc_ref, gt_ref, h_ref, mod_ref,
                gnw_ref, cw_ref, wb_ref, wo_ref, o_ref, *, tps):
    i = pl.program_id(0)
    first = (i % tps) == 0
    last = (i % tps) == tps - 1
    o = of_ref[...] + ob_ref[...]
    z = z_ref[...]
    parts = []
    for h in range(GDN_HEADS):
        sl = slice(h * GDN_DV, (h + 1) * GDN_DV)
        oh = o[:, sl]
        oh = oh * lax.rsqrt(jnp.mean(oh * oh, axis=-1, keepdims=True) + NORM_EPS) * gnw_ref[...]
        zh = z[:, sl]
        parts.append(oh * (zh * jax.nn.sigmoid(zh)))
    br_a = jnp.concatenate(parts, axis=1)
    p = p_ref[...]
    tm = p.shape[0]
    bg = p[:, :SCONV_W]
    cx = p[:, SCONV_W:2 * SCONV_W] * p[:, 2 * SCONV_W:]
    pp = pp_ref[7:8, :]
    pn = pn_ref[0:1, :]
    prev_row = jnp.where(first, 0.0, pp[:, SCONV_W:2 * SCONV_W] * pp[:, 2 * SCONV_W:])
    next_row = jnp.where(last, 0.0, pn[:, SCONV_W:2 * SCONV_W] * pn[:, 2 * SCONV_W:])
    rows = lax.broadcasted_iota(jnp.int32, cx.shape, 0)
    cprev = jnp.where(rows == 0, prev_row, pltpu.roll(cx, 1, 0))
    cnext = jnp.where(rows == tm - 1, next_row, pltpu.roll(cx, tm - 1, 0))
    br_b = bg * (cw_ref[0:1, :] * cprev + cw_ref[1:2, :] * cx + cw_ref[2:3, :] * cnext)
    br_c = oc_ref[...]
    mixed = None
    for n, br in enumerate((br_a, br_b, br_c)):
        t = _dot(br.astype(BF16), wb_ref[n]) * jax.nn.sigmoid(gt_ref[:, n * D_MODEL:(n + 1) * D_MODEL])
        mixed = t if mixed is None else mixed + t
    y = _dot(mixed.astype(BF16), wo_ref[...])
    o_ref[...] = h_ref[...] + mod_ref[0, 2:3, :] * y


def _merge(o_f, o_b, z, psc, oc, gates, h, mod, mod_idx, gnw_row, conv_b_w, wb, wo, seq_len):
    nt, d = h.shape
    tm = 256
    tps = seq_len // tm
    nblk8 = nt // 8
    row = lambda i: (i, 0)
    return pl.pallas_call(
        functools.partial(_merge_body, tps=tps),
        grid=(nt // tm,),
        in_specs=[pl.BlockSpec((tm, GDN_W), row),
                  pl.BlockSpec((tm, GDN_W), row),
                  pl.BlockSpec((tm, W_Z), row),
                  pl.BlockSpec((tm, W_SC), row),
                  pl.BlockSpec((8, W_SC), lambda i: (jnp.maximum(i * (tm // 8) - 1, 0), 0)),
                  pl.BlockSpec((8, W_SC), lambda i: (jnp.minimum((i + 1) * (tm // 8), nblk8 - 1), 0)),
                  pl.BlockSpec((tm, W_Q), row),
                  pl.BlockSpec((tm, W_G), row),
                  pl.BlockSpec((tm, d), row),
                  pl.BlockSpec((1, 6, d), lambda i: (mod_idx(i, tm), 0, 0)),
                  pl.BlockSpec((1, GDN_DV), lambda i: (0, 0)),
                  pl.BlockSpec((3, SCONV_W), lambda i: (0, 0)),
                  pl.BlockSpec((3, SCONV_W, d), lambda i: (0, 0, 0)),
                  pl.BlockSpec((d, d), lambda i: (0, 0))],
        out_specs=pl.BlockSpec((tm, d), row),
        out_shape=jax.ShapeDtypeStruct((nt, d), F32),
        compiler_params=_cparams(("parallel",)),
        name="merge",
    )(o_f, o_b, z, psc, psc, psc, oc, gates, h, mod, gnw_row, conv_b_w, wb, wo)


PEER_NO_RANK = 64.0


def _top_rows(s, count):
    rows = []
    rank = jnp.full(s.shape, PEER_NO_RANK, F32)
    for r in range(count):
        m = jnp.max(s, axis=0, keepdims=True)
        rows.append(m)
        hit = s == m
        rank = jnp.where(hit, float(r + 1), rank)
        s = jnp.where(hit, -jnp.inf, s)
    return rows, rank


def _peer_body(h_ref, mod_ref, nw_ref, wqh_ref, wql_ref, skh_ref, skl_ref, u_ref, vt_ref, o_ref,
               xt_scr, out_scr, sc_scr, cnt_scr, ea_scr, rank_scr, eb_scr, cand_scr, gw_scr,
               *, tm, ec, nchunks):
    c = pl.program_id(1)
    nk = PEER_NKEYS
    half = nk // 2
    tb_w = PEER_TB
    ntb = tm // tb_w
    nlb = tm // 128
    lb_per_tb = tb_w // 128

    @pl.when(c == 0)
    def _():
        u = _rms_mod(h_ref[...], nw_ref[...], mod_ref[0, 3:4, :], mod_ref[0, 4:5, :])
        ut = u.T
        uh, ul = _split_bf16(ut)
        wqh = wqh_ref[...]
        qt = _dot(wqh, uh) + _dot(wqh, ul) + _dot(wql_ref[...], uh)
        for tb in range(ntb):
            xt_scr[tb] = uh[:, tb * tb_w:(tb + 1) * tb_w]
            out_scr[tb] = jnp.zeros(out_scr.shape[1:], F32)
        for h in range(PEER_HEADS):
            for p in range(2):
                qs = qt[h * nk + p * half:h * nk + (p + 1) * half, :]
                qh_, ql_ = _split_bf16(qs)
                s = _dot(skh_ref[h, p], qh_) + _dot(skh_ref[h, p], ql_) + _dot(skl_ref[h, p], qh_)
                for lb in range(nlb):
                    sc_scr[p, h * nlb + lb] = s[:, lb * 128:(lb + 1) * 128]

        def select(hl, carry):
            s0 = sc_scr[0, hl]
            s1 = sc_scr[1, hl]
            top0, rank0 = _top_rows(s0, PEER_TOPK)
            top1, rank1 = _top_rows(s1, PEER_TOPK)
            pairs = [(k, l) for k in range(PEER_TOPK) for l in range(PEER_TOPK) if (k + 1) * (l + 1) <= PEER_TOPK]
            cand_scr[...] = jnp.full(cand_scr.shape, -jnp.inf, F32)
            sums = {}
            for r, (k, l) in enumerate(pairs):
                sums[k, l] = top0[k] + top1[l]
                cand_scr[r:r + 1, :] = sums[k, l]
            best, _ = _top_rows(cand_scr[...], PEER_TOPK)
            tau = best[PEER_TOPK - 1]
            zsum = jnp.zeros_like(tau)
            for b_ in best:
                zsum = zsum + jnp.exp(b_ - best[0])
            cnt = jnp.zeros(s0.shape, F32)
            for k in range(PEER_TOPK):
                n_sel = jnp.zeros_like(tau)
                for l in range(PEER_TOPK // (k + 1)):
                    n_sel = n_sel + jnp.where(sums[k, l] >= tau, 1.0, 0.0)
                cnt = jnp.where(rank0 == float(k + 1), n_sel, cnt)
            cnt_scr[hl] = cnt.reshape(nk // 8, 8, 128)
            ea_scr[hl] = (jnp.exp(s0 - top0[0]) / zsum).reshape(nk // 8, 8, 128)
            rank_scr[hl] = rank1.astype(BF16)
            eb_scr[hl] = jnp.exp(s1 - top1[0]).astype(BF16)
            return carry

        lax.fori_loop(0, PEER_HEADS * nlb, select, 0)

    nblk = ec // nk
    piece = 16
    npiece = nk // piece

    def half_dots(lhs_ref, rhs):
        hm = lhs_ref.shape[0] // 2
        return [_dot(lhs_ref[0:hm, :], rhs), _dot(lhs_ref[hm:2 * hm, :], rhs)]

    def gated_activation(tb, acts):
        hm = ec // 2
        for sub in range(lb_per_tb):
            lb = tb * lb_per_tb + sub
            ls = slice(sub * 128, (sub + 1) * 128)
            for ii in range(nblk):
                wm = [jnp.zeros((piece, 128), BF16) for _ in range(npiece)]
                for h in range(PEER_HEADS):
                    cnt_row = jnp.broadcast_to(cnt_scr[h * nlb + lb, c, ii:ii + 1, :], (piece, 128)).astype(BF16)
                    ea_row = jnp.broadcast_to(ea_scr[h * nlb + lb, c, ii:ii + 1, :], (piece, 128)).astype(BF16)
                    for jp in range(npiece):
                        js = slice(jp * piece, (jp + 1) * piece)
                        sel = rank_scr[h * nlb + lb, js, :] <= cnt_row
                        w = eb_scr[h * nlb + lb, js, :] * ea_row
                        wm[jp] = wm[jp] + jnp.where(sel, w, jnp.zeros_like(w))
                for jp in range(npiece):
                    r0 = ii * nk + jp * piece
                    ap = acts[r0 // hm][r0 % hm:r0 % hm + piece, ls]
                    gl = 0.5 * ap * (1.0 + lax.erf(ap * (0.5 ** 0.5)))
                    gw_scr[tb, r0:r0 + piece, ls] = gl.astype(BF16) * wm[jp]

    for tb in range(ntb):
        acts = half_dots(u_ref, xt_scr[tb])
        gated_activation(tb, acts)
        outs = half_dots(vt_ref, gw_scr[tb])
        hd = out_scr.shape[1] // 2
        out_scr[tb, 0:hd, :] += outs[0]
        out_scr[tb, hd:2 * hd, :] += outs[1]

    @pl.when(c == nchunks - 1)
    def _():
        out = jnp.concatenate([out_scr[tb] for tb in range(ntb)], axis=1)
        o_ref[...] = h_ref[...] + mod_ref[0, 5:6, :] * out.T


def _peer(h, mod, mod_idx, nw, wq_hi, wq_lo, sk_hi, sk_lo, u_tab, vt_tab):
    nt, d = h.shape
    tm = 512
    ec = 1024
    ne = u_tab.shape[0]
    nchunks = ne // ec
    nk = PEER_NKEYS
    ntb = tm // PEER_TB
    nlb = tm // 128
    const2 = lambda i, c: (0, 0)
    per_head = lambda dt: pltpu.VMEM((PEER_HEADS * nlb, nk, 128), dt)
    per_head_rows = pltpu.VMEM((PEER_HEADS * nlb, nk // 8, 8, 128), F32)
    return pl.pallas_call(
        functools.partial(_peer_body, tm=tm, ec=ec, nchunks=nchunks),
        grid=(nt // tm, nchunks),
        in_specs=[pl.BlockSpec((tm, d), lambda i, c: (i, 0)),
                  pl.BlockSpec((1, 6, d), lambda i, c: (mod_idx(i, tm), 0, 0)),
                  pl.BlockSpec((1, d), const2),
                  pl.BlockSpec((d, d), const2),
                  pl.BlockSpec((d, d), const2),
                  pl.BlockSpec((PEER_HEADS, 2, nk, nk // 2), lambda i, c: (0, 0, 0, 0)),
                  pl.BlockSpec((PEER_HEADS, 2, nk, nk // 2), lambda i, c: (0, 0, 0, 0)),
                  pl.BlockSpec((ec, d), lambda i, c: (c, 0)),
                  pl.BlockSpec((d, ec), lambda i, c: (0, c))],
        out_specs=pl.BlockSpec((tm, d), lambda i, c: (i, 0)),
        out_shape=jax.ShapeDtypeStruct((nt, d), F32),
        scratch_shapes=[pltpu.VMEM((ntb, d, PEER_TB), BF16),
                        pltpu.VMEM((ntb, d, PEER_TB), F32),
                        pltpu.VMEM((2, PEER_HEADS * nlb, nk, 128), F32),
                        per_head_rows,
                        per_head_rows,
                        per_head(BF16),
                        per_head(BF16),
                        pltpu.VMEM((64, 128), F32),
                        pltpu.VMEM((ntb, ec, PEER_TB), BF16)],
        compiler_params=_cparams(("parallel", "arbitrary")),
        name="peer",
    )(h, mod, nw.reshape(1, d), wq_hi, wq_lo, sk_hi, sk_lo, u_tab, vt_tab)


def _final_norm_body(h_ref, w_ref, o_ref):
    x = h_ref[...]
    ms = jnp.mean(x * x, axis=-1, keepdims=True)
    o_ref[...] = x * lax.rsqrt(ms + NORM_EPS) * w_ref[...]


def _final_norm(h, w):
    nt, d = h.shape
    tm = 512
    return pl.pallas_call(
        _final_norm_body,
        grid=(nt // tm,),
        in_specs=[pl.BlockSpec((tm, d), lambda i: (i, 0)), pl.BlockSpec((1, d), lambda i: (0, 0))],
        out_specs=pl.BlockSpec((tm, d), lambda i: (i, 0)),
        out_shape=jax.ShapeDtypeStruct((nt, d), F32),
        compiler_params=_cparams(("parallel",)),
        name="final_norm",
    )(h, w.reshape(1, d))


def _rope_tables(seq):
    t = jnp.arange(seq)
    row = (t // GRID_W).astype(F32)
    col = (t % GRID_W).astype(F32)
    freqs = ROPE_THETA ** (-jnp.arange(ROPE_PAIRS, dtype=F32) / ROPE_PAIRS)
    ar = row[:, None] * freqs
    ac = col[:, None] * freqs
    cos = jnp.concatenate([jnp.cos(ar), jnp.cos(ar), jnp.cos(ac), jnp.cos(ac)], axis=1)
    sin = jnp.concatenate([-jnp.sin(ar), jnp.sin(ar), -jnp.sin(ac), jnp.sin(ac)], axis=1)
    return jnp.tile(cos, (1, ATT_HEADS)), jnp.tile(sin, (1, ATT_HEADS))


def _reorder_w_in(w):
    sizes = (W_QKV, W_Z, 4 * GDN_HEADS, W_SC, W_Q, W_KV, W_G)
    parts, start = [], 0
    for s in sizes:
        parts.append(w[:, start:start + s])
        start += s
    qkv, z, ab, sc, q, kv, g = parts
    ab = jnp.pad(ab, ((0, 0), (0, W_AB - ab.shape[1])))
    return jnp.concatenate([qkv, z, sc, q, kv, g, ab], axis=1).astype(BF16)


def _pad_row(v, n=128):
    v = v.reshape(1, -1)
    return jnp.pad(v, ((0, 0), (0, n - v.shape[1])))


def kernel(x, c, ctx, c_ctx, w_mod, b_mod, norm1_w, w_in, conv_a_w, a_log, dt_bias, gdn_norm_w, conv_b_w,
           q_norm_w, k_norm_w, w_branch, w_out, norm2_w, w_query, sub_keys, expert_u, expert_v, final_norm_w):
    batch, seq, d = x.shape
    ctx_len = ctx.shape[1]
    depth = w_mod.shape[0]
    h = x.reshape(batch * seq, d)
    hc = ctx.reshape(batch * ctx_len, d)

    mod_rows = -(-(batch + 1) // 8) * 8
    c_all = jnp.zeros((mod_rows, d), F32).at[:batch].set(c).at[batch].set(c_ctx)
    cos, sin = _rope_tables(seq)
    hd_i = jnp.arange(W_Q) // ATT_HD
    bd = (hd_i[:, None] == hd_i[None, :]).astype(BF16)
    lat_idx = lambda i, tm: i // (seq // tm)
    ctx_idx = lambda i, tm: batch

    for layer in range(depth):
        last = layer == depth - 1
        mod = _modulation(c_all, w_mod[layer], b_mod[layer]).reshape(mod_rows, 6, d)
        w_in_r = _reorder_w_in(w_in[layer])
        alog_row = _pad_row(a_log[layer])
        dtb_row = _pad_row(dt_bias[layer])
        qw_row = jnp.tile(q_norm_w[layer], ATT_HEADS).reshape(1, W_Q)
        kw_row = jnp.tile(k_norm_w[layer], ATT_KV_HEADS).reshape(1, ATT_KV_HEADS * ATT_HD)
        gnw_row = gdn_norm_w[layer].reshape(1, GDN_DV)
        wb = w_branch[layer].astype(BF16)
        wo = w_out[layer].astype(BF16)
        wq_hi, wq_lo = _split_bf16(w_query[layer].T)
        sk_hi, sk_lo = _split_bf16(sub_keys[layer])
        u_tab = expert_u[layer].astype(BF16)
        vt_tab = expert_v[layer].T.astype(BF16)

        pl_ = _inproj(h, mod, lat_idx, norm1_w[layer], w_in_r)
        pc_ = _inproj(hc, mod, ctx_idx, norm1_w[layer], w_in_r)

        prep_c = _gdn_prep(pc_[0], pc_[6], conv_a_w[layer], alog_row, dtb_row, ctx_len)
        prep_l = _gdn_prep(pl_[0], pl_[6], conv_a_w[layer], alog_row, dtb_row, seq)
        s_zero = jnp.zeros((batch, 2, GDN_HEADS, GDN_DK, GDN_DV), F32)
        ocf, ocb, s_ctx = _gdn_scan(prep_c, s_zero, batch, ctx_len)
        olf, olb, _ = _gdn_scan(prep_l, s_ctx, batch, seq)

        qh_c, kh_c, vh_c = _attn_prep(pc_[3], pc_[4], qw_row, kw_row, bd)
        qh_l, kh_l, vh_l = _attn_prep(pl_[3], pl_[4], qw_row, kw_row, bd, cos, sin, seq)
        att_l = _attention(qh_l, [(kh_l, vh_l, seq), (kh_c, vh_c, ctx_len)], seq)

        h = _merge(olf, olb, pl_[1], pl_[2], att_l, pl_[5], h, mod, lat_idx, gnw_row, conv_b_w[layer],
                   wb, wo, seq)
        h = _peer(h, mod, lat_idx, norm2_w[layer], wq_hi, wq_lo, sk_hi, sk_lo, u_tab, vt_tab)
        if not last:
            att_c = _attention(qh_c, [(kh_c, vh_c, ctx_len)], ctx_len)
            hc = _merge(ocf, ocb, pc_[1], pc_[2], att_c, pc_[5], hc, mod, ctx_idx, gnw_row, conv_b_w[layer],
                        wb, wo, ctx_len)
            hc = _peer(hc, mod, ctx_idx, norm2_w[layer], wq_hi, wq_lo, sk_hi, sk_lo, u_tab, vt_tab)
    return _final_norm(h, final_norm_w).reshape(batch, seq, d)
```

```python
import functools
import math

import jax
import jax.numpy as jnp
from jax import lax
from jax.experimental import pallas as pl
from jax.experimental.pallas import tpu as pltpu

F32 = jnp.float32
BF16 = jnp.bfloat16

D_MODEL = 1024
GRID_W = 64
GDN_HEADS = 4
GDN_DK = 128
GDN_DV = 128
GDN_CHUNK = 64
GDN_TILE = 256
GDN_W = GDN_HEADS * GDN_DV
SCONV_W = 512
ATT_HEADS = 8
ATT_KV_HEADS = 2
ATT_GROUP = 4
ATT_HD = 64
ROPE_THETA = 10000.0
ROPE_PAIRS = 16
PEER_HEADS = 8
PEER_NKEYS = 128
PEER_TOPK = 16
PEER_TB = 256
NORM_EPS = 1e-6

W_QKV = 3 * GDN_W
W_Z = GDN_W
W_SC = 3 * SCONV_W
W_Q = ATT_HEADS * ATT_HD
W_KV = 2 * ATT_KV_HEADS * ATT_HD
W_G = 3 * D_MODEL
W_AB = 128
INPROJ_WIDTHS = (W_QKV, W_Z, W_SC, W_Q, W_KV, W_G, W_AB)

VMEM_LIMIT = 56 * 1024 * 1024


def _cparams(sem):
    return pltpu.CompilerParams(dimension_semantics=sem, vmem_limit_bytes=VMEM_LIMIT)


def _split_bf16(x):
    hi = x.astype(BF16)
    lo = (x - hi.astype(F32)).astype(BF16)
    return hi, lo


def _dot(a, b):
    return jnp.dot(a, b, preferred_element_type=F32)


def _dot_nt(a, b):
    return lax.dot_general(a, b, (((1,), (1,)), ((), ())), preferred_element_type=F32)


def _dot_tn(a, b):
    return lax.dot_general(a, b, (((0,), (0,)), ((), ())), preferred_element_type=F32)


def _dot3(a, b):
    ah, al = _split_bf16(a)
    bh, bl = _split_bf16(b)
    return _dot(ah, bh) + _dot(ah, bl) + _dot(al, bh)


def _rms_mod(x, nw, shift, scale):
    ms = jnp.mean(x * x, axis=-1, keepdims=True)
    return (x * lax.rsqrt(ms + NORM_EPS) * nw) * (1.0 + scale) + shift


def _mod_body(c_ref, w_ref, b_ref, o_ref):
    c = c_ref[...]
    a = c * jax.nn.sigmoid(c)
    o_ref[...] = _dot3(a, w_ref[...]) + b_ref[...]


def _modulation(c_all, w_mod, b_mod):
    rows, d = c_all.shape
    n = w_mod.shape[1]
    tn = 1536
    return pl.pallas_call(
        _mod_body,
        grid=(n // tn,),
        in_specs=[pl.BlockSpec((rows, d), lambda j: (0, 0)),
                  pl.BlockSpec((d, tn), lambda j: (0, j)),
                  pl.BlockSpec((1, tn), lambda j: (0, j))],
        out_specs=pl.BlockSpec((rows, tn), lambda j: (0, j)),
        out_shape=jax.ShapeDtypeStruct((rows, n), F32),
        compiler_params=_cparams(("parallel",)),
        name="modulation",
    )(c_all, w_mod, b_mod.reshape(1, n))


def _inproj_body(h_ref, mod_ref, nw_ref, w_ref, *outs):
    u = _rms_mod(h_ref[...], nw_ref[...], mod_ref[0, 0:1, :], mod_ref[0, 1:2, :])
    ub = u.astype(BF16)
    off = 0
    for o_ref, wd in zip(outs, INPROJ_WIDTHS):
        o_ref[...] = _dot(ub, w_ref[:, off:off + wd])
        off += wd


def _inproj(h, mod, mod_idx, nw, w):
    nt, d = h.shape
    tm = 256
    wtot = w.shape[1]
    return pl.pallas_call(
        _inproj_body,
        grid=(nt // tm,),
        in_specs=[pl.BlockSpec((tm, d), lambda i: (i, 0)),
                  pl.BlockSpec((1, 6, d), lambda i: (mod_idx(i, tm), 0, 0)),
                  pl.BlockSpec((1, d), lambda i: (0, 0)),
                  pl.BlockSpec((d, wtot), lambda i: (0, 0), pipeline_mode=pl.Buffered(1))],
        out_specs=[pl.BlockSpec((tm, wd), lambda i: (i, 0)) for wd in INPROJ_WIDTHS],
        out_shape=[jax.ShapeDtypeStruct((nt, wd), F32) for wd in INPROJ_WIDTHS],
        compiler_params=_cparams(("parallel",)),
        name="inproj",
    )(h, mod, nw.reshape(1, d), w)


def _bdot(a, b):
    return lax.dot_general(a, b, (((2,), (1,)), ((0,), (0,))), preferred_element_type=F32)


def _tri_inverse_minus_eye(n):
    nb = n.astype(BF16)
    p = _bdot(nb, nb)
    y = -n
    for step in range(5):
        y = y + p + _bdot(y.astype(BF16), p.astype(BF16))
        if step < 4:
            pb = p.astype(BF16)
            p = _bdot(pb, pb)
    return y


def _gdn_prep_body(x_ref, xp_ref, xn_ref, ab_ref, cw_ref, alog_ref, dtb_ref, *outs, tps):
    i = pl.program_id(0)
    first = (i % tps) == 0
    last = (i % tps) == tps - 1
    x = x_ref[...]
    nrow = x.shape[0]
    c = GDN_CHUNK
    cpt = nrow // c
    rows = lax.broadcasted_iota(jnp.int32, x.shape, 0)
    prev_row = jnp.where(first, 0.0, xp_ref[7:8, :])
    next_row = jnp.where(last, 0.0, xn_ref[0:1, :])
    xprev = jnp.where(rows == 0, prev_row, pltpu.roll(x, 1, 0))
    xnext = jnp.where(rows == nrow - 1, next_row, pltpu.roll(x, nrow - 1, 0))
    y = cw_ref[0:1, :] * xprev + cw_ref[1:2, :] * x + cw_ref[2:3, :] * xnext
    y = y * jax.nn.sigmoid(y)

    ab = ab_ref[...]
    zz = ab + dtb_ref[...]
    sp = jnp.maximum(zz, 0.0) + jnp.log1p(jnp.exp(-jnp.abs(zz)))
    g = -jnp.exp(alog_ref[...]) * sp
    beta = jax.nn.sigmoid(ab)

    ri = lax.broadcasted_iota(jnp.int32, (nrow, nrow), 0)
    ci = lax.broadcasted_iota(jnp.int32, (nrow, nrow), 1)
    same = (ri // c) == (ci // c)
    lower = (same & (ri >= ci)).astype(BF16)
    ones_blk = same.astype(BF16)
    g1 = g.astype(BF16)
    r1 = g - g1.astype(F32)
    g2 = r1.astype(BF16)
    g3 = (r1 - g2.astype(F32)).astype(BF16)
    gc_f = _dot(lower, g1) + _dot(lower, g2) + _dot(lower, g3)
    tot = _dot(ones_blk, g1) + _dot(ones_blk, g2) + _dot(ones_blk, g3)
    gc_b = tot - gc_f + g
    gcf_t = gc_f.T
    gcb_t = gc_b.T
    egc_f = jnp.exp(gc_f)
    egc_b = jnp.exp(gc_b)
    kdec_f = jnp.exp(tot - gc_f)
    kdec_b = jnp.exp(tot - gc_b)
    etot = jnp.exp(tot)

    pi = lax.broadcasted_iota(jnp.int32, (2 * c, 2 * c), 0)
    pj = lax.broadcasted_iota(jnp.int32, (2 * c, 2 * c), 1)
    top = (pi < c) & (pj < c)
    bot = (pi >= c) & (pj >= c)
    incl = (top & (pi >= pj)) | (bot & (pi <= pj))
    strict = (top & (pi > pj)) | (bot & (pi < pj))

    for d in range(2):
        outs[6 * d + 5][...] = jnp.zeros(outs[6 * d + 5].shape, F32)

    n_list, rhs_list, qk_list = [], [], []
    for h in range(GDN_HEADS):
        q = y[:, h * GDN_DK:(h + 1) * GDN_DK]
        k = y[:, GDN_W + h * GDN_DK:GDN_W + (h + 1) * GDN_DK]
        v = y[:, 2 * GDN_W + h * GDN_DV:2 * GDN_W + (h + 1) * GDN_DV]
        q = q * lax.rsqrt(jnp.sum(q * q, axis=-1, keepdims=True) + NORM_EPS) * (GDN_DK ** -0.5)
        k = k * lax.rsqrt(jnp.sum(k * k, axis=-1, keepdims=True) + NORM_EPS)
        cf, cb = h, GDN_HEADS + h
        bf_, bb_ = 2 * GDN_HEADS + h, 3 * GDN_HEADS + h
        sl = slice(h * GDN_DV, (h + 1) * GDN_DV)
        outs[2][:, sl] = k * kdec_f[:, cf:cf + 1]
        outs[8][:, sl] = k * kdec_b[:, cb:cb + 1]
        outs[3][:, sl] = q * egc_f[:, cf:cf + 1]
        outs[9][:, sl] = q * egc_b[:, cb:cb + 1]
        for ch in range(cpt):
            r = slice(ch * c, (ch + 1) * c)
            k2 = jnp.concatenate([k[r], k[r]], axis=0)
            q2 = jnp.concatenate([q[r], q[r]], axis=0)
            v2 = jnp.concatenate([v[r], v[r]], axis=0)
            kb2 = k2.astype(BF16)
            kk2 = _dot_nt(kb2, kb2)
            qk2 = _dot_nt(q2.astype(BF16), kb2)
            gcol = jnp.concatenate([gc_f[r, cf:cf + 1], gc_b[r, cb:cb + 1]], axis=0)
            grow = jnp.concatenate([gcf_t[cf:cf + 1, r], gcb_t[cb:cb + 1, r]], axis=1)
            bcol = jnp.concatenate([beta[r, bf_:bf_ + 1], beta[r, bb_:bb_ + 1]], axis=0)
            ecol = jnp.concatenate([egc_f[r, cf:cf + 1], egc_b[r, cb:cb + 1]], axis=0)
            ldec = jnp.where(incl, jnp.exp(jnp.where(incl, gcol - grow, 0.0)), 0.0)
            n_list.append(jnp.where(strict, kk2 * bcol * ldec, 0.0))
            rhs_list.append(jnp.concatenate([v2 * bcol, k2 * (bcol * ecol)], axis=1))
            qk_list.append(qk2 * ldec)
            for d, col in ((0, cf), (1, cb)):
                outs[6 * d + 5][ch, h:h + 1, :] = jnp.broadcast_to(etot[ch * c:ch * c + 1, col:col + 1], (1, 128))

    yinv = _tri_inverse_minus_eye(jnp.stack(n_list, axis=0))
    rhs = jnp.stack(rhs_list, axis=0)
    sol = rhs + _bdot(yinv.astype(BF16), rhs.astype(BF16))
    for h in range(GDN_HEADS):
        sl = slice(h * GDN_DV, (h + 1) * GDN_DV)
        for ch in range(cpt):
            r = slice(ch * c, (ch + 1) * c)
            s = sol[h * cpt + ch]
            qk = qk_list[h * cpt + ch]
            outs[0][r, sl] = s[:c, :GDN_DV]
            outs[1][r, sl] = s[:c, GDN_DV:]
            outs[6][r, sl] = s[c:, :GDN_DV]
            outs[7][r, sl] = s[c:, GDN_DV:]
            outs[4][r, h * c:(h + 1) * c] = qk[:c, :c]
            outs[10][r, h * c:(h + 1) * c] = qk[c:, c:]


def _gdn_prep(qkv, ab, conv_w, alog_row, dtb_row, seq_len):
    nt = qkv.shape[0]
    c = GDN_CHUNK
    tr = GDN_TILE
    cpt = tr // c
    tps = seq_len // tr
    nblk8 = nt // 8
    out_shape, out_specs = [], []
    for _ in range(2):
        for wd in (GDN_W, GDN_W, GDN_W, GDN_W, GDN_HEADS * c):
            out_shape.append(jax.ShapeDtypeStruct((nt, wd), F32))
            out_specs.append(pl.BlockSpec((tr, wd), lambda i: (i, 0)))
        out_shape.append(jax.ShapeDtypeStruct((nt // c, 8, 128), F32))
        out_specs.append(pl.BlockSpec((cpt, 8, 128), lambda i: (i, 0, 0)))
    return pl.pallas_call(
        functools.partial(_gdn_prep_body, tps=tps),
        grid=(nt // tr,),
        in_specs=[pl.BlockSpec((tr, W_QKV), lambda i: (i, 0)),
                  pl.BlockSpec((8, W_QKV), lambda i: (jnp.maximum(i * (tr // 8) - 1, 0), 0)),
                  pl.BlockSpec((8, W_QKV), lambda i: (jnp.minimum((i + 1) * (tr // 8), nblk8 - 1), 0)),
                  pl.BlockSpec((tr, W_AB), lambda i: (i, 0)),
                  pl.BlockSpec((3, W_QKV), lambda i: (0, 0)),
                  pl.BlockSpec((1, 128), lambda i: (0, 0)),
                  pl.BlockSpec((1, 128), lambda i: (0, 0))],
        out_specs=out_specs,
        out_shape=out_shape,
        compiler_params=_cparams(("parallel",)),
        name="gdn_prep",
    )(qkv, qkv, qkv, ab, conv_w, alog_row, dtb_row)


def _gdn_scan_body(*refs, nch):
    ins = refs[:12]
    s0_ref = refs[12]
    o_refs = refs[13:15]
    sfin_ref = refs[15]
    s_scr = refs[16]
    n = pl.program_id(1)
    c = GDN_CHUNK

    @pl.when(n == 0)
    def _():
        s_scr[...] = s0_ref[0]

    for d in range(2):
        u_ref, w_ref, kd_ref, qd_ref, qk_ref, eg_ref = ins[6 * d:6 * d + 6]
        for h in range(GDN_HEADS):
            sl = slice(h * GDN_DV, (h + 1) * GDN_DV)
            s = s_scr[d, h]
            wq = jnp.concatenate([w_ref[:, sl], qd_ref[:, sl]], axis=0).astype(BF16)
            r = _dot(wq, s.astype(BF16))
            v_new = u_ref[:, sl] - r[:c]
            vb = v_new.astype(BF16)
            o = r[c:] + _dot(qk_ref[:, h * c:(h + 1) * c].astype(BF16), vb)
            o_refs[d][:, sl] = o
            s_scr[d, h] = s * eg_ref[0, h:h + 1, :] + _dot_tn(kd_ref[:, sl].astype(BF16), vb)

    @pl.when(n == nch - 1)
    def _():
        sfin_ref[0] = s_scr[...]


def _gdn_scan(prep, s0, batch, seq_len):
    c = GDN_CHUNK
    nch = seq_len // c
    nt = batch * seq_len
    fwd = lambda b, n: (b * nch + n, 0)
    bwd = lambda b, n: (b * nch + (nch - 1 - n), 0)
    in_specs = []
    for d, im in enumerate((fwd, bwd)):
        for wd in (GDN_W, GDN_W, GDN_W, GDN_W, GDN_HEADS * c):
            in_specs.append(pl.BlockSpec((c, wd), im))
        in_specs.append(pl.BlockSpec((1, 8, 128), (lambda im_: (lambda b, n: im_(b, n) + (0,)))(im)))
    st_spec = pl.BlockSpec((1, 2, GDN_HEADS, GDN_DK, GDN_DV), lambda b, n: (b, 0, 0, 0, 0))
    in_specs.append(st_spec)
    o_f, o_b, s_fin = pl.pallas_call(
        functools.partial(_gdn_scan_body, nch=nch),
        grid=(batch, nch),
        in_specs=in_specs,
        out_specs=[pl.BlockSpec((c, GDN_W), fwd), pl.BlockSpec((c, GDN_W), bwd), st_spec],
        out_shape=[jax.ShapeDtypeStruct((nt, GDN_W), F32), jax.ShapeDtypeStruct((nt, GDN_W), F32),
                   jax.ShapeDtypeStruct(s0.shape, F32)],
        scratch_shapes=[pltpu.VMEM((2, GDN_HEADS, GDN_DK, GDN_DV), F32)],
        compiler_params=_cparams(("arbitrary", "arbitrary")),
        name="gdn_scan",
    )(*prep, s0)
    return o_f, o_b, s_fin


def _group_norm_rope(x, w, bd, cos, sin):
    xx = x * x
    hi, lo = _split_bf16(xx)
    ss = _dot(hi, bd) + _dot(lo, bd)
    xn = x * lax.rsqrt(ss * (1.0 / ATT_HD) + NORM_EPS) * w
    if cos is not None:
        width = x.shape[1]
        lane = lax.broadcasted_iota(jnp.int32, x.shape, 1)
        from_below = pltpu.roll(xn, ROPE_PAIRS, 1)
        from_above = pltpu.roll(xn, width - ROPE_PAIRS, 1)
        partner = jnp.where((lane % (2 * ROPE_PAIRS)) < ROPE_PAIRS, from_above, from_below)
        xn = xn * cos + partner * sin
    return xn


def _attn_prep_body(*refs, rope):
    if rope:
        q_ref, kv_ref, cos_ref, sin_ref, qw_ref, kw_ref, bd_ref, qh_ref, kh_ref, vh_ref = refs
        cos, sin = cos_ref[...], sin_ref[...]
        kcos, ksin = cos[:, :ATT_KV_HEADS * ATT_HD], sin[:, :ATT_KV_HEADS * ATT_HD]
    else:
        q_ref, kv_ref, qw_ref, kw_ref, bd_ref, qh_ref, kh_ref, vh_ref = refs
        cos = sin = kcos = ksin = None
    nk = ATT_KV_HEADS * ATT_HD
    bd = bd_ref[...]
    q = _group_norm_rope(q_ref[...], qw_ref[...], bd, cos, sin) * (ATT_HD ** -0.5 * math.log2(math.e))
    kv = kv_ref[...]
    k = _group_norm_rope(kv[:, :nk], kw_ref[...], bd[:nk, :nk], kcos, ksin)
    v = kv[:, nk:]
    for j in range(ATT_HEADS):
        qh_ref[j] = q[:, j * ATT_HD:(j + 1) * ATT_HD].astype(BF16)
    for j in range(ATT_KV_HEADS):
        kh_ref[j] = k[:, j * ATT_HD:(j + 1) * ATT_HD].astype(BF16)
        vh_ref[j] = v[:, j * ATT_HD:(j + 1) * ATT_HD].astype(BF16)


def _attn_prep(q, kv, qw_row, kw_row, bd, cos=None, sin=None, seq_len=None):
    nt = q.shape[0]
    tm = 256
    rope = cos is not None
    in_specs = [pl.BlockSpec((tm, W_Q), lambda i: (i, 0)),
                pl.BlockSpec((tm, W_KV), lambda i: (i, 0))]
    args = [q, kv]
    if rope:
        tps = seq_len // tm
        in_specs += [pl.BlockSpec((tm, W_Q), lambda i: (i % tps, 0)),
                     pl.BlockSpec((tm, W_Q), lambda i: (i % tps, 0))]
        args += [cos, sin]
    in_specs += [pl.BlockSpec((1, W_Q), lambda i: (0, 0)),
                 pl.BlockSpec((1, ATT_KV_HEADS * ATT_HD), lambda i: (0, 0)),
                 pl.BlockSpec((W_Q, W_Q), lambda i: (0, 0))]
    args += [qw_row, kw_row, bd]
    return pl.pallas_call(
        functools.partial(_attn_prep_body, rope=rope),
        grid=(nt // tm,),
        in_specs=in_specs,
        out_specs=[pl.BlockSpec((ATT_HEADS, tm, ATT_HD), lambda i: (0, i, 0)),
                   pl.BlockSpec((ATT_KV_HEADS, tm, ATT_HD), lambda i: (0, i, 0)),
                   pl.BlockSpec((ATT_KV_HEADS, tm, ATT_HD), lambda i: (0, i, 0))],
        out_shape=[jax.ShapeDtypeStruct((ATT_HEADS, nt, ATT_HD), BF16),
                   jax.ShapeDtypeStruct((ATT_KV_HEADS, nt, ATT_HD), BF16),
                   jax.ShapeDtypeStruct((ATT_KV_HEADS, nt, ATT_HD), BF16)],
        compiler_params=_cparams(("parallel",)),
        name="attn_prep",
    )(*args)


def _attention_body(q_ref, *refs, src_lens, tk, tq):
    n_src = len(src_lens)
    o_ref = refs[2 * n_src]
    rows = ATT_GROUP * tq
    q = q_ref[...].reshape(rows, ATT_HD)

    def block(k, v, carry):
        m, l, acc = carry
        sc = _dot_nt(q, k)
        m_new = jnp.maximum(m, jnp.max(sc, axis=1, keepdims=True))
        alpha = jnp.exp2(m - m_new)
        p = jnp.exp2(sc - m_new)
        l = alpha * l + jnp.sum(p, axis=1, keepdims=True)
        acc = alpha * acc + _dot(p.astype(BF16), v)
        return m_new, l, acc

    carry = (jnp.full((rows, 1), -jnp.inf, F32), jnp.zeros((rows, 1), F32), jnp.zeros((rows, ATT_HD), F32))
    for s in range(n_src):
        k_ref, v_ref = refs[2 * s], refs[2 * s + 1]
        blk = min(tk, src_lens[s])
        nblk = src_lens[s] // blk
        for j in range(nblk):
            carry = block(k_ref[0, j * blk:(j + 1) * blk, :], v_ref[0, j * blk:(j + 1) * blk, :], carry)
    m, l, acc = carry
    o = (acc / l).reshape(ATT_GROUP, tq, ATT_HD)
    o_ref[...] = jnp.concatenate([o[j] for j in range(ATT_GROUP)], axis=1)


def _attention(qh, sources, q_len):
    nq = qh.shape[1]
    tq = 128
    tk = 2048
    tpb = q_len // tq
    in_specs = [pl.BlockSpec((ATT_GROUP, tq, ATT_HD), lambda i, g: (g, i, 0))]
    args = [qh]
    for kh, vh, sl in sources:
        spec = pl.BlockSpec((1, sl, ATT_HD), lambda i, g: (g, i // tpb, 0))
        in_specs += [spec, spec]
        args += [kh, vh]
    return pl.pallas_call(
        functools.partial(_attention_body, src_lens=tuple(s[2] for s in sources), tk=tk, tq=tq),
        grid=(nq // tq, ATT_KV_HEADS),
        in_specs=in_specs,
        out_specs=pl.BlockSpec((tq, ATT_GROUP * ATT_HD), lambda i, g: (i, g)),
        out_shape=jax.ShapeDtypeStruct((nq, W_Q), F32),
        compiler_params=_cparams(("parallel", "parallel")),
        name="attention",
    )(*args)


def _merge_body(of_ref, ob_ref, z_ref, p_ref, pp_ref, pn_ref, oc_ref, gt_ref, h_ref, mod_ref,
                gnw_ref, cw_ref, wb_ref, wo_ref, o_ref, *, tps):
    i = pl.program_id(0)
    first = (i % tps) == 0
    last = (i % tps) == tps - 1
    o = of_ref[...] + ob_ref[...]
    z = z_ref[...]
    parts = []
    for h in range(GDN_HEADS):
        sl = slice(h * GDN_DV, (h + 1) * GDN_DV)
        oh = o[:, sl]
        oh = oh * lax.rsqrt(jnp.mean(oh * oh, axis=-1, keepdims=True) + NORM_EPS) * gnw_ref[...]
        zh = z[:, sl]
        parts.append(oh * (zh * jax.nn.sigmoid(zh)))
    br_a = jnp.concatenate(parts, axis=1)
    p = p_ref[...]
    tm = p.shape[0]
    bg = p[:, :SCONV_W]
    cx = p[:, SCONV_W:2 * SCONV_W] * p[:, 2 * SCONV_W:]
    pp = pp_ref[7:8, :]
    pn = pn_ref[0:1, :]
    prev_row = jnp.where(first, 0.0, pp[:, SCONV_W:2 * SCONV_W] * pp[:, 2 * SCONV_W:])
    next_row = jnp.where(last, 0.0, pn[:, SCONV_W:2 * SCONV_W] * pn[:, 2 * SCONV_W:])
    rows = lax.broadcasted_iota(jnp.int32, cx.shape, 0)
    cprev = jnp.where(rows == 0, prev_row, pltpu.roll(cx, 1, 0))
    cnext = jnp.where(rows == tm - 1, next_row, pltpu.roll(cx, tm - 1, 0))
    br_b = bg * (cw_ref[0:1, :] * cprev + cw_ref[1:2, :] * cx + cw_ref[2:3, :] * cnext)
    br_c = oc_ref[...]
    mixed = None
    for n, br in enumerate((br_a, br_b, br_c)):
        t = _dot(br.astype(BF16), wb_ref[n]) * jax.nn.sigmoid(gt_ref[:, n * D_MODEL:(n + 1) * D_MODEL])
        mixed = t if mixed is None else mixed + t
    y = _dot(mixed.astype(BF16), wo_ref[...])
    o_ref[...] = h_ref[...] + mod_ref[0, 2:3, :] * y


def _merge(o_f, o_b, z, psc, oc, gates, h, mod, mod_idx, gnw_row, conv_b_w, wb, wo, seq_len):
    nt, d = h.shape
    tm = 256
    tps = seq_len // tm
    nblk8 = nt // 8
    row = lambda i: (i, 0)
    return pl.pallas_call(
        functools.partial(_merge_body, tps=tps),
        grid=(nt // tm,),
        in_specs=[pl.BlockSpec((tm, GDN_W), row),
                  pl.BlockSpec((tm, GDN_W), row),
                  pl.BlockSpec((tm, W_Z), row),
                  pl.BlockSpec((tm, W_SC), row),
                  pl.BlockSpec((8, W_SC), lambda i: (jnp.maximum(i * (tm // 8) - 1, 0), 0)),
                  pl.BlockSpec((8, W_SC), lambda i: (jnp.minimum((i + 1) * (tm // 8), nblk8 - 1), 0)),
                  pl.BlockSpec((tm, W_Q), row),
                  pl.BlockSpec((tm, W_G), row),
                  pl.BlockSpec((tm, d), row),
                  pl.BlockSpec((1, 6, d), lambda i: (mod_idx(i, tm), 0, 0)),
                  pl.BlockSpec((1, GDN_DV), lambda i: (0, 0)),
                  pl.BlockSpec((3, SCONV_W), lambda i: (0, 0)),
                  pl.BlockSpec((3, SCONV_W, d), lambda i: (0, 0, 0)),
                  pl.BlockSpec((d, d), lambda i: (0, 0))],
        out_specs=pl.BlockSpec((tm, d), row),
        out_shape=jax.ShapeDtypeStruct((nt, d), F32),
        compiler_params=_cparams(("parallel",)),
        name="merge",
    )(o_f, o_b, z, psc, psc, psc, oc, gates, h, mod, gnw_row, conv_b_w, wb, wo)


PEER_NO_RANK = 64.0


def _top_rows(s, count):
    rows = []
    rank = jnp.full(s.shape, PEER_NO_RANK, F32)
    for r in range(count):
        m = jnp.max(s, axis=0, keepdims=True)
        rows.append(m)
        hit = s == m
        rank = jnp.where(hit, float(r + 1), rank)
        s = jnp.where(hit, -jnp.inf, s)
    return rows, rank


def _peer_body(h_ref, mod_ref, nw_ref, wqh_ref, wql_ref, skh_ref, skl_ref, u_ref, vt_ref, o_ref,
               xt_scr, out_scr, sc_scr, cnt_scr, ea_scr, rank_scr, eb_scr, cand_scr, gw_scr,
               *, tm, ec, nchunks):
    c = pl.program_id(1)
    nk = PEER_NKEYS
    half = nk // 2
    tb_w = PEER_TB
    ntb = tm // tb_w
    nlb = tm // 128
    lb_per_tb = tb_w // 128

    @pl.when(c == 0)
    def _():
        u = _rms_mod(h_ref[...], nw_ref[...], mod_ref[0, 3:4, :], mod_ref[0, 4:5, :])
        ut = u.T
        uh, ul = _split_bf16(ut)
        wqh = wqh_ref[...]
        qt = _dot(wqh, uh) + _dot(wqh, ul) + _dot(wql_ref[...], uh)
        for tb in range(ntb):
            xt_scr[tb] = uh[:, tb * tb_w:(tb + 1) * tb_w]
            out_scr[tb] = jnp.zeros(out_scr.shape[1:], F32)
        for h in range(PEER_HEADS):
            for p in range(2):
                qs = qt[h * nk + p * half:h * nk + (p + 1) * half, :]
                qh_, ql_ = _split_bf16(qs)
                s = _dot(skh_ref[h, p], qh_) + _dot(skh_ref[h, p], ql_) + _dot(skl_ref[h, p], qh_)
                for lb in range(nlb):
                    sc_scr[p, h * nlb + lb] = s[:, lb * 128:(lb + 1) * 128]

        def select_one(hl):
            s0 = sc_scr[0, hl]
            s1 = sc_scr[1, hl]
            top0, _ = _top_rows(s0, PEER_TOPK)
            top1, rank1 = _top_rows(s1, PEER_TOPK)
            pairs = [(k, l) for k in range(PEER_TOPK) for l in range(PEER_TOPK) if (k + 1) * (l + 1) <= PEER_TOPK]
            cand = cand_scr.at[hl % 2]
            cand[...] = jnp.full(cand.shape, -jnp.inf, F32)
            sums = {}
            for r, (k, l) in enumerate(pairs):
                sums[k, l] = top0[k] + top1[l]
                cand[r:r + 1, :] = sums[k, l]
            best, _ = _top_rows(cand[...], PEER_TOPK)
            tau = best[PEER_TOPK - 1]
            zsum = jnp.zeros_like(tau)
            for b_ in best:
                zsum = zsum + jnp.exp(b_ - best[0])
            cnt = jnp.zeros(s0.shape, F32)
            for k in range(PEER_TOPK):
                n_sel = jnp.zeros_like(tau)
                for l in range(PEER_TOPK // (k + 1)):
                    n_sel = n_sel + jnp.where(sums[k, l] >= tau, 1.0, 0.0)
                cnt = jnp.where(s0 == top0[k], n_sel, cnt)
            cnt_scr[hl] = cnt.reshape(nk // 8, 8, 128)
            ea_scr[hl] = (jnp.exp(s0 - top0[0]) / zsum).reshape(nk // 8, 8, 128)
            rank_scr[hl] = rank1.astype(BF16)
            eb_scr[hl] = jnp.exp(s1 - top1[0]).astype(BF16)

        def select(pair, carry):
            select_one(2 * pair)
            select_one(2 * pair + 1)
            return carry

        lax.fori_loop(0, PEER_HEADS * nlb // 2, select, 0)

    nblk = ec // nk
    piece = 16
    npiece = nk // piece

    def half_dots(lhs_ref, rhs):
        hm = lhs_ref.shape[0] // 2
        return [_dot(lhs_ref[0:hm, :], rhs), _dot(lhs_ref[hm:2 * hm, :], rhs)]

    def gated_activation(tb, acts):
        hm = ec // 2
        for sub in range(lb_per_tb):
            lb = tb * lb_per_tb + sub
            ls = slice(sub * 128, (sub + 1) * 128)
            for ii in range(nblk):
                wm = [None] * npiece
                for h in range(PEER_HEADS):
                    cnt_row = jnp.broadcast_to(cnt_scr[h * nlb + lb, c, ii:ii + 1, :], (piece, 128)).astype(BF16)
                    ea_row = jnp.broadcast_to(ea_scr[h * nlb + lb, c, ii:ii + 1, :], (piece, 128)).astype(BF16)
                    for jp in range(npiece):
                        js = slice(jp * piece, (jp + 1) * piece)
                        sel = rank_scr[h * nlb + lb, js, :] <= cnt_row
                        w = eb_scr[h * nlb + lb, js, :] * ea_row
                        w = jnp.where(sel, w, jnp.zeros_like(w))
                        wm[jp] = w if h == 0 else wm[jp] + w
                for jp in range(npiece):
                    r0 = ii * nk + jp * piece
                    ap = acts[r0 // hm][r0 % hm:r0 % hm + piece, ls]
                    gl = 0.5 * ap * (1.0 + lax.erf(ap * (0.5 ** 0.5)))
                    gw_scr[tb, r0:r0 + piece, ls] = gl.astype(BF16) * wm[jp]

    acts = [half_dots(u_ref, xt_scr[tb]) for tb in range(ntb)]
    for tb in range(ntb):
        gated_activation(tb, acts[tb])
        outs = half_dots(vt_ref, gw_scr[tb])
        hd = out_scr.shape[1] // 2
        out_scr[tb, 0:hd, :] += outs[0]
        out_scr[tb, hd:2 * hd, :] += outs[1]

    @pl.when(c == nchunks - 1)
    def _():
        out = jnp.concatenate([out_scr[tb] for tb in range(ntb)], axis=1)
        o_ref[...] = h_ref[...] + mod_ref[0, 5:6, :] * out.T


def _peer(h, mod, mod_idx, nw, wq_hi, wq_lo, sk_hi, sk_lo, u_tab, vt_tab):
    nt, d = h.shape
    tm = 512
    ec = 1024
    ne = u_tab.shape[0]
    nchunks = ne // ec
    nk = PEER_NKEYS
    ntb = tm // PEER_TB
    nlb = tm // 128
    const2 = lambda i, c: (0, 0)
    per_head = lambda dt: pltpu.VMEM((PEER_HEADS * nlb, nk, 128), dt)
    per_head_rows = pltpu.VMEM((PEER_HEADS * nlb, nk // 8, 8, 128), F32)
    return pl.pallas_call(
        functools.partial(_peer_body, tm=tm, ec=ec, nchunks=nchunks),
        grid=(nt // tm, nchunks),
        in_specs=[pl.BlockSpec((tm, d), lambda i, c: (i, 0)),
                  pl.BlockSpec((1, 6, d), lambda i, c: (mod_idx(i, tm), 0, 0)),
                  pl.BlockSpec((1, d), const2),
                  pl.BlockSpec((d, d), const2),
                  pl.BlockSpec((d, d), const2),
                  pl.BlockSpec((PEER_HEADS, 2, nk, nk // 2), lambda i, c: (0, 0, 0, 0)),
                  pl.BlockSpec((PEER_HEADS, 2, nk, nk // 2), lambda i, c: (0, 0, 0, 0)),
                  pl.BlockSpec((ec, d), lambda i, c: (c, 0)),
                  pl.BlockSpec((d, ec), lambda i, c: (0, c))],
        out_specs=pl.BlockSpec((tm, d), lambda i, c: (i, 0)),
        out_shape=jax.ShapeDtypeStruct((nt, d), F32),
        scratch_shapes=[pltpu.VMEM((ntb, d, PEER_TB), BF16),
                        pltpu.VMEM((ntb, d, PEER_TB), F32),
                        pltpu.VMEM((2, PEER_HEADS * nlb, nk, 128), F32),
                        per_head_rows,
                        per_head_rows,
                        per_head(BF16),
                        per_head(BF16),
                        pltpu.VMEM((2, 64, 128), F32),
                        pltpu.VMEM((ntb, ec, PEER_TB), BF16)],
        compiler_params=_cparams(("parallel", "arbitrary")),
        name="peer",
    )(h, mod, nw.reshape(1, d), wq_hi, wq_lo, sk_hi, sk_lo, u_tab, vt_tab)


def _final_norm_body(h_ref, w_ref, o_ref):
    x = h_ref[...]
    ms = jnp.mean(x * x, axis=-1, keepdims=True)
    o_ref[...] = x * lax.rsqrt(ms + NORM_EPS) * w_ref[...]


def _final_norm(h, w):
    nt, d = h.shape
    tm = 512
    return pl.pallas_call(
        _final_norm_body,
        grid=(nt // tm,),
        in_specs=[pl.BlockSpec((tm, d), lambda i: (i, 0)), pl.BlockSpec((1, d), lambda i: (0, 0))],
        out_specs=pl.BlockSpec((tm, d), lambda i: (i, 0)),
        out_shape=jax.ShapeDtypeStruct((nt, d), F32),
        compiler_params=_cparams(("parallel",)),
        name="final_norm",
    )(h, w.reshape(1, d))


def _rope_tables(seq):
    t = jnp.arange(seq)
    row = (t // GRID_W).astype(F32)
    col = (t % GRID_W).astype(F32)
    freqs = ROPE_THETA ** (-jnp.arange(ROPE_PAIRS, dtype=F32) / ROPE_PAIRS)
    ar = row[:, None] * freqs
    ac = col[:, None] * freqs
    cos = jnp.concatenate([jnp.cos(ar), jnp.cos(ar), jnp.cos(ac), jnp.cos(ac)], axis=1)
    sin = jnp.concatenate([-jnp.sin(ar), jnp.sin(ar), -jnp.sin(ac), jnp.sin(ac)], axis=1)
    return jnp.tile(cos, (1, ATT_HEADS)), jnp.tile(sin, (1, ATT_HEADS))


def _reorder_w_in(w):
    sizes = (W_QKV, W_Z, 4 * GDN_HEADS, W_SC, W_Q, W_KV, W_G)
    parts, start = [], 0
    for s in sizes:
        parts.append(w[:, start:start + s])
        start += s
    qkv, z, ab, sc, q, kv, g = parts
    ab = jnp.pad(ab, ((0, 0), (0, W_AB - ab.shape[1])))
    return jnp.concatenate([qkv, z, sc, q, kv, g, ab], axis=1).astype(BF16)


def _pad_row(v, n=128):
    v = v.reshape(1, -1)
    return jnp.pad(v, ((0, 0), (0, n - v.shape[1])))


def kernel(x, c, ctx, c_ctx, w_mod, b_mod, norm1_w, w_in, conv_a_w, a_log, dt_bias, gdn_norm_w, conv_b_w,
           q_norm_w, k_norm_w, w_branch, w_out, norm2_w, w_query, sub_keys, expert_u, expert_v, final_norm_w):
    batch, seq, d = x.shape
    ctx_len = ctx.shape[1]
    depth = w_mod.shape[0]
    h = x.reshape(batch * seq, d)
    hc = ctx.reshape(batch * ctx_len, d)

    mod_rows = -(-(batch + 1) // 8) * 8
    c_all = jnp.zeros((mod_rows, d), F32).at[:batch].set(c).at[batch].set(c_ctx)
    cos, sin = _rope_tables(seq)
    hd_i = jnp.arange(W_Q) // ATT_HD
    bd = (hd_i[:, None] == hd_i[None, :]).astype(BF16)
    lat_idx = lambda i, tm: i // (seq // tm)
    ctx_idx = lambda i, tm: batch

    for layer in range(depth):
        last = layer == depth - 1
        mod = _modulation(c_all, w_mod[layer], b_mod[layer]).reshape(mod_rows, 6, d)
        w_in_r = _reorder_w_in(w_in[layer])
        alog_row = _pad_row(a_log[layer])
        dtb_row = _pad_row(dt_bias[layer])
        qw_row = jnp.tile(q_norm_w[layer], ATT_HEADS).reshape(1, W_Q)
        kw_row = jnp.tile(k_norm_w[layer], ATT_KV_HEADS).reshape(1, ATT_KV_HEADS * ATT_HD)
        gnw_row = gdn_norm_w[layer].reshape(1, GDN_DV)
        wb = w_branch[layer].astype(BF16)
        wo = w_out[layer].astype(BF16)
        wq_hi, wq_lo = _split_bf16(w_query[layer].T)
        sk_hi, sk_lo = _split_bf16(sub_keys[layer])
        u_tab = expert_u[layer].astype(BF16)
        vt_tab = expert_v[layer].T.astype(BF16)

        pl_ = _inproj(h, mod, lat_idx, norm1_w[layer], w_in_r)
        pc_ = _inproj(hc, mod, ctx_idx, norm1_w[layer], w_in_r)

        prep_c = _gdn_prep(pc_[0], pc_[6], conv_a_w[layer], alog_row, dtb_row, ctx_len)
        prep_l = _gdn_prep(pl_[0], pl_[6], conv_a_w[layer], alog_row, dtb_row, seq)
        s_zero = jnp.zeros((batch, 2, GDN_HEADS, GDN_DK, GDN_DV), F32)
        ocf, ocb, s_ctx = _gdn_scan(prep_c, s_zero, batch, ctx_len)
        olf, olb, _ = _gdn_scan(prep_l, s_ctx, batch, seq)

        qh_c, kh_c, vh_c = _attn_prep(pc_[3], pc_[4], qw_row, kw_row, bd)
        qh_l, kh_l, vh_l = _attn_prep(pl_[3], pl_[4], qw_row, kw_row, bd, cos, sin, seq)
        att_l = _attention(qh_l, [(kh_l, vh_l, seq), (kh_c, vh_c, ctx_len)], seq)

        h = _merge(olf, olb, pl_[1], pl_[2], att_l, pl_[5], h, mod, lat_idx, gnw_row, conv_b_w[layer],
                   wb, wo, seq)
        h = _peer(h, mod, lat_idx, norm2_w[layer], wq_hi, wq_lo, sk_hi, sk_lo, u_tab, vt_tab)
        if not last:
            att_c = _attention(qh_c, [(kh_c, vh_c, ctx_len)], ctx_len)
            hc = _merge(ocf, ocb, pc_[1], pc_[2], att_c, pc_[5], hc, mod, ctx_idx, gnw_row, conv_b_w[layer],
                        wb, wo, ctx_len)
            hc = _peer(hc, mod, ctx_idx, norm2_w[layer], wq_hi, wq_lo, sk_hi, sk_lo, u_tab, vt_tab)
    return _final_norm(h, final_norm_w).reshape(batch, seq, d)
```

```python
import functools
import math

import jax
import jax.numpy as jnp
from jax import lax
from jax.experimental import pallas as pl
from jax.experimental.pallas import tpu as pltpu

F32 = jnp.float32
BF16 = jnp.bfloat16

D_MODEL = 1024
GRID_W = 64
GDN_HEADS = 4
GDN_DK = 128
GDN_DV = 128
GDN_CHUNK = 64
GDN_TILE = 256
GDN_W = GDN_HEADS * GDN_DV
SCONV_W = 512
ATT_HEADS = 8
ATT_KV_HEADS = 2
ATT_GROUP = 4
ATT_HD = 64
ROPE_THETA = 10000.0
ROPE_PAIRS = 16
PEER_HEADS = 8
PEER_NKEYS = 128
PEER_TOPK = 16
PEER_TB = 256
PEER_TM = 512
PEER_EC = 2048
NORM_EPS = 1e-6

W_QKV = 3 * GDN_W
W_Z = GDN_W
W_SC = 3 * SCONV_W
W_Q = ATT_HEADS * ATT_HD
W_KV = 2 * ATT_KV_HEADS * ATT_HD
W_G = 3 * D_MODEL
W_AB = 128
INPROJ_WIDTHS = (W_QKV, W_Z, W_SC, W_Q, W_KV, W_G, W_AB)

VMEM_LIMIT = 56 * 1024 * 1024


def _cparams(sem):
    return pltpu.CompilerParams(dimension_semantics=sem, vmem_limit_bytes=VMEM_LIMIT)


def _split_bf16(x):
    hi = x.astype(BF16)
    lo = (x - hi.astype(F32)).astype(BF16)
    return hi, lo


def _dot(a, b):
    return jnp.dot(a, b, preferred_element_type=F32)


def _dot_nt(a, b):
    return lax.dot_general(a, b, (((1,), (1,)), ((), ())), preferred_element_type=F32)


def _dot_tn(a, b):
    return lax.dot_general(a, b, (((0,), (0,)), ((), ())), preferred_element_type=F32)


def _dot3(a, b):
    ah, al = _split_bf16(a)
    bh, bl = _split_bf16(b)
    return _dot(ah, bh) + _dot(ah, bl) + _dot(al, bh)


def _rms_mod(x, nw, shift, scale):
    ms = jnp.mean(x * x, axis=-1, keepdims=True)
    return (x * lax.rsqrt(ms + NORM_EPS) * nw) * (1.0 + scale) + shift


def _mod_body(c_ref, w_ref, b_ref, o_ref):
    c = c_ref[...]
    a = c * jax.nn.sigmoid(c)
    o_ref[...] = _dot3(a, w_ref[...]) + b_ref[...]


def _modulation(c_all, w_mod, b_mod):
    rows, d = c_all.shape
    n = w_mod.shape[1]
    tn = 1536
    return pl.pallas_call(
        _mod_body,
        grid=(n // tn,),
        in_specs=[pl.BlockSpec((rows, d), lambda j: (0, 0)),
                  pl.BlockSpec((d, tn), lambda j: (0, j)),
                  pl.BlockSpec((1, tn), lambda j: (0, j))],
        out_specs=pl.BlockSpec((rows, tn), lambda j: (0, j)),
        out_shape=jax.ShapeDtypeStruct((rows, n), F32),
        compiler_params=_cparams(("parallel",)),
        name="modulation",
    )(c_all, w_mod, b_mod.reshape(1, n))


def _inproj_body(h_ref, mod_ref, nw_ref, w_ref, *outs):
    u = _rms_mod(h_ref[...], nw_ref[...], mod_ref[0, 0:1, :], mod_ref[0, 1:2, :])
    ub = u.astype(BF16)
    off = 0
    for o_ref, wd in zip(outs, INPROJ_WIDTHS):
        o_ref[...] = _dot(ub, w_ref[:, off:off + wd])
        off += wd


def _inproj(h, mod, mod_idx, nw, w):
    nt, d = h.shape
    tm = 256
    wtot = w.shape[1]
    return pl.pallas_call(
        _inproj_body,
        grid=(nt // tm,),
        in_specs=[pl.BlockSpec((tm, d), lambda i: (i, 0)),
                  pl.BlockSpec((1, 6, d), lambda i: (mod_idx(i, tm), 0, 0)),
                  pl.BlockSpec((1, d), lambda i: (0, 0)),
                  pl.BlockSpec((d, wtot), lambda i: (0, 0), pipeline_mode=pl.Buffered(1))],
        out_specs=[pl.BlockSpec((tm, wd), lambda i: (i, 0)) for wd in INPROJ_WIDTHS],
        out_shape=[jax.ShapeDtypeStruct((nt, wd), F32) for wd in INPROJ_WIDTHS],
        compiler_params=_cparams(("parallel",)),
        name="inproj",
    )(h, mod, nw.reshape(1, d), w)


def _bdot(a, b):
    return lax.dot_general(a, b, (((2,), (1,)), ((0,), (0,))), preferred_element_type=F32)


def _tri_inverse_minus_eye(n):
    nb = n.astype(BF16)
    p = _bdot(nb, nb)
    y = -n
    for step in range(5):
        y = y + p + _bdot(y.astype(BF16), p.astype(BF16))
        if step < 4:
            pb = p.astype(BF16)
            p = _bdot(pb, pb)
    return y


def _gdn_prep_body(x_ref, xp_ref, xn_ref, ab_ref, cw_ref, alog_ref, dtb_ref, *outs, tps):
    i = pl.program_id(0)
    first = (i % tps) == 0
    last = (i % tps) == tps - 1
    x = x_ref[...]
    nrow = x.shape[0]
    c = GDN_CHUNK
    cpt = nrow // c
    rows = lax.broadcasted_iota(jnp.int32, x.shape, 0)
    prev_row = jnp.where(first, 0.0, xp_ref[7:8, :])
    next_row = jnp.where(last, 0.0, xn_ref[0:1, :])
    xprev = jnp.where(rows == 0, prev_row, pltpu.roll(x, 1, 0))
    xnext = jnp.where(rows == nrow - 1, next_row, pltpu.roll(x, nrow - 1, 0))
    y = cw_ref[0:1, :] * xprev + cw_ref[1:2, :] * x + cw_ref[2:3, :] * xnext
    y = y * jax.nn.sigmoid(y)

    ab = ab_ref[...]
    zz = ab + dtb_ref[...]
    sp = jnp.maximum(zz, 0.0) + jnp.log1p(jnp.exp(-jnp.abs(zz)))
    g = -jnp.exp(alog_ref[...]) * sp
    beta = jax.nn.sigmoid(ab)

    ri = lax.broadcasted_iota(jnp.int32, (nrow, nrow), 0)
    ci = lax.broadcasted_iota(jnp.int32, (nrow, nrow), 1)
    same = (ri // c) == (ci // c)
    lower = (same & (ri >= ci)).astype(BF16)
    ones_blk = same.astype(BF16)
    g1 = g.astype(BF16)
    r1 = g - g1.astype(F32)
    g2 = r1.astype(BF16)
    g3 = (r1 - g2.astype(F32)).astype(BF16)
    gc_f = _dot(lower, g1) + _dot(lower, g2) + _dot(lower, g3)
    tot = _dot(ones_blk, g1) + _dot(ones_blk, g2) + _dot(ones_blk, g3)
    gc_b = tot - gc_f + g
    gcf_t = gc_f.T
    gcb_t = gc_b.T
    egc_f = jnp.exp(gc_f)
    egc_b = jnp.exp(gc_b)
    kdec_f = jnp.exp(tot - gc_f)
    kdec_b = jnp.exp(tot - gc_b)
    etot = jnp.exp(tot)

    pi = lax.broadcasted_iota(jnp.int32, (2 * c, 2 * c), 0)
    pj = lax.broadcasted_iota(jnp.int32, (2 * c, 2 * c), 1)
    top = (pi < c) & (pj < c)
    bot = (pi >= c) & (pj >= c)
    incl = (top & (pi >= pj)) | (bot & (pi <= pj))
    strict = (top & (pi > pj)) | (bot & (pi < pj))

    for d in range(2):
        outs[6 * d + 5][...] = jnp.zeros(outs[6 * d + 5].shape, F32)

    n_list, rhs_list, qk_list = [], [], []
    for h in range(GDN_HEADS):
        q = y[:, h * GDN_DK:(h + 1) * GDN_DK]
        k = y[:, GDN_W + h * GDN_DK:GDN_W + (h + 1) * GDN_DK]
        v = y[:, 2 * GDN_W + h * GDN_DV:2 * GDN_W + (h + 1) * GDN_DV]
        q = q * lax.rsqrt(jnp.sum(q * q, axis=-1, keepdims=True) + NORM_EPS) * (GDN_DK ** -0.5)
        k = k * lax.rsqrt(jnp.sum(k * k, axis=-1, keepdims=True) + NORM_EPS)
        cf, cb = h, GDN_HEADS + h
        bf_, bb_ = 2 * GDN_HEADS + h, 3 * GDN_HEADS + h
        sl = slice(h * GDN_DV, (h + 1) * GDN_DV)
        outs[2][:, sl] = k * kdec_f[:, cf:cf + 1]
        outs[8][:, sl] = k * kdec_b[:, cb:cb + 1]
        outs[3][:, sl] = q * egc_f[:, cf:cf + 1]
        outs[9][:, sl] = q * egc_b[:, cb:cb + 1]
        for ch in range(cpt):
            r = slice(ch * c, (ch + 1) * c)
            k2 = jnp.concatenate([k[r], k[r]], axis=0)
            q2 = jnp.concatenate([q[r], q[r]], axis=0)
            v2 = jnp.concatenate([v[r], v[r]], axis=0)
            kb2 = k2.astype(BF16)
            kk2 = _dot_nt(kb2, kb2)
            qk2 = _dot_nt(q2.astype(BF16), kb2)
            gcol = jnp.concatenate([gc_f[r, cf:cf + 1], gc_b[r, cb:cb + 1]], axis=0)
            grow = jnp.concatenate([gcf_t[cf:cf + 1, r], gcb_t[cb:cb + 1, r]], axis=1)
            bcol = jnp.concatenate([beta[r, bf_:bf_ + 1], beta[r, bb_:bb_ + 1]], axis=0)
            ecol = jnp.concatenate([egc_f[r, cf:cf + 1], egc_b[r, cb:cb + 1]], axis=0)
            ldec = jnp.where(incl, jnp.exp(jnp.where(incl, gcol - grow, 0.0)), 0.0)
            n_list.append(jnp.where(strict, kk2 * bcol * ldec, 0.0))
            rhs_list.append(jnp.concatenate([v2 * bcol, k2 * (bcol * ecol)], axis=1))
            qk_list.append(qk2 * ldec)
            for d, col in ((0, cf), (1, cb)):
                outs[6 * d + 5][ch, h:h + 1, :] = jnp.broadcast_to(etot[ch * c:ch * c + 1, col:col + 1], (1, 128))

    yinv = _tri_inverse_minus_eye(jnp.stack(n_list, axis=0))
    rhs = jnp.stack(rhs_list, axis=0)
    sol = rhs + _bdot(yinv.astype(BF16), rhs.astype(BF16))
    for h in range(GDN_HEADS):
        sl = slice(h * GDN_DV, (h + 1) * GDN_DV)
        for ch in range(cpt):
            r = slice(ch * c, (ch + 1) * c)
            s = sol[h * cpt + ch]
            qk = qk_list[h * cpt + ch]
            outs[0][r, sl] = s[:c, :GDN_DV]
            outs[1][r, sl] = s[:c, GDN_DV:]
            outs[6][r, sl] = s[c:, :GDN_DV]
            outs[7][r, sl] = s[c:, GDN_DV:]
            outs[4][r, h * c:(h + 1) * c] = qk[:c, :c]
            outs[10][r, h * c:(h + 1) * c] = qk[c:, c:]


def _gdn_prep(qkv, ab, conv_w, alog_row, dtb_row, seq_len):
    nt = qkv.shape[0]
    c = GDN_CHUNK
    tr = GDN_TILE
    cpt = tr // c
    tps = seq_len // tr
    nblk8 = nt // 8
    out_shape, out_specs = [], []
    for _ in range(2):
        for wd in (GDN_W, GDN_W, GDN_W, GDN_W, GDN_HEADS * c):
            out_shape.append(jax.ShapeDtypeStruct((nt, wd), F32))
            out_specs.append(pl.BlockSpec((tr, wd), lambda i: (i, 0)))
        out_shape.append(jax.ShapeDtypeStruct((nt // c, 8, 128), F32))
        out_specs.append(pl.BlockSpec((cpt, 8, 128), lambda i: (i, 0, 0)))
    return pl.pallas_call(
        functools.partial(_gdn_prep_body, tps=tps),
        grid=(nt // tr,),
        in_specs=[pl.BlockSpec((tr, W_QKV), lambda i: (i, 0)),
                  pl.BlockSpec((8, W_QKV), lambda i: (jnp.maximum(i * (tr // 8) - 1, 0), 0)),
                  pl.BlockSpec((8, W_QKV), lambda i: (jnp.minimum((i + 1) * (tr // 8), nblk8 - 1), 0)),
                  pl.BlockSpec((tr, W_AB), lambda i: (i, 0)),
                  pl.BlockSpec((3, W_QKV), lambda i: (0, 0)),
                  pl.BlockSpec((1, 128), lambda i: (0, 0)),
                  pl.BlockSpec((1, 128), lambda i: (0, 0))],
        out_specs=out_specs,
        out_shape=out_shape,
        compiler_params=_cparams(("parallel",)),
        name="gdn_prep",
    )(qkv, qkv, qkv, ab, conv_w, alog_row, dtb_row)


def _gdn_scan_body(*refs, nch):
    ins = refs[:12]
    s0_ref = refs[12]
    o_refs = refs[13:15]
    sfin_ref = refs[15]
    s_scr = refs[16]
    n = pl.program_id(1)
    c = GDN_CHUNK

    @pl.when(n == 0)
    def _():
        s_scr[...] = s0_ref[0]

    chains = [(d, h) for d in range(2) for h in range(GDN_HEADS)]
    state, prod, vnew = {}, {}, {}
    for d, h in chains:
        w_ref, qd_ref = ins[6 * d + 1], ins[6 * d + 3]
        sl = slice(h * GDN_DV, (h + 1) * GDN_DV)
        state[d, h] = s_scr[d, h]
        wq = jnp.concatenate([w_ref[:, sl], qd_ref[:, sl]], axis=0).astype(BF16)
        prod[d, h] = _dot(wq, state[d, h].astype(BF16))
    for d, h in chains:
        u_ref, qk_ref = ins[6 * d], ins[6 * d + 4]
        sl = slice(h * GDN_DV, (h + 1) * GDN_DV)
        r = prod[d, h]
        vnew[d, h] = (u_ref[:, sl] - r[:c]).astype(BF16)
        o_refs[d][:, sl] = r[c:] + _dot(qk_ref[:, h * c:(h + 1) * c].astype(BF16), vnew[d, h])
    for d, h in chains:
        kd_ref, eg_ref = ins[6 * d + 2], ins[6 * d + 5]
        sl = slice(h * GDN_DV, (h + 1) * GDN_DV)
        s_scr[d, h] = state[d, h] * eg_ref[0, h:h + 1, :] + _dot_tn(kd_ref[:, sl].astype(BF16), vnew[d, h])

    @pl.when(n == nch - 1)
    def _():
        sfin_ref[0] = s_scr[...]


def _gdn_scan(prep, s0, batch, seq_len):
    c = GDN_CHUNK
    nch = seq_len // c
    nt = batch * seq_len
    fwd = lambda b, n: (b * nch + n, 0)
    bwd = lambda b, n: (b * nch + (nch - 1 - n), 0)
    in_specs = []
    for d, im in enumerate((fwd, bwd)):
        for wd in (GDN_W, GDN_W, GDN_W, GDN_W, GDN_HEADS * c):
            in_specs.append(pl.BlockSpec((c, wd), im))
        in_specs.append(pl.BlockSpec((1, 8, 128), (lambda im_: (lambda b, n: im_(b, n) + (0,)))(im)))
    st_spec = pl.BlockSpec((1, 2, GDN_HEADS, GDN_DK, GDN_DV), lambda b, n: (b, 0, 0, 0, 0))
    in_specs.append(st_spec)
    o_f, o_b, s_fin = pl.pallas_call(
        functools.partial(_gdn_scan_body, nch=nch),
        grid=(batch, nch),
        in_specs=in_specs,
        out_specs=[pl.BlockSpec((c, GDN_W), fwd), pl.BlockSpec((c, GDN_W), bwd), st_spec],
        out_shape=[jax.ShapeDtypeStruct((nt, GDN_W), F32), jax.ShapeDtypeStruct((nt, GDN_W), F32),
                   jax.ShapeDtypeStruct(s0.shape, F32)],
        scratch_shapes=[pltpu.VMEM((2, GDN_HEADS, GDN_DK, GDN_DV), F32)],
        compiler_params=_cparams(("arbitrary", "arbitrary")),
        name="gdn_scan",
    )(*prep, s0)
    return o_f, o_b, s_fin


def _group_norm_rope(x, w, bd, cos, sin):
    xx = x * x
    hi, lo = _split_bf16(xx)
    ss = _dot(hi, bd) + _dot(lo, bd)
    xn = x * lax.rsqrt(ss * (1.0 / ATT_HD) + NORM_EPS) * w
    if cos is not None:
        width = x.shape[1]
        lane = lax.broadcasted_iota(jnp.int32, x.shape, 1)
        from_below = pltpu.roll(xn, ROPE_PAIRS, 1)
        from_above = pltpu.roll(xn, width - ROPE_PAIRS, 1)
        partner = jnp.where((lane % (2 * ROPE_PAIRS)) < ROPE_PAIRS, from_above, from_below)
        xn = xn * cos + partner * sin
    return xn


def _attn_prep_body(*refs, rope):
    if rope:
        q_ref, kv_ref, cos_ref, sin_ref, qw_ref, kw_ref, bd_ref, qh_ref, kh_ref, vh_ref = refs
        cos, sin = cos_ref[...], sin_ref[...]
        kcos, ksin = cos[:, :ATT_KV_HEADS * ATT_HD], sin[:, :ATT_KV_HEADS * ATT_HD]
    else:
        q_ref, kv_ref, qw_ref, kw_ref, bd_ref, qh_ref, kh_ref, vh_ref = refs
        cos = sin = kcos = ksin = None
    nk = ATT_KV_HEADS * ATT_HD
    bd = bd_ref[...]
    q = _group_norm_rope(q_ref[...], qw_ref[...], bd, cos, sin) * (ATT_HD ** -0.5 * math.log2(math.e))
    kv = kv_ref[...]
    k = _group_norm_rope(kv[:, :nk], kw_ref[...], bd[:nk, :nk], kcos, ksin)
    v = kv[:, nk:]
    for j in range(ATT_HEADS):
        qh_ref[j] = q[:, j * ATT_HD:(j + 1) * ATT_HD].astype(BF16)
    for j in range(ATT_KV_HEADS):
        kh_ref[j] = k[:, j * ATT_HD:(j + 1) * ATT_HD].astype(BF16)
        vh_ref[j] = v[:, j * ATT_HD:(j + 1) * ATT_HD].astype(BF16)


def _attn_prep(q, kv, qw_row, kw_row, bd, cos=None, sin=None, seq_len=None):
    nt = q.shape[0]
    tm = 256
    rope = cos is not None
    in_specs = [pl.BlockSpec((tm, W_Q), lambda i: (i, 0)),
                pl.BlockSpec((tm, W_KV), lambda i: (i, 0))]
    args = [q, kv]
    if rope:
        tps = seq_len // tm
        in_specs += [pl.BlockSpec((tm, W_Q), lambda i: (i % tps, 0)),
                     pl.BlockSpec((tm, W_Q), lambda i: (i % tps, 0))]
        args += [cos, sin]
    in_specs += [pl.BlockSpec((1, W_Q), lambda i: (0, 0)),
                 pl.BlockSpec((1, ATT_KV_HEADS * ATT_HD), lambda i: (0, 0)),
                 pl.BlockSpec((W_Q, W_Q), lambda i: (0, 0))]
    args += [qw_row, kw_row, bd]
    return pl.pallas_call(
        functools.partial(_attn_prep_body, rope=rope),
        grid=(nt // tm,),
        in_specs=in_specs,
        out_specs=[pl.BlockSpec((ATT_HEADS, tm, ATT_HD), lambda i: (0, i, 0)),
                   pl.BlockSpec((ATT_KV_HEADS, tm, ATT_HD), lambda i: (0, i, 0)),
                   pl.BlockSpec((ATT_KV_HEADS, tm, ATT_HD), lambda i: (0, i, 0))],
        out_shape=[jax.ShapeDtypeStruct((ATT_HEADS, nt, ATT_HD), BF16),
                   jax.ShapeDtypeStruct((ATT_KV_HEADS, nt, ATT_HD), BF16),
                   jax.ShapeDtypeStruct((ATT_KV_HEADS, nt, ATT_HD), BF16)],
        compiler_params=_cparams(("parallel",)),
        name="attn_prep",
    )(*args)


def _attention_body(q_ref, *refs, src_lens, tk, tq):
    n_src = len(src_lens)
    o_ref = refs[2 * n_src]
    rows = ATT_GROUP * tq
    q = q_ref[...].reshape(rows, ATT_HD)

    def block(sc, v, carry):
        m, l, acc = carry
        m_new = jnp.maximum(m, jnp.max(sc, axis=1, keepdims=True))
        alpha = jnp.exp2(m - m_new)
        p = jnp.exp2(sc - m_new)
        l = alpha * l + jnp.sum(p, axis=1, keepdims=True)
        acc = alpha * acc + _dot(p.astype(BF16), v)
        return m_new, l, acc

    carry = (jnp.full((rows, 1), -jnp.inf, F32), jnp.zeros((rows, 1), F32), jnp.zeros((rows, ATT_HD), F32))
    for s in range(n_src):
        k_ref, v_ref = refs[2 * s], refs[2 * s + 1]
        blk = min(tk, src_lens[s])
        nblk = src_lens[s] // blk
        for j in range(nblk):
            sc = _dot_nt(q, k_ref[0, j * blk:(j + 1) * blk, :])
            carry = block(sc, v_ref[0, j * blk:(j + 1) * blk, :], carry)
    m, l, acc = carry
    o = (acc / l).reshape(ATT_GROUP, tq, ATT_HD)
    o_ref[...] = jnp.concatenate([o[j] for j in range(ATT_GROUP)], axis=1)


def _attention(qh, sources, q_len):
    nq = qh.shape[1]
    tq = 128
    tk = 2048
    tpb = q_len // tq
    in_specs = [pl.BlockSpec((ATT_GROUP, tq, ATT_HD), lambda i, g: (g, i, 0))]
    args = [qh]
    for kh, vh, sl in sources:
        spec = pl.BlockSpec((1, sl, ATT_HD), lambda i, g: (g, i // tpb, 0))
        in_specs += [spec, spec]
        args += [kh, vh]
    return pl.pallas_call(
        functools.partial(_attention_body, src_lens=tuple(s[2] for s in sources), tk=tk, tq=tq),
        grid=(nq // tq, ATT_KV_HEADS),
        in_specs=in_specs,
        out_specs=pl.BlockSpec((tq, ATT_GROUP * ATT_HD), lambda i, g: (i, g)),
        out_shape=jax.ShapeDtypeStruct((nq, W_Q), F32),
        compiler_params=_cparams(("parallel", "parallel")),
        name="attention",
    )(*args)


def _merge_body(of_ref, ob_ref, z_ref, p_ref, pp_ref, pn_ref, oc_ref, gt_ref, h_ref, mod_ref,
                gnw_ref, cw_ref, wb_ref, wo_ref, o_ref, *, tps):
    i = pl.program_id(0)
    first = (i % tps) == 0
    last = (i % tps) == tps - 1
    o = of_ref[...] + ob_ref[...]
    z = z_ref[...]
    parts = []
    for h in range(GDN_HEADS):
        sl = slice(h * GDN_DV, (h + 1) * GDN_DV)
        oh = o[:, sl]
        oh = oh * lax.rsqrt(jnp.mean(oh * oh, axis=-1, keepdims=True) + NORM_EPS) * gnw_ref[...]
        zh = z[:, sl]
        parts.append(oh * (zh * jax.nn.sigmoid(zh)))
    br_a = jnp.concatenate(parts, axis=1)
    p = p_ref[...]
    tm = p.shape[0]
    bg = p[:, :SCONV_W]
    cx = p[:, SCONV_W:2 * SCONV_W] * p[:, 2 * SCONV_W:]
    pp = pp_ref[7:8, :]
    pn = pn_ref[0:1, :]
    prev_row = jnp.where(first, 0.0, pp[:, SCONV_W:2 * SCONV_W] * pp[:, 2 * SCONV_W:])
    next_row = jnp.where(last, 0.0, pn[:, SCONV_W:2 * SCONV_W] * pn[:, 2 * SCONV_W:])
    rows = lax.broadcasted_iota(jnp.int32, cx.shape, 0)
    cprev = jnp.where(rows == 0, prev_row, pltpu.roll(cx, 1, 0))
    cnext = jnp.where(rows == tm - 1, next_row, pltpu.roll(cx, tm - 1, 0))
    br_b = bg * (cw_ref[0:1, :] * cprev + cw_ref[1:2, :] * cx + cw_ref[2:3, :] * cnext)
    br_c = oc_ref[...]
    mixed = None
    for n, br in enumerate((br_a, br_b, br_c)):
        t = _dot(br.astype(BF16), wb_ref[n]) * jax.nn.sigmoid(gt_ref[:, n * D_MODEL:(n + 1) * D_MODEL])
        mixed = t if mixed is None else mixed + t
    y = _dot(mixed.astype(BF16), wo_ref[...])
    o_ref[...] = h_ref[...] + mod_ref[0, 2:3, :] * y


def _merge(o_f, o_b, z, psc, oc, gates, h, mod, mod_idx, gnw_row, conv_b_w, wb, wo, seq_len):
    nt, d = h.shape
    tm = 256
    tps = seq_len // tm
    nblk8 = nt // 8
    row = lambda i: (i, 0)
    return pl.pallas_call(
        functools.partial(_merge_body, tps=tps),
        grid=(nt // tm,),
        in_specs=[pl.BlockSpec((tm, GDN_W), row),
                  pl.BlockSpec((tm, GDN_W), row),
                  pl.BlockSpec((tm, W_Z), row),
                  pl.BlockSpec((tm, W_SC), row),
                  pl.BlockSpec((8, W_SC), lambda i: (jnp.maximum(i * (tm // 8) - 1, 0), 0)),
                  pl.BlockSpec((8, W_SC), lambda i: (jnp.minimum((i + 1) * (tm // 8), nblk8 - 1), 0)),
                  pl.BlockSpec((tm, W_Q), row),
                  pl.BlockSpec((tm, W_G), row),
                  pl.BlockSpec((tm, d), row),
                  pl.BlockSpec((1, 6, d), lambda i: (mod_idx(i, tm), 0, 0)),
                  pl.BlockSpec((1, GDN_DV), lambda i: (0, 0)),
                  pl.BlockSpec((3, SCONV_W), lambda i: (0, 0)),
                  pl.BlockSpec((3, SCONV_W, d), lambda i: (0, 0, 0)),
                  pl.BlockSpec((d, d), lambda i: (0, 0))],
        out_specs=pl.BlockSpec((tm, d), row),
        out_shape=jax.ShapeDtypeStruct((nt, d), F32),
        compiler_params=_cparams(("parallel",)),
        name="merge",
    )(o_f, o_b, z, psc, psc, psc, oc, gates, h, mod, gnw_row, conv_b_w, wb, wo)


PEER_NO_RANK = 64.0


def _top_rows(s, count):
    rows = []
    rank = jnp.full(s.shape, PEER_NO_RANK, F32)
    for r in range(count):
        m = jnp.max(s, axis=0, keepdims=True)
        rows.append(m)
        hit = s == m
        rank = jnp.where(hit, float(r + 1), rank)
        s = jnp.where(hit, -jnp.inf, s)
    return rows, rank


def _peer_body(h_ref, mod_ref, nw_ref, wq_ref, sk_ref, u_ref, vt_ref, o_ref,
               xt_scr, out_scr, sc_scr, cnt_scr, ea_scr, rank_scr, eb_scr, cand_scr, gw_scr,
               *, tm, ec, nchunks):
    c = pl.program_id(1)
    nk = PEER_NKEYS
    half = nk // 2
    tb_w = PEER_TB
    ntb = tm // tb_w
    nlb = tm // 128
    lb_per_tb = tb_w // 128

    @pl.when(c == 0)
    def _():
        u = _rms_mod(h_ref[...], nw_ref[...], mod_ref[0, 3:4, :], mod_ref[0, 4:5, :])
        ut = u.T
        uh = ut.astype(BF16)
        qt = _dot(wq_ref[...], uh)
        for tb in range(ntb):
            xt_scr[tb] = uh[:, tb * tb_w:(tb + 1) * tb_w]
            out_scr[tb] = jnp.zeros(out_scr.shape[1:], F32)
        for h in range(PEER_HEADS):
            for p in range(2):
                qs = qt[h * nk + p * half:h * nk + (p + 1) * half, :]
                s = _dot(sk_ref[h, p], qs.astype(BF16))
                for lb in range(nlb):
                    sc_scr[p, h * nlb + lb] = s[:, lb * 128:(lb + 1) * 128]

        def select_one(hl):
            s0 = sc_scr[0, hl]
            s1 = sc_scr[1, hl]
            top0, _ = _top_rows(s0, PEER_TOPK)
            top1, rank1 = _top_rows(s1, PEER_TOPK)
            pairs = [(k, l) for k in range(PEER_TOPK) for l in range(PEER_TOPK) if (k + 1) * (l + 1) <= PEER_TOPK]
            cand = cand_scr.at[hl % 2]
            cand[...] = jnp.full(cand.shape, -jnp.inf, F32)
            sums = {}
            for r, (k, l) in enumerate(pairs):
                sums[k, l] = top0[k] + top1[l]
                cand[r:r + 1, :] = sums[k, l]
            best, _ = _top_rows(cand[...], PEER_TOPK)
            tau = best[PEER_TOPK - 1]
            zsum = jnp.zeros_like(tau)
            for b_ in best:
                zsum = zsum + jnp.exp(b_ - best[0])
            cnt = jnp.zeros(s0.shape, F32)
            for k in range(PEER_TOPK):
                n_sel = jnp.zeros_like(tau)
                for l in range(PEER_TOPK // (k + 1)):
                    n_sel = n_sel + jnp.where(sums[k, l] >= tau, 1.0, 0.0)
                cnt = jnp.where(s0 == top0[k], n_sel, cnt)
            cnt_scr[hl] = cnt.reshape(nk // 8, 8, 128)
            ea_scr[hl] = (jnp.exp(s0 - top0[0]) / zsum).reshape(nk // 8, 8, 128)
            rank_scr[hl] = rank1.astype(BF16)
            eb_scr[hl] = jnp.exp(s1 - top1[0]).astype(BF16)

        def select(pair, carry):
            select_one(2 * pair)
            select_one(2 * pair + 1)
            return carry

        lax.fori_loop(0, PEER_HEADS * nlb // 2, select, 0)

    nblk = ec // nk
    piece = 16
    npiece = nk // piece

    def half_dots(lhs_ref, rhs):
        hm = lhs_ref.shape[0] // 2
        return [_dot(lhs_ref[0:hm, :], rhs), _dot(lhs_ref[hm:2 * hm, :], rhs)]

    def gated_activation(tb, acts):
        hm = ec // 2
        for sub in range(lb_per_tb):
            lb = tb * lb_per_tb + sub
            ls = slice(sub * 128, (sub + 1) * 128)
            for ii in range(nblk):
                wm = [None] * npiece
                for h in range(PEER_HEADS):
                    cnt_row = jnp.broadcast_to(cnt_scr[h * nlb + lb, c * (nblk // 8) + ii // 8, ii % 8:ii % 8 + 1, :], (piece, 128)).astype(BF16)
                    ea_row = jnp.broadcast_to(ea_scr[h * nlb + lb, c * (nblk // 8) + ii // 8, ii % 8:ii % 8 + 1, :], (piece, 128)).astype(BF16)
                    for jp in range(npiece):
                        js = slice(jp * piece, (jp + 1) * piece)
                        sel = rank_scr[h * nlb + lb, js, :] <= cnt_row
                        w = eb_scr[h * nlb + lb, js, :] * ea_row
                        w = jnp.where(sel, w, jnp.zeros_like(w))
                        wm[jp] = w if h == 0 else wm[jp] + w
                for jp in range(npiece):
                    r0 = ii * nk + jp * piece
                    ap = acts[r0 // hm][r0 % hm:r0 % hm + piece, ls]
                    gl = 0.5 * ap * (1.0 + lax.erf(ap * (0.5 ** 0.5)))
                    gw_scr[tb, r0:r0 + piece, ls] = gl.astype(BF16) * wm[jp]

    acts = [half_dots(u_ref, xt_scr[tb]) for tb in range(ntb)]
    for tb in range(ntb):
        gated_activation(tb, acts[tb])
        outs = half_dots(vt_ref.at[0], gw_scr[tb])
        hd = out_scr.shape[1] // 2
        out_scr[tb, 0:hd, :] += outs[0]
        out_scr[tb, hd:2 * hd, :] += outs[1]

    @pl.when(c == nchunks - 1)
    def _():
        out = jnp.concatenate([out_scr[tb] for tb in range(ntb)], axis=1)
        o_ref[...] = h_ref[...] + mod_ref[0, 5:6, :] * out.T


def _peer(h, mod, mod_idx, nw, wq_t, sk, u_tab, vt_tab):
    nt, d = h.shape
    tm = PEER_TM
    ec = PEER_EC
    ne = u_tab.shape[0]
    nchunks = ne // ec
    nk = PEER_NKEYS
    ntb = tm // PEER_TB
    nlb = tm // 128
    const2 = lambda i, c: (0, 0)
    per_head = lambda dt: pltpu.VMEM((PEER_HEADS * nlb, nk, 128), dt)
    per_head_rows = pltpu.VMEM((PEER_HEADS * nlb, nk // 8, 8, 128), F32)
    return pl.pallas_call(
        functools.partial(_peer_body, tm=tm, ec=ec, nchunks=nchunks),
        grid=(nt // tm, nchunks),
        in_specs=[pl.BlockSpec((tm, d), lambda i, c: (i, 0)),
                  pl.BlockSpec((1, 6, d), lambda i, c: (mod_idx(i, tm), 0, 0)),
                  pl.BlockSpec((1, d), const2),
                  pl.BlockSpec((d, d), const2),
                  pl.BlockSpec((PEER_HEADS, 2, nk, nk // 2), lambda i, c: (0, 0, 0, 0)),
                  pl.BlockSpec((ec, d), lambda i, c: (c, 0)),
                  pl.BlockSpec((1, d, ec), lambda i, c: (c, 0, 0))],
        out_specs=pl.BlockSpec((tm, d), lambda i, c: (i, 0)),
        out_shape=jax.ShapeDtypeStruct((nt, d), F32),
        scratch_shapes=[pltpu.VMEM((ntb, d, PEER_TB), BF16),
                        pltpu.VMEM((ntb, d, PEER_TB), F32),
                        pltpu.VMEM((2, PEER_HEADS * nlb, nk, 128), F32),
                        per_head_rows,
                        per_head_rows,
                        per_head(BF16),
                        per_head(BF16),
                        pltpu.VMEM((2, 64, 128), F32),
                        pltpu.VMEM((ntb, ec, PEER_TB), BF16)],
        compiler_params=_cparams(("parallel", "arbitrary")),
        name="peer",
    )(h, mod, nw.reshape(1, d), wq_t, sk, u_tab, vt_tab)


def _final_norm_body(h_ref, w_ref, o_ref):
    x = h_ref[...]
    ms = jnp.mean(x * x, axis=-1, keepdims=True)
    o_ref[...] = x * lax.rsqrt(ms + NORM_EPS) * w_ref[...]


def _final_norm(h, w):
    nt, d = h.shape
    tm = 512
    return pl.pallas_call(
        _final_norm_body,
        grid=(nt // tm,),
        in_specs=[pl.BlockSpec((tm, d), lambda i: (i, 0)), pl.BlockSpec((1, d), lambda i: (0, 0))],
        out_specs=pl.BlockSpec((tm, d), lambda i: (i, 0)),
        out_shape=jax.ShapeDtypeStruct((nt, d), F32),
        compiler_params=_cparams(("parallel",)),
        name="final_norm",
    )(h, w.reshape(1, d))


def _rope_tables(seq):
    t = jnp.arange(seq)
    row = (t // GRID_W).astype(F32)
    col = (t % GRID_W).astype(F32)
    freqs = ROPE_THETA ** (-jnp.arange(ROPE_PAIRS, dtype=F32) / ROPE_PAIRS)
    ar = row[:, None] * freqs
    ac = col[:, None] * freqs
    cos = jnp.concatenate([jnp.cos(ar), jnp.cos(ar), jnp.cos(ac), jnp.cos(ac)], axis=1)
    sin = jnp.concatenate([-jnp.sin(ar), jnp.sin(ar), -jnp.sin(ac), jnp.sin(ac)], axis=1)
    return jnp.tile(cos, (1, ATT_HEADS)), jnp.tile(sin, (1, ATT_HEADS))


def _reorder_w_in(w):
    sizes = (W_QKV, W_Z, 4 * GDN_HEADS, W_SC, W_Q, W_KV, W_G)
    parts, start = [], 0
    for s in sizes:
        parts.append(w[:, start:start + s])
        start += s
    qkv, z, ab, sc, q, kv, g = parts
    ab = jnp.pad(ab, ((0, 0), (0, W_AB - ab.shape[1])))
    return jnp.concatenate([qkv, z, sc, q, kv, g, ab], axis=1).astype(BF16)


def _pad_row(v, n=128):
    v = v.reshape(1, -1)
    return jnp.pad(v, ((0, 0), (0, n - v.shape[1])))


def kernel(x, c, ctx, c_ctx, w_mod, b_mod, norm1_w, w_in, conv_a_w, a_log, dt_bias, gdn_norm_w, conv_b_w,
           q_norm_w, k_norm_w, w_branch, w_out, norm2_w, w_query, sub_keys, expert_u, expert_v, final_norm_w):
    batch, seq, d = x.shape
    ctx_len = ctx.shape[1]
    depth = w_mod.shape[0]
    h = x.reshape(batch * seq, d)
    hc = ctx.reshape(batch * ctx_len, d)

    mod_rows = -(-(batch + 1) // 8) * 8
    c_all = jnp.zeros((mod_rows, d), F32).at[:batch].set(c).at[batch].set(c_ctx)
    cos, sin = _rope_tables(seq)
    hd_i = jnp.arange(W_Q) // ATT_HD
    bd = (hd_i[:, None] == hd_i[None, :]).astype(BF16)
    lat_idx = lambda i, tm: i // (seq // tm)
    ctx_idx = lambda i, tm: batch

    for layer in range(depth):
        last = layer == depth - 1
        mod = _modulation(c_all, w_mod[layer], b_mod[layer]).reshape(mod_rows, 6, d)
        w_in_r = _reorder_w_in(w_in[layer])
        alog_row = _pad_row(a_log[layer])
        dtb_row = _pad_row(dt_bias[layer])
        qw_row = jnp.tile(q_norm_w[layer], ATT_HEADS).reshape(1, W_Q)
        kw_row = jnp.tile(k_norm_w[layer], ATT_KV_HEADS).reshape(1, ATT_KV_HEADS * ATT_HD)
        gnw_row = gdn_norm_w[layer].reshape(1, GDN_DV)
        wb = w_branch[layer].astype(BF16)
        wo = w_out[layer].astype(BF16)
        wq_t = w_query[layer].T.astype(BF16)
        sk = sub_keys[layer].astype(BF16)
        u_tab = expert_u[layer].astype(BF16)
        vt_tab = expert_v[layer].astype(BF16).reshape(-1, PEER_EC, d).transpose(0, 2, 1)

        pl_ = _inproj(h, mod, lat_idx, norm1_w[layer], w_in_r)
        pc_ = _inproj(hc, mod, ctx_idx, norm1_w[layer], w_in_r)

        prep_c = _gdn_prep(pc_[0], pc_[6], conv_a_w[layer], alog_row, dtb_row, ctx_len)
        prep_l = _gdn_prep(pl_[0], pl_[6], conv_a_w[layer], alog_row, dtb_row, seq)
        s_zero = jnp.zeros((batch, 2, GDN_HEADS, GDN_DK, GDN_DV), F32)
        ocf, ocb, s_ctx = _gdn_scan(prep_c, s_zero, batch, ctx_len)
        olf, olb, _ = _gdn_scan(prep_l, s_ctx, batch, seq)

        qh_c, kh_c, vh_c = _attn_prep(pc_[3], pc_[4], qw_row, kw_row, bd)
        qh_l, kh_l, vh_l = _attn_prep(pl_[3], pl_[4], qw_row, kw_row, bd, cos, sin, seq)
        att_l = _attention(qh_l, [(kh_l, vh_l, seq), (kh_c, vh_c, ctx_len)], seq)

        h = _merge(olf, olb, pl_[1], pl_[2], att_l, pl_[5], h, mod, lat_idx, gnw_row, conv_b_w[layer],
                   wb, wo, seq)
        h = _peer(h, mod, lat_idx, norm2_w[layer], wq_t, sk, u_tab, vt_tab)
        if not last:
            att_c = _attention(qh_c, [(kh_c, vh_c, ctx_len)], ctx_len)
            hc = _merge(ocf, ocb, pc_[1], pc_[2], att_c, pc_[5], hc, mod, ctx_idx, gnw_row, conv_b_w[layer],
                        wb, wo, ctx_len)
            hc = _peer(hc, mod, ctx_idx, norm2_w[layer], wq_t, sk, u_tab, vt_tab)
    return _final_norm(h, final_norm_w).reshape(batch, seq, d)
```

```python
import functools
import math

import jax
import jax.numpy as jnp
from jax import lax
from jax.experimental import pallas as pl
from jax.experimental.pallas import tpu as pltpu

F32 = jnp.float32
BF16 = jnp.bfloat16

D_MODEL = 1024
GRID_W = 64
GDN_HEADS = 4
GDN_DK = 128
GDN_DV = 128
GDN_CHUNK = 64
GDN_TILE = 256
GDN_W = GDN_HEADS * GDN_DV
SCONV_W = 512
ATT_HEADS = 8
ATT_KV_HEADS = 2
ATT_GROUP = 4
ATT_HD = 64
ROPE_THETA = 10000.0
ROPE_PAIRS = 16
PEER_HEADS = 8
PEER_NKEYS = 128
PEER_TOPK = 16
PEER_TB = 256
PEER_TM = 512
PEER_EC = 2048
NORM_EPS = 1e-6

W_QKV = 3 * GDN_W
W_Z = GDN_W
W_SC = 3 * SCONV_W
W_Q = ATT_HEADS * ATT_HD
W_KV = 2 * ATT_KV_HEADS * ATT_HD
W_G = 3 * D_MODEL
W_AB = 128
INPROJ_WIDTHS = (W_QKV, W_Z, W_SC, W_Q, W_KV, W_G, W_AB)

VMEM_LIMIT = 56 * 1024 * 1024


def _cparams(sem):
    return pltpu.CompilerParams(dimension_semantics=sem, vmem_limit_bytes=VMEM_LIMIT)


def _split_bf16(x):
    hi = x.astype(BF16)
    lo = (x - hi.astype(F32)).astype(BF16)
    return hi, lo


def _dot(a, b):
    return jnp.dot(a, b, preferred_element_type=F32)


def _dot_nt(a, b):
    return lax.dot_general(a, b, (((1,), (1,)), ((), ())), preferred_element_type=F32)


def _dot_tn(a, b):
    return lax.dot_general(a, b, (((0,), (0,)), ((), ())), preferred_element_type=F32)


def _dot3(a, b):
    ah, al = _split_bf16(a)
    bh, bl = _split_bf16(b)
    return _dot(ah, bh) + _dot(ah, bl) + _dot(al, bh)


def _rms_mod(x, nw, shift, scale):
    ms = jnp.mean(x * x, axis=-1, keepdims=True)
    return (x * lax.rsqrt(ms + NORM_EPS) * nw) * (1.0 + scale) + shift


def _mod_body(c_ref, w_ref, b_ref, o_ref):
    c = c_ref[...]
    a = c * jax.nn.sigmoid(c)
    o_ref[...] = _dot3(a, w_ref[...]) + b_ref[...]


def _modulation(c_all, w_mod, b_mod):
    rows, d = c_all.shape
    n = w_mod.shape[1]
    tn = 1536
    return pl.pallas_call(
        _mod_body,
        grid=(n // tn,),
        in_specs=[pl.BlockSpec((rows, d), lambda j: (0, 0)),
                  pl.BlockSpec((d, tn), lambda j: (0, j)),
                  pl.BlockSpec((1, tn), lambda j: (0, j))],
        out_specs=pl.BlockSpec((rows, tn), lambda j: (0, j)),
        out_shape=jax.ShapeDtypeStruct((rows, n), F32),
        compiler_params=_cparams(("parallel",)),
        name="modulation",
    )(c_all, w_mod, b_mod.reshape(1, n))


def _inproj_body(h_ref, mod_ref, nw_ref, w_ref, *outs):
    u = _rms_mod(h_ref[...], nw_ref[...], mod_ref[0, 0:1, :], mod_ref[0, 1:2, :])
    ub = u.astype(BF16)
    off = 0
    for o_ref, wd in zip(outs, INPROJ_WIDTHS):
        o_ref[...] = _dot(ub, w_ref[:, off:off + wd])
        off += wd


def _inproj(h, mod, mod_idx, nw, w):
    nt, d = h.shape
    tm = 256
    wtot = w.shape[1]
    return pl.pallas_call(
        _inproj_body,
        grid=(nt // tm,),
        in_specs=[pl.BlockSpec((tm, d), lambda i: (i, 0)),
                  pl.BlockSpec((1, 6, d), lambda i: (mod_idx(i, tm), 0, 0)),
                  pl.BlockSpec((1, d), lambda i: (0, 0)),
                  pl.BlockSpec((d, wtot), lambda i: (0, 0), pipeline_mode=pl.Buffered(1))],
        out_specs=[pl.BlockSpec((tm, wd), lambda i: (i, 0)) for wd in INPROJ_WIDTHS],
        out_shape=[jax.ShapeDtypeStruct((nt, wd), F32) for wd in INPROJ_WIDTHS],
        compiler_params=_cparams(("parallel",)),
        name="inproj",
    )(h, mod, nw.reshape(1, d), w)


def _bdot(a, b):
    return lax.dot_general(a, b, (((2,), (1,)), ((0,), (0,))), preferred_element_type=F32)


def _tri_inverse_minus_eye(n):
    nb = n.astype(BF16)
    p = _bdot(nb, nb)
    y = -n
    for step in range(5):
        y = y + p + _bdot(y.astype(BF16), p.astype(BF16))
        if step < 4:
            pb = p.astype(BF16)
            p = _bdot(pb, pb)
    return y


def _gdn_prep_body(x_ref, xp_ref, xn_ref, ab_ref, cw_ref, alog_ref, dtb_ref, *outs, tps):
    i = pl.program_id(0)
    first = (i % tps) == 0
    last = (i % tps) == tps - 1
    x = x_ref[...]
    nrow = x.shape[0]
    c = GDN_CHUNK
    cpt = nrow // c
    rows = lax.broadcasted_iota(jnp.int32, x.shape, 0)
    prev_row = jnp.where(first, 0.0, xp_ref[7:8, :])
    next_row = jnp.where(last, 0.0, xn_ref[0:1, :])
    xprev = jnp.where(rows == 0, prev_row, pltpu.roll(x, 1, 0))
    xnext = jnp.where(rows == nrow - 1, next_row, pltpu.roll(x, nrow - 1, 0))
    y = cw_ref[0:1, :] * xprev + cw_ref[1:2, :] * x + cw_ref[2:3, :] * xnext
    y = y * jax.nn.sigmoid(y)

    ab = ab_ref[...]
    zz = ab + dtb_ref[...]
    sp = jnp.maximum(zz, 0.0) + jnp.log1p(jnp.exp(-jnp.abs(zz)))
    g = -jnp.exp(alog_ref[...]) * sp
    beta = jax.nn.sigmoid(ab)

    ri = lax.broadcasted_iota(jnp.int32, (nrow, nrow), 0)
    ci = lax.broadcasted_iota(jnp.int32, (nrow, nrow), 1)
    same = (ri // c) == (ci // c)
    lower = (same & (ri >= ci)).astype(BF16)
    ones_blk = same.astype(BF16)
    g1 = g.astype(BF16)
    r1 = g - g1.astype(F32)
    g2 = r1.astype(BF16)
    g3 = (r1 - g2.astype(F32)).astype(BF16)
    gc_f = _dot(lower, g1) + _dot(lower, g2) + _dot(lower, g3)
    tot = _dot(ones_blk, g1) + _dot(ones_blk, g2) + _dot(ones_blk, g3)
    gc_b = tot - gc_f + g
    gcf_t = gc_f.T
    gcb_t = gc_b.T
    egc_f = jnp.exp(gc_f)
    egc_b = jnp.exp(gc_b)
    kdec_f = jnp.exp(tot - gc_f)
    kdec_b = jnp.exp(tot - gc_b)
    etot = jnp.exp(tot)

    pi = lax.broadcasted_iota(jnp.int32, (2 * c, 2 * c), 0)
    pj = lax.broadcasted_iota(jnp.int32, (2 * c, 2 * c), 1)
    top = (pi < c) & (pj < c)
    bot = (pi >= c) & (pj >= c)
    incl = (top & (pi >= pj)) | (bot & (pi <= pj))
    strict = (top & (pi > pj)) | (bot & (pi < pj))

    for d in range(2):
        outs[6 * d + 5][...] = jnp.zeros(outs[6 * d + 5].shape, F32)

    n_list, rhs_list, qk_list = [], [], []
    for h in range(GDN_HEADS):
        q = y[:, h * GDN_DK:(h + 1) * GDN_DK]
        k = y[:, GDN_W + h * GDN_DK:GDN_W + (h + 1) * GDN_DK]
        v = y[:, 2 * GDN_W + h * GDN_DV:2 * GDN_W + (h + 1) * GDN_DV]
        q = q * lax.rsqrt(jnp.sum(q * q, axis=-1, keepdims=True) + NORM_EPS) * (GDN_DK ** -0.5)
        k = k * lax.rsqrt(jnp.sum(k * k, axis=-1, keepdims=True) + NORM_EPS)
        cf, cb = h, GDN_HEADS + h
        bf_, bb_ = 2 * GDN_HEADS + h, 3 * GDN_HEADS + h
        sl = slice(h * GDN_DV, (h + 1) * GDN_DV)
        outs[2][:, sl] = (k * kdec_f[:, cf:cf + 1]).astype(BF16)
        outs[8][:, sl] = (k * kdec_b[:, cb:cb + 1]).astype(BF16)
        outs[3][:, sl] = (q * egc_f[:, cf:cf + 1]).astype(BF16)
        outs[9][:, sl] = (q * egc_b[:, cb:cb + 1]).astype(BF16)
        for ch in range(cpt):
            r = slice(ch * c, (ch + 1) * c)
            k2 = jnp.concatenate([k[r], k[r]], axis=0)
            q2 = jnp.concatenate([q[r], q[r]], axis=0)
            v2 = jnp.concatenate([v[r], v[r]], axis=0)
            kb2 = k2.astype(BF16)
            kk2 = _dot_nt(kb2, kb2)
            qk2 = _dot_nt(q2.astype(BF16), kb2)
            gcol = jnp.concatenate([gc_f[r, cf:cf + 1], gc_b[r, cb:cb + 1]], axis=0)
            grow = jnp.concatenate([gcf_t[cf:cf + 1, r], gcb_t[cb:cb + 1, r]], axis=1)
            bcol = jnp.concatenate([beta[r, bf_:bf_ + 1], beta[r, bb_:bb_ + 1]], axis=0)
            ecol = jnp.concatenate([egc_f[r, cf:cf + 1], egc_b[r, cb:cb + 1]], axis=0)
            ldec = jnp.where(incl, jnp.exp(jnp.where(incl, gcol - grow, 0.0)), 0.0)
            n_list.append(jnp.where(strict, kk2 * bcol * ldec, 0.0))
            rhs_list.append(jnp.concatenate([v2 * bcol, k2 * (bcol * ecol)], axis=1))
            qk_list.append(qk2 * ldec)
            for d, col in ((0, cf), (1, cb)):
                outs[6 * d + 5][ch, h:h + 1, :] = jnp.broadcast_to(etot[ch * c:ch * c + 1, col:col + 1], (1, 128))

    yinv = _tri_inverse_minus_eye(jnp.stack(n_list, axis=0))
    rhs = jnp.stack(rhs_list, axis=0)
    sol = rhs + _bdot(yinv.astype(BF16), rhs.astype(BF16))
    for h in range(GDN_HEADS):
        sl = slice(h * GDN_DV, (h + 1) * GDN_DV)
        for ch in range(cpt):
            r = slice(ch * c, (ch + 1) * c)
            s = sol[h * cpt + ch]
            qk = qk_list[h * cpt + ch]
            outs[0][r, sl] = s[:c, :GDN_DV]
            outs[1][r, sl] = s[:c, GDN_DV:].astype(BF16)
            outs[6][r, sl] = s[c:, :GDN_DV]
            outs[7][r, sl] = s[c:, GDN_DV:].astype(BF16)
            outs[4][r, h * c:(h + 1) * c] = qk[:c, :c].astype(BF16)
            outs[10][r, h * c:(h + 1) * c] = qk[c:, c:].astype(BF16)


def _gdn_prep(qkv, ab, conv_w, alog_row, dtb_row, seq_len):
    nt = qkv.shape[0]
    c = GDN_CHUNK
    tr = GDN_TILE
    cpt = tr // c
    tps = seq_len // tr
    nblk8 = nt // 8
    out_shape, out_specs = [], []
    for _ in range(2):
        for wd, dt in ((GDN_W, F32), (GDN_W, BF16), (GDN_W, BF16), (GDN_W, BF16), (GDN_HEADS * c, BF16)):
            out_shape.append(jax.ShapeDtypeStruct((nt, wd), dt))
            out_specs.append(pl.BlockSpec((tr, wd), lambda i: (i, 0)))
        out_shape.append(jax.ShapeDtypeStruct((nt // c, 8, 128), F32))
        out_specs.append(pl.BlockSpec((cpt, 8, 128), lambda i: (i, 0, 0)))
    return pl.pallas_call(
        functools.partial(_gdn_prep_body, tps=tps),
        grid=(nt // tr,),
        in_specs=[pl.BlockSpec((tr, W_QKV), lambda i: (i, 0)),
                  pl.BlockSpec((8, W_QKV), lambda i: (jnp.maximum(i * (tr // 8) - 1, 0), 0)),
                  pl.BlockSpec((8, W_QKV), lambda i: (jnp.minimum((i + 1) * (tr // 8), nblk8 - 1), 0)),
                  pl.BlockSpec((tr, W_AB), lambda i: (i, 0)),
                  pl.BlockSpec((3, W_QKV), lambda i: (0, 0)),
                  pl.BlockSpec((1, 128), lambda i: (0, 0)),
                  pl.BlockSpec((1, 128), lambda i: (0, 0))],
        out_specs=out_specs,
        out_shape=out_shape,
        compiler_params=_cparams(("parallel",)),
        name="gdn_prep",
    )(qkv, qkv, qkv, ab, conv_w, alog_row, dtb_row)


def _gdn_scan_body(*refs, nch):
    ins = refs[:12]
    s0_ref = refs[12]
    o_refs = refs[13:15]
    sfin_ref = refs[15]
    s_scr = refs[16]
    n = pl.program_id(1)
    c = GDN_CHUNK

    @pl.when(n == 0)
    def _():
        s_scr[...] = s0_ref[0]

    chains = [(d, h) for d in range(2) for h in range(GDN_HEADS)]
    state, prod, vnew = {}, {}, {}
    for d, h in chains:
        w_ref, qd_ref = ins[6 * d + 1], ins[6 * d + 3]
        sl = slice(h * GDN_DV, (h + 1) * GDN_DV)
        state[d, h] = s_scr[d, h]
        wq = jnp.concatenate([w_ref[:, sl], qd_ref[:, sl]], axis=0)
        prod[d, h] = _dot(wq, state[d, h].astype(BF16))
    for d, h in chains:
        u_ref, qk_ref = ins[6 * d], ins[6 * d + 4]
        sl = slice(h * GDN_DV, (h + 1) * GDN_DV)
        r = prod[d, h]
        vnew[d, h] = (u_ref[:, sl] - r[:c]).astype(BF16)
        o_refs[d][:, sl] = r[c:] + _dot(qk_ref[:, h * c:(h + 1) * c], vnew[d, h])
    for d, h in chains:
        kd_ref, eg_ref = ins[6 * d + 2], ins[6 * d + 5]
        sl = slice(h * GDN_DV, (h + 1) * GDN_DV)
        s_scr[d, h] = state[d, h] * eg_ref[0, h:h + 1, :] + _dot_tn(kd_ref[:, sl], vnew[d, h])

    @pl.when(n == nch - 1)
    def _():
        sfin_ref[0] = s_scr[...]


def _gdn_scan(prep, s0, batch, seq_len):
    c = GDN_CHUNK
    nch = seq_len // c
    nt = batch * seq_len
    fwd = lambda b, n: (b * nch + n, 0)
    bwd = lambda b, n: (b * nch + (nch - 1 - n), 0)
    in_specs = []
    for d, im in enumerate((fwd, bwd)):
        for wd in (GDN_W, GDN_W, GDN_W, GDN_W, GDN_HEADS * c):
            in_specs.append(pl.BlockSpec((c, wd), im))
        in_specs.append(pl.BlockSpec((1, 8, 128), (lambda im_: (lambda b, n: im_(b, n) + (0,)))(im)))
    st_spec = pl.BlockSpec((1, 2, GDN_HEADS, GDN_DK, GDN_DV), lambda b, n: (b, 0, 0, 0, 0))
    in_specs.append(st_spec)
    o_f, o_b, s_fin = pl.pallas_call(
        functools.partial(_gdn_scan_body, nch=nch),
        grid=(batch, nch),
        in_specs=in_specs,
        out_specs=[pl.BlockSpec((c, GDN_W), fwd), pl.BlockSpec((c, GDN_W), bwd), st_spec],
        out_shape=[jax.ShapeDtypeStruct((nt, GDN_W), F32), jax.ShapeDtypeStruct((nt, GDN_W), F32),
                   jax.ShapeDtypeStruct(s0.shape, F32)],
        scratch_shapes=[pltpu.VMEM((2, GDN_HEADS, GDN_DK, GDN_DV), F32)],
        compiler_params=_cparams(("arbitrary", "arbitrary")),
        name="gdn_scan",
    )(*prep, s0)
    return o_f, o_b, s_fin


def _group_norm_rope(x, w, bd, cos, sin):
    xx = x * x
    hi, lo = _split_bf16(xx)
    ss = _dot(hi, bd) + _dot(lo, bd)
    xn = x * lax.rsqrt(ss * (1.0 / ATT_HD) + NORM_EPS) * w
    if cos is not None:
        width = x.shape[1]
        lane = lax.broadcasted_iota(jnp.int32, x.shape, 1)
        from_below = pltpu.roll(xn, ROPE_PAIRS, 1)
        from_above = pltpu.roll(xn, width - ROPE_PAIRS, 1)
        partner = jnp.where((lane % (2 * ROPE_PAIRS)) < ROPE_PAIRS, from_above, from_below)
        xn = xn * cos + partner * sin
    return xn


def _attn_prep_body(*refs, rope):
    if rope:
        q_ref, kv_ref, cos_ref, sin_ref, qw_ref, kw_ref, bd_ref, qh_ref, kh_ref, vh_ref = refs
        cos, sin = cos_ref[...], sin_ref[...]
        kcos, ksin = cos[:, :ATT_KV_HEADS * ATT_HD], sin[:, :ATT_KV_HEADS * ATT_HD]
    else:
        q_ref, kv_ref, qw_ref, kw_ref, bd_ref, qh_ref, kh_ref, vh_ref = refs
        cos = sin = kcos = ksin = None
    nk = ATT_KV_HEADS * ATT_HD
    bd = bd_ref[...]
    q = _group_norm_rope(q_ref[...], qw_ref[...], bd, cos, sin) * (ATT_HD ** -0.5 * math.log2(math.e))
    kv = kv_ref[...]
    k = _group_norm_rope(kv[:, :nk], kw_ref[...], bd[:nk, :nk], kcos, ksin)
    v = kv[:, nk:]
    for j in range(ATT_HEADS):
        qh_ref[j] = q[:, j * ATT_HD:(j + 1) * ATT_HD].astype(BF16)
    for j in range(ATT_KV_HEADS):
        kh_ref[j] = k[:, j * ATT_HD:(j + 1) * ATT_HD].astype(BF16)
        vh_ref[j] = v[:, j * ATT_HD:(j + 1) * ATT_HD].astype(BF16)


def _attn_prep(q, kv, qw_row, kw_row, bd, cos=None, sin=None, seq_len=None):
    nt = q.shape[0]
    tm = 256
    rope = cos is not None
    in_specs = [pl.BlockSpec((tm, W_Q), lambda i: (i, 0)),
                pl.BlockSpec((tm, W_KV), lambda i: (i, 0))]
    args = [q, kv]
    if rope:
        tps = seq_len // tm
        in_specs += [pl.BlockSpec((tm, W_Q), lambda i: (i % tps, 0)),
                     pl.BlockSpec((tm, W_Q), lambda i: (i % tps, 0))]
        args += [cos, sin]
    in_specs += [pl.BlockSpec((1, W_Q), lambda i: (0, 0)),
                 pl.BlockSpec((1, ATT_KV_HEADS * ATT_HD), lambda i: (0, 0)),
                 pl.BlockSpec((W_Q, W_Q), lambda i: (0, 0))]
    args += [qw_row, kw_row, bd]
    return pl.pallas_call(
        functools.partial(_attn_prep_body, rope=rope),
        grid=(nt // tm,),
        in_specs=in_specs,
        out_specs=[pl.BlockSpec((ATT_HEADS, tm, ATT_HD), lambda i: (0, i, 0)),
                   pl.BlockSpec((ATT_KV_HEADS, tm, ATT_HD), lambda i: (0, i, 0)),
                   pl.BlockSpec((ATT_KV_HEADS, tm, ATT_HD), lambda i: (0, i, 0))],
        out_shape=[jax.ShapeDtypeStruct((ATT_HEADS, nt, ATT_HD), BF16),
                   jax.ShapeDtypeStruct((ATT_KV_HEADS, nt, ATT_HD), BF16),
                   jax.ShapeDtypeStruct((ATT_KV_HEADS, nt, ATT_HD), BF16)],
        compiler_params=_cparams(("parallel",)),
        name="attn_prep",
    )(*args)


def _attention_body(q_ref, *refs, src_lens, tk, tq):
    n_src = len(src_lens)
    o_ref = refs[2 * n_src]
    rows = ATT_GROUP * tq
    q = q_ref[...].reshape(rows, ATT_HD)

    def block(sc, v, carry):
        m, l, acc = carry
        m_new = jnp.maximum(m, jnp.max(sc, axis=1, keepdims=True))
        alpha = jnp.exp2(m - m_new)
        p = jnp.exp2(sc - m_new)
        l = alpha * l + jnp.sum(p, axis=1, keepdims=True)
        acc = alpha * acc + _dot(p.astype(BF16), v)
        return m_new, l, acc

    carry = (jnp.full((rows, 1), -jnp.inf, F32), jnp.zeros((rows, 1), F32), jnp.zeros((rows, ATT_HD), F32))
    for s in range(n_src):
        k_ref, v_ref = refs[2 * s], refs[2 * s + 1]
        blk = min(tk, src_lens[s])
        nblk = src_lens[s] // blk
        for j in range(nblk):
            sc = _dot_nt(q, k_ref[0, j * blk:(j + 1) * blk, :])
            carry = block(sc, v_ref[0, j * blk:(j + 1) * blk, :], carry)
    m, l, acc = carry
    o = (acc / l).reshape(ATT_GROUP, tq, ATT_HD)
    o_ref[...] = jnp.concatenate([o[j] for j in range(ATT_GROUP)], axis=1)


def _attention(qh, sources, q_len):
    nq = qh.shape[1]
    tq = 128
    tk = 8192
    tpb = q_len // tq
    in_specs = [pl.BlockSpec((ATT_GROUP, tq, ATT_HD), lambda i, g: (g, i, 0))]
    args = [qh]
    for kh, vh, sl in sources:
        spec = pl.BlockSpec((1, sl, ATT_HD), lambda i, g: (g, i // tpb, 0))
        in_specs += [spec, spec]
        args += [kh, vh]
    return pl.pallas_call(
        functools.partial(_attention_body, src_lens=tuple(s[2] for s in sources), tk=tk, tq=tq),
        grid=(nq // tq, ATT_KV_HEADS),
        in_specs=in_specs,
        out_specs=pl.BlockSpec((tq, ATT_GROUP * ATT_HD), lambda i, g: (i, g)),
        out_shape=jax.ShapeDtypeStruct((nq, W_Q), F32),
        compiler_params=_cparams(("parallel", "parallel")),
        name="attention",
    )(*args)


def _merge_body(of_ref, ob_ref, z_ref, p_ref, pp_ref, pn_ref, oc_ref, gt_ref, h_ref, mod_ref,
                gnw_ref, cw_ref, wb_ref, wo_ref, o_ref, *, tps):
    i = pl.program_id(0)
    first = (i % tps) == 0
    last = (i % tps) == tps - 1
    o = of_ref[...] + ob_ref[...]
    z = z_ref[...]
    parts = []
    for h in range(GDN_HEADS):
        sl = slice(h * GDN_DV, (h + 1) * GDN_DV)
        oh = o[:, sl]
        oh = oh * lax.rsqrt(jnp.mean(oh * oh, axis=-1, keepdims=True) + NORM_EPS) * gnw_ref[...]
        zh = z[:, sl]
        parts.append(oh * (zh * jax.nn.sigmoid(zh)))
    br_a = jnp.concatenate(parts, axis=1)
    p = p_ref[...]
    tm = p.shape[0]
    bg = p[:, :SCONV_W]
    cx = p[:, SCONV_W:2 * SCONV_W] * p[:, 2 * SCONV_W:]
    pp = pp_ref[7:8, :]
    pn = pn_ref[0:1, :]
    prev_row = jnp.where(first, 0.0, pp[:, SCONV_W:2 * SCONV_W] * pp[:, 2 * SCONV_W:])
    next_row = jnp.where(last, 0.0, pn[:, SCONV_W:2 * SCONV_W] * pn[:, 2 * SCONV_W:])
    rows = lax.broadcasted_iota(jnp.int32, cx.shape, 0)
    cprev = jnp.where(rows == 0, prev_row, pltpu.roll(cx, 1, 0))
    cnext = jnp.where(rows == tm - 1, next_row, pltpu.roll(cx, tm - 1, 0))
    br_b = bg * (cw_ref[0:1, :] * cprev + cw_ref[1:2, :] * cx + cw_ref[2:3, :] * cnext)
    br_c = oc_ref[...]
    mixed = None
    for n, br in enumerate((br_a, br_b, br_c)):
        t = _dot(br.astype(BF16), wb_ref[n]) * jax.nn.sigmoid(gt_ref[:, n * D_MODEL:(n + 1) * D_MODEL])
        mixed = t if mixed is None else mixed + t
    y = _dot(mixed.astype(BF16), wo_ref[...])
    o_ref[...] = h_ref[...] + mod_ref[0, 2:3, :] * y


def _merge(o_f, o_b, z, psc, oc, gates, h, mod, mod_idx, gnw_row, conv_b_w, wb, wo, seq_len):
    nt, d = h.shape
    tm = 256
    tps = seq_len // tm
    nblk8 = nt // 8
    row = lambda i: (i, 0)
    return pl.pallas_call(
        functools.partial(_merge_body, tps=tps),
        grid=(nt // tm,),
        in_specs=[pl.BlockSpec((tm, GDN_W), row),
                  pl.BlockSpec((tm, GDN_W), row),
                  pl.BlockSpec((tm, W_Z), row),
                  pl.BlockSpec((tm, W_SC), row),
                  pl.BlockSpec((8, W_SC), lambda i: (jnp.maximum(i * (tm // 8) - 1, 0), 0)),
                  pl.BlockSpec((8, W_SC), lambda i: (jnp.minimum((i + 1) * (tm // 8), nblk8 - 1), 0)),
                  pl.BlockSpec((tm, W_Q), row),
                  pl.BlockSpec((tm, W_G), row),
                  pl.BlockSpec((tm, d), row),
                  pl.BlockSpec((1, 6, d), lambda i: (mod_idx(i, tm), 0, 0)),
                  pl.BlockSpec((1, GDN_DV), lambda i: (0, 0)),
                  pl.BlockSpec((3, SCONV_W), lambda i: (0, 0)),
                  pl.BlockSpec((3, SCONV_W, d), lambda i: (0, 0, 0)),
                  pl.BlockSpec((d, d), lambda i: (0, 0))],
        out_specs=pl.BlockSpec((tm, d), row),
        out_shape=jax.ShapeDtypeStruct((nt, d), F32),
        compiler_params=_cparams(("parallel",)),
        name="merge",
    )(o_f, o_b, z, psc, psc, psc, oc, gates, h, mod, gnw_row, conv_b_w, wb, wo)


PEER_NO_RANK = 64.0


def _top_rows(s, count):
    rows = []
    rank = jnp.full(s.shape, PEER_NO_RANK, F32)
    for r in range(count):
        m = jnp.max(s, axis=0, keepdims=True)
        rows.append(m)
        hit = s == m
        rank = jnp.where(hit, float(r + 1), rank)
        s = jnp.where(hit, -jnp.inf, s)
    return rows, rank


def _peer_body(h_ref, mod_ref, nw_ref, fw_ref, wq_ref, sk_ref, u_ref, vt_ref, o_ref,
               xt_scr, out_scr, sc_scr, cnt_scr, ea_scr, rank_scr, eb_scr, cand_scr, gw_scr,
               *, tm, ec, nchunks, final_norm):
    c = pl.program_id(1)
    nk = PEER_NKEYS
    half = nk // 2
    tb_w = PEER_TB
    ntb = tm // tb_w
    nlb = tm // 128
    lb_per_tb = tb_w // 128

    @pl.when(c == 0)
    def _():
        u = _rms_mod(h_ref[...], nw_ref[...], mod_ref[0, 3:4, :], mod_ref[0, 4:5, :])
        ut = u.T
        uh = ut.astype(BF16)
        qt = _dot(wq_ref[...], uh)
        for tb in range(ntb):
            xt_scr[tb] = uh[:, tb * tb_w:(tb + 1) * tb_w]
            out_scr[tb] = jnp.zeros(out_scr.shape[1:], F32)
        for h in range(PEER_HEADS):
            for p in range(2):
                qs = qt[h * nk + p * half:h * nk + (p + 1) * half, :]
                s = _dot(sk_ref[h, p], qs.astype(BF16))
                for lb in range(nlb):
                    sc_scr[p, h * nlb + lb] = s[:, lb * 128:(lb + 1) * 128]

        def select_one(hl):
            s0 = sc_scr[0, hl]
            s1 = sc_scr[1, hl]
            top0, _ = _top_rows(s0, PEER_TOPK)
            top1, rank1 = _top_rows(s1, PEER_TOPK)
            pairs = [(k, l) for k in range(PEER_TOPK) for l in range(PEER_TOPK) if (k + 1) * (l + 1) <= PEER_TOPK]
            cand = cand_scr.at[hl % 2]
            cand[...] = jnp.full(cand.shape, -jnp.inf, F32)
            sums = {}
            for r, (k, l) in enumerate(pairs):
                sums[k, l] = top0[k] + top1[l]
                cand[r:r + 1, :] = sums[k, l]
            best, _ = _top_rows(cand[...], PEER_TOPK)
            tau = best[PEER_TOPK - 1]
            zsum = jnp.zeros_like(tau)
            for b_ in best:
                zsum = zsum + jnp.exp(b_ - best[0])
            cnt = jnp.zeros(s0.shape, F32)
            for k in range(PEER_TOPK):
                n_sel = jnp.zeros_like(tau)
                for l in range(PEER_TOPK // (k + 1)):
                    n_sel = n_sel + jnp.where(sums[k, l] >= tau, 1.0, 0.0)
                cnt = jnp.where(s0 == top0[k], n_sel, cnt)
            cnt_scr[hl] = cnt.reshape(nk // 8, 8, 128)
            ea_scr[hl] = (jnp.exp(s0 - top0[0]) / zsum).reshape(nk // 8, 8, 128)
            rank_scr[hl] = rank1.astype(BF16)
            eb_scr[hl] = jnp.exp(s1 - top1[0]).astype(BF16)

        def select(pair, carry):
            select_one(2 * pair)
            select_one(2 * pair + 1)
            return carry

        lax.fori_loop(0, PEER_HEADS * nlb // 2, select, 0)

    nblk = ec // nk
    piece = 16
    npiece = nk // piece

    def half_dots(lhs_ref, rhs):
        hm = lhs_ref.shape[0] // 2
        return [_dot(lhs_ref[0:hm, :], rhs), _dot(lhs_ref[hm:2 * hm, :], rhs)]

    def gated_activation(tb, acts):
        hm = ec // 2
        for sub in range(lb_per_tb):
            lb = tb * lb_per_tb + sub
            ls = slice(sub * 128, (sub + 1) * 128)
            for ii in range(nblk):
                wm = [None] * npiece
                for h in range(PEER_HEADS):
                    cnt_row = jnp.broadcast_to(cnt_scr[h * nlb + lb, c * (nblk // 8) + ii // 8, ii % 8:ii % 8 + 1, :], (piece, 128)).astype(BF16)
                    ea_row = jnp.broadcast_to(ea_scr[h * nlb + lb, c * (nblk // 8) + ii // 8, ii % 8:ii % 8 + 1, :], (piece, 128)).astype(BF16)
                    for jp in range(npiece):
                        js = slice(jp * piece, (jp + 1) * piece)
                        sel = rank_scr[h * nlb + lb, js, :] <= cnt_row
                        w = eb_scr[h * nlb + lb, js, :] * ea_row
                        w = jnp.where(sel, w, jnp.zeros_like(w))
                        wm[jp] = w if h == 0 else wm[jp] + w
                for jp in range(npiece):
                    r0 = ii * nk + jp * piece
                    ap = acts[r0 // hm][r0 % hm:r0 % hm + piece, ls]
                    gl = 0.5 * ap * (1.0 + lax.erf(ap * (0.5 ** 0.5)))
                    gw_scr[tb, r0:r0 + piece, ls] = gl.astype(BF16) * wm[jp]

    acts = [half_dots(u_ref, xt_scr[tb]) for tb in range(ntb)]
    for tb in range(ntb):
        gated_activation(tb, acts[tb])
        outs = half_dots(vt_ref.at[0], gw_scr[tb])
        hd = out_scr.shape[1] // 2
        out_scr[tb, 0:hd, :] += outs[0]
        out_scr[tb, hd:2 * hd, :] += outs[1]

    @pl.when(c == nchunks - 1)
    def _():
        out = jnp.concatenate([out_scr[tb] for tb in range(ntb)], axis=1)
        hn = h_ref[...] + mod_ref[0, 5:6, :] * out.T
        if final_norm:
            hn = hn * lax.rsqrt(jnp.mean(hn * hn, axis=-1, keepdims=True) + NORM_EPS) * fw_ref[...]
        o_ref[...] = hn


def _peer(h, mod, mod_idx, nw, final_w, apply_final, wq_t, sk, u_tab, vt_tab):
    nt, d = h.shape
    tm = PEER_TM
    ec = PEER_EC
    ne = u_tab.shape[0]
    nchunks = ne // ec
    nk = PEER_NKEYS
    ntb = tm // PEER_TB
    nlb = tm // 128
    const2 = lambda i, c: (0, 0)
    per_head = lambda dt: pltpu.VMEM((PEER_HEADS * nlb, nk, 128), dt)
    per_head_rows = pltpu.VMEM((PEER_HEADS * nlb, nk // 8, 8, 128), F32)
    return pl.pallas_call(
        functools.partial(_peer_body, tm=tm, ec=ec, nchunks=nchunks, final_norm=apply_final),
        grid=(nt // tm, nchunks),
        in_specs=[pl.BlockSpec((tm, d), lambda i, c: (i, 0)),
                  pl.BlockSpec((1, 6, d), lambda i, c: (mod_idx(i, tm), 0, 0)),
                  pl.BlockSpec((1, d), const2),
                  pl.BlockSpec((1, d), const2),
                  pl.BlockSpec((d, d), const2),
                  pl.BlockSpec((PEER_HEADS, 2, nk, nk // 2), lambda i, c: (0, 0, 0, 0)),
                  pl.BlockSpec((ec, d), lambda i, c: (c, 0)),
                  pl.BlockSpec((1, d, ec), lambda i, c: (c, 0, 0))],
        out_specs=pl.BlockSpec((tm, d), lambda i, c: (i, 0)),
        out_shape=jax.ShapeDtypeStruct((nt, d), F32),
        scratch_shapes=[pltpu.VMEM((ntb, d, PEER_TB), BF16),
                        pltpu.VMEM((ntb, d, PEER_TB), F32),
                        pltpu.VMEM((2, PEER_HEADS * nlb, nk, 128), F32),
                        per_head_rows,
                        per_head_rows,
                        per_head(BF16),
                        per_head(BF16),
                        pltpu.VMEM((2, 64, 128), F32),
                        pltpu.VMEM((ntb, ec, PEER_TB), BF16)],
        compiler_params=_cparams(("parallel", "arbitrary")),
        name="peer",
    )(h, mod, nw.reshape(1, d), final_w.reshape(1, d), wq_t, sk, u_tab, vt_tab)


def _rope_tables(seq):
    t = jnp.arange(seq)
    row = (t // GRID_W).astype(F32)
    col = (t % GRID_W).astype(F32)
    freqs = ROPE_THETA ** (-jnp.arange(ROPE_PAIRS, dtype=F32) / ROPE_PAIRS)
    ar = row[:, None] * freqs
    ac = col[:, None] * freqs
    cos = jnp.concatenate([jnp.cos(ar), jnp.cos(ar), jnp.cos(ac), jnp.cos(ac)], axis=1)
    sin = jnp.concatenate([-jnp.sin(ar), jnp.sin(ar), -jnp.sin(ac), jnp.sin(ac)], axis=1)
    return jnp.tile(cos, (1, ATT_HEADS)), jnp.tile(sin, (1, ATT_HEADS))


def _reorder_w_in(w):
    sizes = (W_QKV, W_Z, 4 * GDN_HEADS, W_SC, W_Q, W_KV, W_G)
    parts, start = [], 0
    for s in sizes:
        parts.append(w[:, start:start + s])
        start += s
    qkv, z, ab, sc, q, kv, g = parts
    ab = jnp.pad(ab, ((0, 0), (0, W_AB - ab.shape[1])))
    return jnp.concatenate([qkv, z, sc, q, kv, g, ab], axis=1).astype(BF16)


def _pad_row(v, n=128):
    v = v.reshape(1, -1)
    return jnp.pad(v, ((0, 0), (0, n - v.shape[1])))


def kernel(x, c, ctx, c_ctx, w_mod, b_mod, norm1_w, w_in, conv_a_w, a_log, dt_bias, gdn_norm_w, conv_b_w,
           q_norm_w, k_norm_w, w_branch, w_out, norm2_w, w_query, sub_keys, expert_u, expert_v, final_norm_w):
    batch, seq, d = x.shape
    ctx_len = ctx.shape[1]
    depth = w_mod.shape[0]
    h = x.reshape(batch * seq, d)
    hc = ctx.reshape(batch * ctx_len, d)

    mod_rows = -(-(batch + 1) // 8) * 8
    c_all = jnp.zeros((mod_rows, d), F32).at[:batch].set(c).at[batch].set(c_ctx)
    cos, sin = _rope_tables(seq)
    hd_i = jnp.arange(W_Q) // ATT_HD
    bd = (hd_i[:, None] == hd_i[None, :]).astype(BF16)
    lat_idx = lambda i, tm: i // (seq // tm)
    ctx_idx = lambda i, tm: batch

    for layer in range(depth):
        last = layer == depth - 1
        mod = _modulation(c_all, w_mod[layer], b_mod[layer]).reshape(mod_rows, 6, d)
        w_in_r = _reorder_w_in(w_in[layer])
        alog_row = _pad_row(a_log[layer])
        dtb_row = _pad_row(dt_bias[layer])
        qw_row = jnp.tile(q_norm_w[layer], ATT_HEADS).reshape(1, W_Q)
        kw_row = jnp.tile(k_norm_w[layer], ATT_KV_HEADS).reshape(1, ATT_KV_HEADS * ATT_HD)
        gnw_row = gdn_norm_w[layer].reshape(1, GDN_DV)
        wb = w_branch[layer].astype(BF16)
        wo = w_out[layer].astype(BF16)
        wq_t = w_query[layer].T.astype(BF16)
        sk = sub_keys[layer].astype(BF16)
        u_tab = expert_u[layer].astype(BF16)
        vt_tab = expert_v[layer].astype(BF16).reshape(-1, PEER_EC, d).transpose(0, 2, 1)

        pl_ = _inproj(h, mod, lat_idx, norm1_w[layer], w_in_r)
        pc_ = _inproj(hc, mod, ctx_idx, norm1_w[layer], w_in_r)

        prep_c = _gdn_prep(pc_[0], pc_[6], conv_a_w[layer], alog_row, dtb_row, ctx_len)
        prep_l = _gdn_prep(pl_[0], pl_[6], conv_a_w[layer], alog_row, dtb_row, seq)
        s_zero = jnp.zeros((batch, 2, GDN_HEADS, GDN_DK, GDN_DV), F32)
        ocf, ocb, s_ctx = _gdn_scan(prep_c, s_zero, batch, ctx_len)
        olf, olb, _ = _gdn_scan(prep_l, s_ctx, batch, seq)

        qh_c, kh_c, vh_c = _attn_prep(pc_[3], pc_[4], qw_row, kw_row, bd)
        qh_l, kh_l, vh_l = _attn_prep(pl_[3], pl_[4], qw_row, kw_row, bd, cos, sin, seq)
        att_l = _attention(qh_l, [(kh_l, vh_l, seq), (kh_c, vh_c, ctx_len)], seq)

        h = _merge(olf, olb, pl_[1], pl_[2], att_l, pl_[5], h, mod, lat_idx, gnw_row, conv_b_w[layer],
                   wb, wo, seq)
        h = _peer(h, mod, lat_idx, norm2_w[layer], final_norm_w, last, wq_t, sk, u_tab, vt_tab)
        if not last:
            att_c = _attention(qh_c, [(kh_c, vh_c, ctx_len)], ctx_len)
            hc = _merge(ocf, ocb, pc_[1], pc_[2], att_c, pc_[5], hc, mod, ctx_idx, gnw_row, conv_b_w[layer],
                        wb, wo, ctx_len)
            hc = _peer(hc, mod, ctx_idx, norm2_w[layer], final_norm_w, False, wq_t, sk, u_tab, vt_tab)
    return h.reshape(batch, seq, d)
```

```python
import functools
import math

import jax
import jax.numpy as jnp
from jax import lax
from jax.experimental import pallas as pl
from jax.experimental.pallas import tpu as pltpu

F32 = jnp.float32
BF16 = jnp.bfloat16

D_MODEL = 1024
GRID_W = 64
GDN_HEADS = 4
GDN_DK = 128
GDN_DV = 128
GDN_CHUNK = 64
GDN_TILE = 256
GDN_W = GDN_HEADS * GDN_DV
SCONV_W = 512
ATT_HEADS = 8
ATT_KV_HEADS = 2
ATT_GROUP = 4
ATT_HD = 64
ROPE_THETA = 10000.0
ROPE_PAIRS = 16
PEER_HEADS = 8
PEER_NKEYS = 128
PEER_TOPK = 16
PEER_TB = 256
PEER_TM = 512
PEER_EC = 2048
PEER_SELECT_GROUP = 4
NORM_EPS = 1e-6

W_QKV = 3 * GDN_W
W_Z = GDN_W
W_SC = 3 * SCONV_W
W_Q = ATT_HEADS * ATT_HD
W_KV = 2 * ATT_KV_HEADS * ATT_HD
W_G = 3 * D_MODEL
W_AB = 128
INPROJ_WIDTHS = (W_QKV, W_Z, W_SC, W_Q, W_KV, W_G, W_AB)

VMEM_LIMIT = 56 * 1024 * 1024


def _cparams(sem):
    return pltpu.CompilerParams(dimension_semantics=sem, vmem_limit_bytes=VMEM_LIMIT)


def _split_bf16(x):
    hi = x.astype(BF16)
    lo = (x - hi.astype(F32)).astype(BF16)
    return hi, lo


def _dot(a, b):
    return jnp.dot(a, b, preferred_element_type=F32)


def _dot_nt(a, b):
    return lax.dot_general(a, b, (((1,), (1,)), ((), ())), preferred_element_type=F32)


def _dot_tn(a, b):
    return lax.dot_general(a, b, (((0,), (0,)), ((), ())), preferred_element_type=F32)


def _dot3(a, b):
    ah, al = _split_bf16(a)
    bh, bl = _split_bf16(b)
    return _dot(ah, bh) + _dot(ah, bl) + _dot(al, bh)


def _rms_mod(x, nw, shift, scale):
    ms = jnp.mean(x * x, axis=-1, keepdims=True)
    return (x * lax.rsqrt(ms + NORM_EPS) * nw) * (1.0 + scale) + shift


def _mod_body(c_ref, w_ref, b_ref, o_ref):
    c = c_ref[...]
    a = c * jax.nn.sigmoid(c)
    o_ref[...] = _dot3(a, w_ref[...]) + b_ref[...]


def _modulation(c_all, w_mod, b_mod):
    rows, d = c_all.shape
    n = w_mod.shape[1]
    tn = 1536
    return pl.pallas_call(
        _mod_body,
        grid=(n // tn,),
        in_specs=[pl.BlockSpec((rows, d), lambda j: (0, 0)),
                  pl.BlockSpec((d, tn), lambda j: (0, j)),
                  pl.BlockSpec((1, tn), lambda j: (0, j))],
        out_specs=pl.BlockSpec((rows, tn), lambda j: (0, j)),
        out_shape=jax.ShapeDtypeStruct((rows, n), F32),
        compiler_params=_cparams(("parallel",)),
        name="modulation",
    )(c_all, w_mod, b_mod.reshape(1, n))


def _inproj_body(*refs, rope):
    h_ref, mod_ref, nw_ref, w_ref = refs[:4]
    rest = refs[4:]
    cos = sin = None
    if rope:
        cos, sin = rest[0][...], rest[1][...]
        rest = rest[2:]
    qw_ref, kw_ref, bd_ref, qkv_ref, z_ref, sc_ref, g_ref, ab_ref, qh_ref, kh_ref, vh_ref = rest
    u = _rms_mod(h_ref[...], nw_ref[...], mod_ref[0, 0:1, :], mod_ref[0, 1:2, :])
    ub = u.astype(BF16)
    offs = [sum(INPROJ_WIDTHS[:i]) for i in range(len(INPROJ_WIDTHS))]
    proj = lambda i: _dot(ub, w_ref[:, offs[i]:offs[i] + INPROJ_WIDTHS[i]])
    qkv_ref[...] = proj(0)
    z_ref[...] = proj(1)
    sc_ref[...] = proj(2)
    g_ref[...] = proj(5)
    ab_ref[...] = proj(6)
    _attention_operands(proj(3), proj(4), cos, sin, qw_ref[...], kw_ref[...], bd_ref[...], qh_ref, kh_ref, vh_ref)


def _inproj(h, mod, mod_idx, nw, w, qw_row, kw_row, bd, cos=None, sin=None, seq_len=None):
    nt, d = h.shape
    tm = 256
    wtot = w.shape[1]
    rope = cos is not None
    row = lambda i: (i, 0)
    const = lambda i: (0, 0)
    in_specs = [pl.BlockSpec((tm, d), row),
                pl.BlockSpec((1, 6, d), lambda i: (mod_idx(i, tm), 0, 0)),
                pl.BlockSpec((1, d), const),
                pl.BlockSpec((d, wtot), const, pipeline_mode=pl.Buffered(1))]
    args = [h, mod, nw.reshape(1, d), w]
    if rope:
        tps = seq_len // tm
        in_specs += [pl.BlockSpec((tm, W_Q), lambda i: (i % tps, 0))] * 2
        args += [cos, sin]
    in_specs += [pl.BlockSpec((1, W_Q), const),
                 pl.BlockSpec((1, ATT_KV_HEADS * ATT_HD), const),
                 pl.BlockSpec((W_Q, W_Q), const)]
    args += [qw_row, kw_row, bd]
    plain = (W_QKV, W_Z, W_SC, W_G, W_AB)
    heads = (ATT_HEADS, ATT_KV_HEADS, ATT_KV_HEADS)
    return pl.pallas_call(
        functools.partial(_inproj_body, rope=rope),
        grid=(nt // tm,),
        in_specs=in_specs,
        out_specs=[pl.BlockSpec((tm, wd), row) for wd in plain]
        + [pl.BlockSpec((nh, tm, ATT_HD), lambda i: (0, i, 0)) for nh in heads],
        out_shape=[jax.ShapeDtypeStruct((nt, wd), F32) for wd in plain]
        + [jax.ShapeDtypeStruct((nh, nt, ATT_HD), BF16) for nh in heads],
        compiler_params=_cparams(("parallel",)),
        name="inproj",
    )(*args)


def _bdot(a, b):
    return lax.dot_general(a, b, (((2,), (1,)), ((0,), (0,))), preferred_element_type=F32)


def _tri_inverse_minus_eye(n):
    nb = n.astype(BF16)
    p = _bdot(nb, nb)
    y = -n
    for step in range(5):
        y = y + p + _bdot(y.astype(BF16), p.astype(BF16))
        if step < 4:
            pb = p.astype(BF16)
            p = _bdot(pb, pb)
    return y


def _gdn_prep_body(x_ref, xp_ref, xn_ref, ab_ref, cw_ref, alog_ref, dtb_ref, *outs, tps):
    i = pl.program_id(0)
    first = (i % tps) == 0
    last = (i % tps) == tps - 1
    x = x_ref[...]
    nrow = x.shape[0]
    c = GDN_CHUNK
    cpt = nrow // c
    rows = lax.broadcasted_iota(jnp.int32, x.shape, 0)
    prev_row = jnp.where(first, 0.0, xp_ref[7:8, :])
    next_row = jnp.where(last, 0.0, xn_ref[0:1, :])
    xprev = jnp.where(rows == 0, prev_row, pltpu.roll(x, 1, 0))
    xnext = jnp.where(rows == nrow - 1, next_row, pltpu.roll(x, nrow - 1, 0))
    y = cw_ref[0:1, :] * xprev + cw_ref[1:2, :] * x + cw_ref[2:3, :] * xnext
    y = y * jax.nn.sigmoid(y)

    ab = ab_ref[...]
    zz = ab + dtb_ref[...]
    sp = jnp.maximum(zz, 0.0) + jnp.log1p(jnp.exp(-jnp.abs(zz)))
    g = -jnp.exp(alog_ref[...]) * sp
    beta = jax.nn.sigmoid(ab)

    ri = lax.broadcasted_iota(jnp.int32, (nrow, nrow), 0)
    ci = lax.broadcasted_iota(jnp.int32, (nrow, nrow), 1)
    same = (ri // c) == (ci // c)
    lower = (same & (ri >= ci)).astype(BF16)
    ones_blk = same.astype(BF16)
    g1 = g.astype(BF16)
    r1 = g - g1.astype(F32)
    g2 = r1.astype(BF16)
    g3 = (r1 - g2.astype(F32)).astype(BF16)
    gc_f = _dot(lower, g1) + _dot(lower, g2) + _dot(lower, g3)
    tot = _dot(ones_blk, g1) + _dot(ones_blk, g2) + _dot(ones_blk, g3)
    gc_b = tot - gc_f + g
    gcf_t = gc_f.T
    gcb_t = gc_b.T
    egc_f = jnp.exp(gc_f)
    egc_b = jnp.exp(gc_b)
    kdec_f = jnp.exp(tot - gc_f)
    kdec_b = jnp.exp(tot - gc_b)
    etot = jnp.exp(tot)

    pi = lax.broadcasted_iota(jnp.int32, (2 * c, 2 * c), 0)
    pj = lax.broadcasted_iota(jnp.int32, (2 * c, 2 * c), 1)
    top = (pi < c) & (pj < c)
    bot = (pi >= c) & (pj >= c)
    incl = (top & (pi >= pj)) | (bot & (pi <= pj))
    strict = (top & (pi > pj)) | (bot & (pi < pj))

    for d in range(2):
        outs[6 * d + 5][...] = jnp.zeros(outs[6 * d + 5].shape, F32)

    n_list, rhs_list, qk_list = [], [], []
    for h in range(GDN_HEADS):
        q = y[:, h * GDN_DK:(h + 1) * GDN_DK]
        k = y[:, GDN_W + h * GDN_DK:GDN_W + (h + 1) * GDN_DK]
        v = y[:, 2 * GDN_W + h * GDN_DV:2 * GDN_W + (h + 1) * GDN_DV]
        q = q * lax.rsqrt(jnp.sum(q * q, axis=-1, keepdims=True) + NORM_EPS) * (GDN_DK ** -0.5)
        k = k * lax.rsqrt(jnp.sum(k * k, axis=-1, keepdims=True) + NORM_EPS)
        cf, cb = h, GDN_HEADS + h
        bf_, bb_ = 2 * GDN_HEADS + h, 3 * GDN_HEADS + h
        sl = slice(h * GDN_DV, (h + 1) * GDN_DV)
        outs[2][:, sl] = (k * kdec_f[:, cf:cf + 1]).astype(BF16)
        outs[8][:, sl] = (k * kdec_b[:, cb:cb + 1]).astype(BF16)
        outs[3][:, sl] = (q * egc_f[:, cf:cf + 1]).astype(BF16)
        outs[9][:, sl] = (q * egc_b[:, cb:cb + 1]).astype(BF16)
        for ch in range(cpt):
            r = slice(ch * c, (ch + 1) * c)
            k2 = jnp.concatenate([k[r], k[r]], axis=0)
            q2 = jnp.concatenate([q[r], q[r]], axis=0)
            v2 = jnp.concatenate([v[r], v[r]], axis=0)
            kb2 = k2.astype(BF16)
            kk2 = _dot_nt(kb2, kb2)
            qk2 = _dot_nt(q2.astype(BF16), kb2)
            gcol = jnp.concatenate([gc_f[r, cf:cf + 1], gc_b[r, cb:cb + 1]], axis=0)
            grow = jnp.concatenate([gcf_t[cf:cf + 1, r], gcb_t[cb:cb + 1, r]], axis=1)
            bcol = jnp.concatenate([beta[r, bf_:bf_ + 1], beta[r, bb_:bb_ + 1]], axis=0)
            ecol = jnp.concatenate([egc_f[r, cf:cf + 1], egc_b[r, cb:cb + 1]], axis=0)
            ldec = jnp.where(incl, jnp.exp(jnp.where(incl, gcol - grow, 0.0)), 0.0)
            n_list.append(jnp.where(strict, kk2 * bcol * ldec, 0.0))
            rhs_list.append(jnp.concatenate([v2 * bcol, k2 * (bcol * ecol)], axis=1))
            qk_list.append(qk2 * ldec)
            for d, col in ((0, cf), (1, cb)):
                outs[6 * d + 5][ch, h:h + 1, :] = jnp.broadcast_to(etot[ch * c:ch * c + 1, col:col + 1], (1, 128))

    yinv = _tri_inverse_minus_eye(jnp.stack(n_list, axis=0))
    rhs = jnp.stack(rhs_list, axis=0)
    sol = rhs + _bdot(yinv.astype(BF16), rhs.astype(BF16))
    for h in range(GDN_HEADS):
        sl = slice(h * GDN_DV, (h + 1) * GDN_DV)
        for ch in range(cpt):
            r = slice(ch * c, (ch + 1) * c)
            s = sol[h * cpt + ch]
            qk = qk_list[h * cpt + ch]
            outs[0][r, sl] = s[:c, :GDN_DV]
            outs[1][r, sl] = s[:c, GDN_DV:].astype(BF16)
            outs[6][r, sl] = s[c:, :GDN_DV]
            outs[7][r, sl] = s[c:, GDN_DV:].astype(BF16)
            outs[4][r, h * c:(h + 1) * c] = qk[:c, :c].astype(BF16)
            outs[10][r, h * c:(h + 1) * c] = qk[c:, c:].astype(BF16)


def _gdn_prep(qkv, ab, conv_w, alog_row, dtb_row, seq_len):
    nt = qkv.shape[0]
    c = GDN_CHUNK
    tr = GDN_TILE
    cpt = tr // c
    tps = seq_len // tr
    nblk8 = nt // 8
    out_shape, out_specs = [], []
    for _ in range(2):
        for wd, dt in ((GDN_W, F32), (GDN_W, BF16), (GDN_W, BF16), (GDN_W, BF16), (GDN_HEADS * c, BF16)):
            out_shape.append(jax.ShapeDtypeStruct((nt, wd), dt))
            out_specs.append(pl.BlockSpec((tr, wd), lambda i: (i, 0)))
        out_shape.append(jax.ShapeDtypeStruct((nt // c, 8, 128), F32))
        out_specs.append(pl.BlockSpec((cpt, 8, 128), lambda i: (i, 0, 0)))
    return pl.pallas_call(
        functools.partial(_gdn_prep_body, tps=tps),
        grid=(nt // tr,),
        in_specs=[pl.BlockSpec((tr, W_QKV), lambda i: (i, 0)),
                  pl.BlockSpec((8, W_QKV), lambda i: (jnp.maximum(i * (tr // 8) - 1, 0), 0)),
                  pl.BlockSpec((8, W_QKV), lambda i: (jnp.minimum((i + 1) * (tr // 8), nblk8 - 1), 0)),
                  pl.BlockSpec((tr, W_AB), lambda i: (i, 0)),
                  pl.BlockSpec((3, W_QKV), lambda i: (0, 0)),
                  pl.BlockSpec((1, 128), lambda i: (0, 0)),
                  pl.BlockSpec((1, 128), lambda i: (0, 0))],
        out_specs=out_specs,
        out_shape=out_shape,
        compiler_params=_cparams(("parallel",)),
        name="gdn_prep",
    )(qkv, qkv, qkv, ab, conv_w, alog_row, dtb_row)


def _gdn_scan_body(*refs, nch):
    ins = refs[:12]
    s0_ref = refs[12]
    o_refs = refs[13:15]
    sfin_ref = refs[15]
    s_scr = refs[16]
    n = pl.program_id(1)
    c = GDN_CHUNK

    @pl.when(n == 0)
    def _():
        s_scr[...] = s0_ref[0]

    chains = [(d, h) for d in range(2) for h in range(GDN_HEADS)]
    state, prod, vnew = {}, {}, {}
    for d, h in chains:
        w_ref, qd_ref = ins[6 * d + 1], ins[6 * d + 3]
        sl = slice(h * GDN_DV, (h + 1) * GDN_DV)
        state[d, h] = s_scr[d, h]
        wq = jnp.concatenate([w_ref[:, sl], qd_ref[:, sl]], axis=0)
        prod[d, h] = _dot(wq, state[d, h].astype(BF16))
    for d, h in chains:
        u_ref, qk_ref = ins[6 * d], ins[6 * d + 4]
        sl = slice(h * GDN_DV, (h + 1) * GDN_DV)
        r = prod[d, h]
        vnew[d, h] = (u_ref[:, sl] - r[:c]).astype(BF16)
        o_refs[d][:, sl] = r[c:] + _dot(qk_ref[:, h * c:(h + 1) * c], vnew[d, h])
    for d, h in chains:
        kd_ref, eg_ref = ins[6 * d + 2], ins[6 * d + 5]
        sl = slice(h * GDN_DV, (h + 1) * GDN_DV)
        s_scr[d, h] = state[d, h] * eg_ref[0, h:h + 1, :] + _dot_tn(kd_ref[:, sl], vnew[d, h])

    @pl.when(n == nch - 1)
    def _():
        sfin_ref[0] = s_scr[...]


def _gdn_scan(prep, s0, batch, seq_len):
    c = GDN_CHUNK
    nch = seq_len // c
    nt = batch * seq_len
    fwd = lambda b, n: (b * nch + n, 0)
    bwd = lambda b, n: (b * nch + (nch - 1 - n), 0)
    in_specs = []
    for d, im in enumerate((fwd, bwd)):
        for wd in (GDN_W, GDN_W, GDN_W, GDN_W, GDN_HEADS * c):
            in_specs.append(pl.BlockSpec((c, wd), im))
        in_specs.append(pl.BlockSpec((1, 8, 128), (lambda im_: (lambda b, n: im_(b, n) + (0,)))(im)))
    st_spec = pl.BlockSpec((1, 2, GDN_HEADS, GDN_DK, GDN_DV), lambda b, n: (b, 0, 0, 0, 0))
    in_specs.append(st_spec)
    o_f, o_b, s_fin = pl.pallas_call(
        functools.partial(_gdn_scan_body, nch=nch),
        grid=(batch, nch),
        in_specs=in_specs,
        out_specs=[pl.BlockSpec((c, GDN_W), fwd), pl.BlockSpec((c, GDN_W), bwd), st_spec],
        out_shape=[jax.ShapeDtypeStruct((nt, GDN_W), F32), jax.ShapeDtypeStruct((nt, GDN_W), F32),
                   jax.ShapeDtypeStruct(s0.shape, F32)],
        scratch_shapes=[pltpu.VMEM((2, GDN_HEADS, GDN_DK, GDN_DV), F32)],
        compiler_params=_cparams(("arbitrary", "arbitrary")),
        name="gdn_scan",
    )(*prep, s0)
    return o_f, o_b, s_fin


def _group_norm_rope(x, w, bd, cos, sin):
    xx = x * x
    hi, lo = _split_bf16(xx)
    ss = _dot(hi, bd) + _dot(lo, bd)
    xn = x * lax.rsqrt(ss * (1.0 / ATT_HD) + NORM_EPS) * w
    if cos is not None:
        width = x.shape[1]
        lane = lax.broadcasted_iota(jnp.int32, x.shape, 1)
        from_below = pltpu.roll(xn, ROPE_PAIRS, 1)
        from_above = pltpu.roll(xn, width - ROPE_PAIRS, 1)
        partner = jnp.where((lane % (2 * ROPE_PAIRS)) < ROPE_PAIRS, from_above, from_below)
        xn = xn * cos + partner * sin
    return xn


def _attention_operands(q, kv, cos, sin, qw, kw, bd, qh_ref, kh_ref, vh_ref):
    nk = ATT_KV_HEADS * ATT_HD
    kcos = ksin = None
    if cos is not None:
        kcos, ksin = cos[:, :nk], sin[:, :nk]
    q = _group_norm_rope(q, qw, bd, cos, sin) * (ATT_HD ** -0.5 * math.log2(math.e))
    k = _group_norm_rope(kv[:, :nk], kw, bd[:nk, :nk], kcos, ksin)
    v = kv[:, nk:]
    for j in range(ATT_HEADS):
        qh_ref[j] = q[:, j * ATT_HD:(j + 1) * ATT_HD].astype(BF16)
    for j in range(ATT_KV_HEADS):
        kh_ref[j] = k[:, j * ATT_HD:(j + 1) * ATT_HD].astype(BF16)
        vh_ref[j] = v[:, j * ATT_HD:(j + 1) * ATT_HD].astype(BF16)


def _attention_body(q_ref, *refs, src_lens, tk, tq):
    n_src = len(src_lens)
    o_ref = refs[2 * n_src]
    rows = ATT_GROUP * tq
    q = q_ref[...].reshape(rows, ATT_HD)

    def block(sc, v, carry):
        m, l, acc = carry
        m_new = jnp.maximum(m, jnp.max(sc, axis=1, keepdims=True))
        alpha = jnp.exp2(m - m_new)
        p = jnp.exp2(sc - m_new)
        l = alpha * l + jnp.sum(p, axis=1, keepdims=True)
        acc = alpha * acc + _dot(p.astype(BF16), v)
        return m_new, l, acc

    carry = (jnp.full((rows, 1), -jnp.inf, F32), jnp.zeros((rows, 1), F32), jnp.zeros((rows, ATT_HD), F32))
    for s in range(n_src):
        k_ref, v_ref = refs[2 * s], refs[2 * s + 1]
        blk = min(tk, src_lens[s])
        nblk = src_lens[s] // blk
        for j in range(nblk):
            sc = _dot_nt(q, k_ref[0, j * blk:(j + 1) * blk, :])
            carry = block(sc, v_ref[0, j * blk:(j + 1) * blk, :], carry)
    m, l, acc = carry
    o = (acc / l).reshape(ATT_GROUP, tq, ATT_HD)
    o_ref[...] = jnp.concatenate([o[j] for j in range(ATT_GROUP)], axis=1)


def _attention(qh, sources, q_len):
    nq = qh.shape[1]
    tq = 128
    tk = 8192
    tpb = q_len // tq
    in_specs = [pl.BlockSpec((ATT_GROUP, tq, ATT_HD), lambda i, g: (g, i, 0))]
    args = [qh]
    for kh, vh, sl in sources:
        spec = pl.BlockSpec((1, sl, ATT_HD), lambda i, g: (g, i // tpb, 0))
        in_specs += [spec, spec]
        args += [kh, vh]
    return pl.pallas_call(
        functools.partial(_attention_body, src_lens=tuple(s[2] for s in sources), tk=tk, tq=tq),
        grid=(nq // tq, ATT_KV_HEADS),
        in_specs=in_specs,
        out_specs=pl.BlockSpec((tq, ATT_GROUP * ATT_HD), lambda i, g: (i, g)),
        out_shape=jax.ShapeDtypeStruct((nq, W_Q), F32),
        compiler_params=_cparams(("parallel", "parallel")),
        name="attention",
    )(*args)


def _merge_body(of_ref, ob_ref, z_ref, p_ref, pp_ref, pn_ref, oc_ref, gt_ref, h_ref, mod_ref,
                gnw_ref, cw_ref, wb_ref, wo_ref, o_ref, *, tps):
    i = pl.program_id(0)
    first = (i % tps) == 0
    last = (i % tps) == tps - 1
    o = of_ref[...] + ob_ref[...]
    z = z_ref[...]
    parts = []
    for h in range(GDN_HEADS):
        sl = slice(h * GDN_DV, (h + 1) * GDN_DV)
        oh = o[:, sl]
        oh = oh * lax.rsqrt(jnp.mean(oh * oh, axis=-1, keepdims=True) + NORM_EPS) * gnw_ref[...]
        zh = z[:, sl]
        parts.append(oh * (zh * jax.nn.sigmoid(zh)))
    br_a = jnp.concatenate(parts, axis=1)
    p = p_ref[...]
    tm = p.shape[0]
    bg = p[:, :SCONV_W]
    cx = p[:, SCONV_W:2 * SCONV_W] * p[:, 2 * SCONV_W:]
    pp = pp_ref[7:8, :]
    pn = pn_ref[0:1, :]
    prev_row = jnp.where(first, 0.0, pp[:, SCONV_W:2 * SCONV_W] * pp[:, 2 * SCONV_W:])
    next_row = jnp.where(last, 0.0, pn[:, SCONV_W:2 * SCONV_W] * pn[:, 2 * SCONV_W:])
    rows = lax.broadcasted_iota(jnp.int32, cx.shape, 0)
    cprev = jnp.where(rows == 0, prev_row, pltpu.roll(cx, 1, 0))
    cnext = jnp.where(rows == tm - 1, next_row, pltpu.roll(cx, tm - 1, 0))
    br_b = bg * (cw_ref[0:1, :] * cprev + cw_ref[1:2, :] * cx + cw_ref[2:3, :] * cnext)
    br_c = oc_ref[...]
    mixed = None
    for n, br in enumerate((br_a, br_b, br_c)):
        t = _dot(br.astype(BF16), wb_ref[n]) * jax.nn.sigmoid(gt_ref[:, n * D_MODEL:(n + 1) * D_MODEL])
        mixed = t if mixed is None else mixed + t
    y = _dot(mixed.astype(BF16), wo_ref[...])
    o_ref[...] = h_ref[...] + mod_ref[0, 2:3, :] * y


def _merge(o_f, o_b, z, psc, oc, gates, h, mod, mod_idx, gnw_row, conv_b_w, wb, wo, seq_len):
    nt, d = h.shape
    tm = 256
    tps = seq_len // tm
    nblk8 = nt // 8
    row = lambda i: (i, 0)
    return pl.pallas_call(
        functools.partial(_merge_body, tps=tps),
        grid=(nt // tm,),
        in_specs=[pl.BlockSpec((tm, GDN_W), row),
                  pl.BlockSpec((tm, GDN_W), row),
                  pl.BlockSpec((tm, W_Z), row),
                  pl.BlockSpec((tm, W_SC), row),
                  pl.BlockSpec((8, W_SC), lambda i: (jnp.maximum(i * (tm // 8) - 1, 0), 0)),
                  pl.BlockSpec((8, W_SC), lambda i: (jnp.minimum((i + 1) * (tm // 8), nblk8 - 1), 0)),
                  pl.BlockSpec((tm, W_Q), row),
                  pl.BlockSpec((tm, W_G), row),
                  pl.BlockSpec((tm, d), row),
                  pl.BlockSpec((1, 6, d), lambda i: (mod_idx(i, tm), 0, 0)),
                  pl.BlockSpec((1, GDN_DV), lambda i: (0, 0)),
                  pl.BlockSpec((3, SCONV_W), lambda i: (0, 0)),
                  pl.BlockSpec((3, SCONV_W, d), lambda i: (0, 0, 0)),
                  pl.BlockSpec((d, d), lambda i: (0, 0))],
        out_specs=pl.BlockSpec((tm, d), row),
        out_shape=jax.ShapeDtypeStruct((nt, d), F32),
        compiler_params=_cparams(("parallel",)),
        name="merge",
    )(o_f, o_b, z, psc, psc, psc, oc, gates, h, mod, gnw_row, conv_b_w, wb, wo)


PEER_NO_RANK = 64.0


def _top_rows(s, count):
    rows = []
    rank = jnp.full(s.shape, PEER_NO_RANK, F32)
    for r in range(count):
        m = jnp.max(s, axis=0, keepdims=True)
        rows.append(m)
        hit = s == m
        rank = jnp.where(hit, float(r + 1), rank)
        s = jnp.where(hit, -jnp.inf, s)
    return rows, rank


def _peer_body(h_ref, mod_ref, nw_ref, fw_ref, wq_ref, sk_ref, u_ref, vt_ref, o_ref,
               xt_scr, out_scr, sc_scr, cnt_scr, ea_scr, rank_scr, eb_scr, cand_scr, gw_scr,
               *, tm, ec, nchunks, final_norm):
    c = pl.program_id(1)
    nk = PEER_NKEYS
    half = nk // 2
    tb_w = PEER_TB
    ntb = tm // tb_w
    nlb = tm // 128
    lb_per_tb = tb_w // 128

    @pl.when(c == 0)
    def _():
        u = _rms_mod(h_ref[...], nw_ref[...], mod_ref[0, 3:4, :], mod_ref[0, 4:5, :])
        ut = u.T
        uh = ut.astype(BF16)
        qt = _dot(wq_ref[...], uh)
        for tb in range(ntb):
            xt_scr[tb] = uh[:, tb * tb_w:(tb + 1) * tb_w]
            out_scr[tb] = jnp.zeros(out_scr.shape[1:], F32)
        for h in range(PEER_HEADS):
            for p in range(2):
                qs = qt[h * nk + p * half:h * nk + (p + 1) * half, :]
                s = _dot(sk_ref[h, p], qs.astype(BF16))
                for lb in range(nlb):
                    sc_scr[p, h * nlb + lb] = s[:, lb * 128:(lb + 1) * 128]

        def select_one(hl):
            s0 = sc_scr[0, hl]
            s1 = sc_scr[1, hl]
            top0, _ = _top_rows(s0, PEER_TOPK)
            top1, rank1 = _top_rows(s1, PEER_TOPK)
            pairs = [(k, l) for k in range(PEER_TOPK) for l in range(PEER_TOPK) if (k + 1) * (l + 1) <= PEER_TOPK]
            cand = cand_scr.at[hl % PEER_SELECT_GROUP]
            cand[...] = jnp.full(cand.shape, -jnp.inf, F32)
            sums = {}
            for r, (k, l) in enumerate(pairs):
                sums[k, l] = top0[k] + top1[l]
                cand[r:r + 1, :] = sums[k, l]
            best, _ = _top_rows(cand[...], PEER_TOPK)
            tau = best[PEER_TOPK - 1]
            zsum = jnp.zeros_like(tau)
            for b_ in best:
                zsum = zsum + jnp.exp(b_ - best[0])
            cnt = jnp.zeros(s0.shape, F32)
            for k in range(PEER_TOPK):
                n_sel = jnp.zeros_like(tau)
                for l in range(PEER_TOPK // (k + 1)):
                    n_sel = n_sel + jnp.where(sums[k, l] >= tau, 1.0, 0.0)
                cnt = jnp.where(s0 == top0[k], n_sel, cnt)
            cnt_scr[hl] = cnt.reshape(nk // 8, 8, 128)
            ea_scr[hl] = (jnp.exp(s0 - top0[0]) / zsum).reshape(nk // 8, 8, 128)
            rank_scr[hl] = rank1.astype(BF16)
            eb_scr[hl] = jnp.exp(s1 - top1[0]).astype(BF16)

        def select(grp, carry):
            for k in range(PEER_SELECT_GROUP):
                select_one(PEER_SELECT_GROUP * grp + k)
            return carry

        lax.fori_loop(0, PEER_HEADS * nlb // PEER_SELECT_GROUP, select, 0)

    nblk = ec // nk
    piece = 16
    npiece = nk // piece

    def half_dots(lhs_ref, rhs):
        hm = lhs_ref.shape[0] // 2
        return [_dot(lhs_ref[0:hm, :], rhs), _dot(lhs_ref[hm:2 * hm, :], rhs)]

    def gated_activation(tb, acts):
        hm = ec // 2
        for sub in range(lb_per_tb):
            lb = tb * lb_per_tb + sub
            ls = slice(sub * 128, (sub + 1) * 128)
            for ii in range(nblk):
                wm = [None] * npiece
                for h in range(PEER_HEADS):
                    cnt_row = jnp.broadcast_to(cnt_scr[h * nlb + lb, c * (nblk // 8) + ii // 8, ii % 8:ii % 8 + 1, :], (piece, 128)).astype(BF16)
                    ea_row = jnp.broadcast_to(ea_scr[h * nlb + lb, c * (nblk // 8) + ii // 8, ii % 8:ii % 8 + 1, :], (piece, 128)).astype(BF16)
                    for jp in range(npiece):
                        js = slice(jp * piece, (jp + 1) * piece)
                        sel = rank_scr[h * nlb + lb, js, :] <= cnt_row
                        w = eb_scr[h * nlb + lb, js, :] * ea_row
                        w = jnp.where(sel, w, jnp.zeros_like(w))
                        wm[jp] = w if h == 0 else wm[jp] + w
                for jp in range(npiece):
                    r0 = ii * nk + jp * piece
                    ap = acts[r0 // hm][r0 % hm:r0 % hm + piece, ls]
                    gl = 0.5 * ap * (1.0 + lax.erf(ap * (0.5 ** 0.5)))
                    gw_scr[tb, r0:r0 + piece, ls] = gl.astype(BF16) * wm[jp]

    acts = [half_dots(u_ref, xt_scr[tb]) for tb in range(ntb)]
    for tb in range(ntb):
        gated_activation(tb, acts[tb])
        outs = half_dots(vt_ref.at[0], gw_scr[tb])
        hd = out_scr.shape[1] // 2
        out_scr[tb, 0:hd, :] += outs[0]
        out_scr[tb, hd:2 * hd, :] += outs[1]

    @pl.when(c == nchunks - 1)
    def _():
        out = jnp.concatenate([out_scr[tb] for tb in range(ntb)], axis=1)
        hn = h_ref[...] + mod_ref[0, 5:6, :] * out.T
        if final_norm:
            hn = hn * lax.rsqrt(jnp.mean(hn * hn, axis=-1, keepdims=True) + NORM_EPS) * fw_ref[...]
        o_ref[...] = hn


def _peer(h, mod, mod_idx, nw, final_w, apply_final, wq_t, sk, u_tab, vt_tab):
    nt, d = h.shape
    tm = PEER_TM
    ec = PEER_EC
    ne = u_tab.shape[0]
    nchunks = ne // ec
    nk = PEER_NKEYS
    ntb = tm // PEER_TB
    nlb = tm // 128
    const2 = lambda i, c: (0, 0)
    per_head = lambda dt: pltpu.VMEM((PEER_HEADS * nlb, nk, 128), dt)
    per_head_rows = pltpu.VMEM((PEER_HEADS * nlb, nk // 8, 8, 128), F32)
    return pl.pallas_call(
        functools.partial(_peer_body, tm=tm, ec=ec, nchunks=nchunks, final_norm=apply_final),
        grid=(nt // tm, nchunks),
        in_specs=[pl.BlockSpec((tm, d), lambda i, c: (i, 0)),
                  pl.BlockSpec((1, 6, d), lambda i, c: (mod_idx(i, tm), 0, 0)),
                  pl.BlockSpec((1, d), const2),
                  pl.BlockSpec((1, d), const2),
                  pl.BlockSpec((d, d), const2),
                  pl.BlockSpec((PEER_HEADS, 2, nk, nk // 2), lambda i, c: (0, 0, 0, 0)),
                  pl.BlockSpec((ec, d), lambda i, c: (c, 0)),
                  pl.BlockSpec((1, d, ec), lambda i, c: (c, 0, 0))],
        out_specs=pl.BlockSpec((tm, d), lambda i, c: (i, 0)),
        out_shape=jax.ShapeDtypeStruct((nt, d), F32),
        scratch_shapes=[pltpu.VMEM((ntb, d, PEER_TB), BF16),
                        pltpu.VMEM((ntb, d, PEER_TB), F32),
                        pltpu.VMEM((2, PEER_HEADS * nlb, nk, 128), F32),
                        per_head_rows,
                        per_head_rows,
                        per_head(BF16),
                        per_head(BF16),
                        pltpu.VMEM((PEER_SELECT_GROUP, 64, 128), F32),
                        pltpu.VMEM((ntb, ec, PEER_TB), BF16)],
        compiler_params=_cparams(("parallel", "arbitrary")),
        name="peer",
    )(h, mod, nw.reshape(1, d), final_w.reshape(1, d), wq_t, sk, u_tab, vt_tab)


def _rope_tables(seq):
    t = jnp.arange(seq)
    row = (t // GRID_W).astype(F32)
    col = (t % GRID_W).astype(F32)
    freqs = ROPE_THETA ** (-jnp.arange(ROPE_PAIRS, dtype=F32) / ROPE_PAIRS)
    ar = row[:, None] * freqs
    ac = col[:, None] * freqs
    cos = jnp.concatenate([jnp.cos(ar), jnp.cos(ar), jnp.cos(ac), jnp.cos(ac)], axis=1)
    sin = jnp.concatenate([-jnp.sin(ar), jnp.sin(ar), -jnp.sin(ac), jnp.sin(ac)], axis=1)
    return jnp.tile(cos, (1, ATT_HEADS)), jnp.tile(sin, (1, ATT_HEADS))


def _reorder_w_in(w):
    sizes = (W_QKV, W_Z, 4 * GDN_HEADS, W_SC, W_Q, W_KV, W_G)
    parts, start = [], 0
    for s in sizes:
        parts.append(w[:, start:start + s])
        start += s
    qkv, z, ab, sc, q, kv, g = parts
    ab = jnp.pad(ab, ((0, 0), (0, W_AB - ab.shape[1])))
    return jnp.concatenate([qkv, z, sc, q, kv, g, ab], axis=1).astype(BF16)


def _pad_row(v, n=128):
    v = v.reshape(1, -1)
    return jnp.pad(v, ((0, 0), (0, n - v.shape[1])))


def kernel(x, c, ctx, c_ctx, w_mod, b_mod, norm1_w, w_in, conv_a_w, a_log, dt_bias, gdn_norm_w, conv_b_w,
           q_norm_w, k_norm_w, w_branch, w_out, norm2_w, w_query, sub_keys, expert_u, expert_v, final_norm_w):
    batch, seq, d = x.shape
    ctx_len = ctx.shape[1]
    depth = w_mod.shape[0]
    h = x.reshape(batch * seq, d)
    hc = ctx.reshape(batch * ctx_len, d)

    mod_rows = -(-(batch + 1) // 8) * 8
    c_all = jnp.zeros((mod_rows, d), F32).at[:batch].set(c).at[batch].set(c_ctx)
    cos, sin = _rope_tables(seq)
    hd_i = jnp.arange(W_Q) // ATT_HD
    bd = (hd_i[:, None] == hd_i[None, :]).astype(BF16)
    lat_idx = lambda i, tm: i // (seq // tm)
    ctx_idx = lambda i, tm: batch

    for layer in range(depth):
        last = layer == depth - 1
        mod = _modulation(c_all, w_mod[layer], b_mod[layer]).reshape(mod_rows, 6, d)
        w_in_r = _reorder_w_in(w_in[layer])
        alog_row = _pad_row(a_log[layer])
        dtb_row = _pad_row(dt_bias[layer])
        qw_row = jnp.tile(q_norm_w[layer], ATT_HEADS).reshape(1, W_Q)
        kw_row = jnp.tile(k_norm_w[layer], ATT_KV_HEADS).reshape(1, ATT_KV_HEADS * ATT_HD)
        gnw_row = gdn_norm_w[layer].reshape(1, GDN_DV)
        wb = w_branch[layer].astype(BF16)
        wo = w_out[layer].astype(BF16)
        wq_t = w_query[layer].T.astype(BF16)
        sk = sub_keys[layer].astype(BF16)
        u_tab = expert_u[layer].astype(BF16)
        vt_tab = expert_v[layer].astype(BF16).reshape(-1, PEER_EC, d).transpose(0, 2, 1)

        qkv_l, z_l, sc_l, g_l, ab_l, qh_l, kh_l, vh_l = _inproj(h, mod, lat_idx, norm1_w[layer], w_in_r, qw_row, kw_row, bd,
                                                                cos, sin, seq)
        qkv_c, z_c, sc_c, g_c, ab_c, qh_c, kh_c, vh_c = _inproj(hc, mod, ctx_idx, norm1_w[layer], w_in_r, qw_row, kw_row, bd)

        prep_c = _gdn_prep(qkv_c, ab_c, conv_a_w[layer], alog_row, dtb_row, ctx_len)
        prep_l = _gdn_prep(qkv_l, ab_l, conv_a_w[layer], alog_row, dtb_row, seq)
        s_zero = jnp.zeros((batch, 2, GDN_HEADS, GDN_DK, GDN_DV), F32)
        ocf, ocb, s_ctx = _gdn_scan(prep_c, s_zero, batch, ctx_len)
        olf, olb, _ = _gdn_scan(prep_l, s_ctx, batch, seq)

        att_l = _attention(qh_l, [(kh_l, vh_l, seq), (kh_c, vh_c, ctx_len)], seq)

        h = _merge(olf, olb, z_l, sc_l, att_l, g_l, h, mod, lat_idx, gnw_row, conv_b_w[layer],
                   wb, wo, seq)
        h = _peer(h, mod, lat_idx, norm2_w[layer], final_norm_w, last, wq_t, sk, u_tab, vt_tab)
        if not last:
            att_c = _attention(qh_c, [(kh_c, vh_c, ctx_len)], ctx_len)
            hc = _merge(ocf, ocb, z_c, sc_c, att_c, g_c, hc, mod, ctx_idx, gnw_row, conv_b_w[layer],
                        wb, wo, ctx_len)
            hc = _peer(hc, mod, ctx_idx, norm2_w[layer], final_norm_w, False, wq_t, sk, u_tab, vt_tab)
    return h.reshape(batch, seq, d)
```

```python
import functools
import math

import jax
import jax.numpy as jnp
from jax import lax
from jax.experimental import pallas as pl
from jax.experimental.pallas import tpu as pltpu

F32 = jnp.float32
BF16 = jnp.bfloat16

D_MODEL = 1024
GRID_W = 64
GDN_HEADS = 4
GDN_DK = 128
GDN_DV = 128
GDN_CHUNK = 64
GDN_TILE = 256
GDN_W = GDN_HEADS * GDN_DV
SCONV_W = 512
ATT_HEADS = 8
ATT_KV_HEADS = 2
ATT_GROUP = 4
ATT_HD = 64
ROPE_THETA = 10000.0
ROPE_PAIRS = 16
PEER_HEADS = 8
PEER_NKEYS = 128
PEER_TOPK = 16
PEER_TB = 256
PEER_TM = 512
PEER_EC = 2048
PEER_SELECT_GROUP = 4
NORM_EPS = 1e-6

W_QKV = 3 * GDN_W
W_Z = GDN_W
W_SC = 3 * SCONV_W
W_Q = ATT_HEADS * ATT_HD
W_KV = 2 * ATT_KV_HEADS * ATT_HD
W_G = 3 * D_MODEL
W_AB = 128
INPROJ_WIDTHS = (W_QKV, W_Z, W_SC, W_Q, W_KV, W_G, W_AB)

VMEM_LIMIT = 56 * 1024 * 1024


def _cparams(sem):
    return pltpu.CompilerParams(dimension_semantics=sem, vmem_limit_bytes=VMEM_LIMIT)


def _split_bf16(x):
    hi = x.astype(BF16)
    lo = (x - hi.astype(F32)).astype(BF16)
    return hi, lo


def _dot(a, b):
    return jnp.dot(a, b, preferred_element_type=F32)


def _dot_nt(a, b):
    return lax.dot_general(a, b, (((1,), (1,)), ((), ())), preferred_element_type=F32)


def _dot_tn(a, b):
    return lax.dot_general(a, b, (((0,), (0,)), ((), ())), preferred_element_type=F32)


def _dot3(a, b):
    ah, al = _split_bf16(a)
    bh, bl = _split_bf16(b)
    return _dot(ah, bh) + _dot(ah, bl) + _dot(al, bh)


def _rms_mod(x, nw, shift, scale):
    ms = jnp.mean(x * x, axis=-1, keepdims=True)
    return (x * lax.rsqrt(ms + NORM_EPS) * nw) * (1.0 + scale) + shift


def _mod_body(c_ref, w_ref, b_ref, o_ref):
    c = c_ref[...]
    a = c * jax.nn.sigmoid(c)
    o_ref[...] = _dot3(a, w_ref[...]) + b_ref[...]


def _modulation(c_all, w_mod, b_mod):
    rows, d = c_all.shape
    n = w_mod.shape[1]
    tn = 1536
    return pl.pallas_call(
        _mod_body,
        grid=(n // tn,),
        in_specs=[pl.BlockSpec((rows, d), lambda j: (0, 0)),
                  pl.BlockSpec((d, tn), lambda j: (0, j)),
                  pl.BlockSpec((1, tn), lambda j: (0, j))],
        out_specs=pl.BlockSpec((rows, tn), lambda j: (0, j)),
        out_shape=jax.ShapeDtypeStruct((rows, n), F32),
        compiler_params=_cparams(("parallel",)),
        name="modulation",
    )(c_all, w_mod, b_mod.reshape(1, n))


def _inproj_body(*refs, rope):
    h_ref, mod_ref, nw_ref, w_ref = refs[:4]
    rest = refs[4:]
    cos = sin = None
    if rope:
        cos, sin = rest[0][...], rest[1][...]
        rest = rest[2:]
    qw_ref, kw_ref, bd_ref, qkv_ref, z_ref, sc_ref, g_ref, ab_ref, qh_ref, kh_ref, vh_ref = rest
    u = _rms_mod(h_ref[...], nw_ref[...], mod_ref[0, 0:1, :], mod_ref[0, 1:2, :])
    ub = u.astype(BF16)
    offs = [sum(INPROJ_WIDTHS[:i]) for i in range(len(INPROJ_WIDTHS))]
    proj = lambda i: _dot(ub, w_ref[:, offs[i]:offs[i] + INPROJ_WIDTHS[i]])
    _attention_operands(proj(3), proj(4), cos, sin, qw_ref[...], kw_ref[...], bd_ref[...], qh_ref, kh_ref, vh_ref)
    qkv_ref[...] = proj(0)
    z_ref[...] = proj(1)
    sc_ref[...] = proj(2)
    g_ref[...] = proj(5)
    ab_ref[...] = proj(6)


def _inproj(h, mod, mod_idx, nw, w, qw_row, kw_row, bd, cos=None, sin=None, seq_len=None):
    nt, d = h.shape
    tm = 256
    wtot = w.shape[1]
    rope = cos is not None
    row = lambda i: (i, 0)
    const = lambda i: (0, 0)
    in_specs = [pl.BlockSpec((tm, d), row),
                pl.BlockSpec((1, 6, d), lambda i: (mod_idx(i, tm), 0, 0)),
                pl.BlockSpec((1, d), const),
                pl.BlockSpec((d, wtot), const, pipeline_mode=pl.Buffered(1))]
    args = [h, mod, nw.reshape(1, d), w]
    if rope:
        tps = seq_len // tm
        in_specs += [pl.BlockSpec((tm, W_Q), lambda i: (i % tps, 0))] * 2
        args += [cos, sin]
    in_specs += [pl.BlockSpec((1, W_Q), const),
                 pl.BlockSpec((1, ATT_KV_HEADS * ATT_HD), const),
                 pl.BlockSpec((W_Q, W_Q), const)]
    args += [qw_row, kw_row, bd]
    plain = (W_QKV, W_Z, W_SC, W_G, W_AB)
    heads = (ATT_HEADS, ATT_KV_HEADS, ATT_KV_HEADS)
    return pl.pallas_call(
        functools.partial(_inproj_body, rope=rope),
        grid=(nt // tm,),
        in_specs=in_specs,
        out_specs=[pl.BlockSpec((tm, wd), row) for wd in plain]
        + [pl.BlockSpec((nh, tm, ATT_HD), lambda i: (0, i, 0)) for nh in heads],
        out_shape=[jax.ShapeDtypeStruct((nt, wd), F32) for wd in plain]
        + [jax.ShapeDtypeStruct((nh, nt, ATT_HD), BF16) for nh in heads],
        compiler_params=_cparams(("parallel",)),
        name="inproj",
    )(*args)


def _bdot(a, b):
    return lax.dot_general(a, b, (((2,), (1,)), ((0,), (0,))), preferred_element_type=F32)


def _tri_inverse_minus_eye(n):
    nb = n.astype(BF16)
    p = _bdot(nb, nb)
    y = -n
    for step in range(5):
        y = y + p + _bdot(y.astype(BF16), p.astype(BF16))
        if step < 4:
            pb = p.astype(BF16)
            p = _bdot(pb, pb)
    return y


def _gdn_prep_body(x_ref, xp_ref, xn_ref, ab_ref, cw_ref, alog_ref, dtb_ref, *outs, tps):
    i = pl.program_id(0)
    first = (i % tps) == 0
    last = (i % tps) == tps - 1
    x = x_ref[...]
    nrow = x.shape[0]
    c = GDN_CHUNK
    cpt = nrow // c
    rows = lax.broadcasted_iota(jnp.int32, x.shape, 0)
    prev_row = jnp.where(first, 0.0, xp_ref[7:8, :])
    next_row = jnp.where(last, 0.0, xn_ref[0:1, :])
    xprev = jnp.where(rows == 0, prev_row, pltpu.roll(x, 1, 0))
    xnext = jnp.where(rows == nrow - 1, next_row, pltpu.roll(x, nrow - 1, 0))
    y = cw_ref[0:1, :] * xprev + cw_ref[1:2, :] * x + cw_ref[2:3, :] * xnext
    y = y * jax.nn.sigmoid(y)

    ab = ab_ref[...]
    zz = ab + dtb_ref[...]
    sp = jnp.maximum(zz, 0.0) + jnp.log1p(jnp.exp(-jnp.abs(zz)))
    g = -jnp.exp(alog_ref[...]) * sp
    beta = jax.nn.sigmoid(ab)

    ri = lax.broadcasted_iota(jnp.int32, (nrow, nrow), 0)
    ci = lax.broadcasted_iota(jnp.int32, (nrow, nrow), 1)
    same = (ri // c) == (ci // c)
    lower = (same & (ri >= ci)).astype(BF16)
    ones_blk = same.astype(BF16)
    g1 = g.astype(BF16)
    r1 = g - g1.astype(F32)
    g2 = r1.astype(BF16)
    g3 = (r1 - g2.astype(F32)).astype(BF16)
    gc_f = _dot(lower, g1) + _dot(lower, g2) + _dot(lower, g3)
    tot = _dot(ones_blk, g1) + _dot(ones_blk, g2) + _dot(ones_blk, g3)
    gc_b = tot - gc_f + g
    gcf_t = gc_f.T
    gcb_t = gc_b.T
    egc_f = jnp.exp(gc_f)
    egc_b = jnp.exp(gc_b)
    kdec_f = jnp.exp(tot - gc_f)
    kdec_b = jnp.exp(tot - gc_b)
    etot = jnp.exp(tot)

    pi = lax.broadcasted_iota(jnp.int32, (2 * c, 2 * c), 0)
    pj = lax.broadcasted_iota(jnp.int32, (2 * c, 2 * c), 1)
    top = (pi < c) & (pj < c)
    bot = (pi >= c) & (pj >= c)
    incl = (top & (pi >= pj)) | (bot & (pi <= pj))
    strict = (top & (pi > pj)) | (bot & (pi < pj))

    for d in range(2):
        outs[6 * d + 5][...] = jnp.zeros(outs[6 * d + 5].shape, F32)

    n_list, rhs_list, qk_list = [], [], []
    for h in range(GDN_HEADS):
        q = y[:, h * GDN_DK:(h + 1) * GDN_DK]
        k = y[:, GDN_W + h * GDN_DK:GDN_W + (h + 1) * GDN_DK]
        v = y[:, 2 * GDN_W + h * GDN_DV:2 * GDN_W + (h + 1) * GDN_DV]
        q = q * lax.rsqrt(jnp.sum(q * q, axis=-1, keepdims=True) + NORM_EPS) * (GDN_DK ** -0.5)
        k = k * lax.rsqrt(jnp.sum(k * k, axis=-1, keepdims=True) + NORM_EPS)
        cf, cb = h, GDN_HEADS + h
        bf_, bb_ = 2 * GDN_HEADS + h, 3 * GDN_HEADS + h
        sl = slice(h * GDN_DV, (h + 1) * GDN_DV)
        outs[2][:, sl] = (k * kdec_f[:, cf:cf + 1]).astype(BF16)
        outs[8][:, sl] = (k * kdec_b[:, cb:cb + 1]).astype(BF16)
        outs[3][:, sl] = (q * egc_f[:, cf:cf + 1]).astype(BF16)
        outs[9][:, sl] = (q * egc_b[:, cb:cb + 1]).astype(BF16)
        for ch in range(cpt):
            r = slice(ch * c, (ch + 1) * c)
            k2 = jnp.concatenate([k[r], k[r]], axis=0)
            q2 = jnp.concatenate([q[r], q[r]], axis=0)
            v2 = jnp.concatenate([v[r], v[r]], axis=0)
            kb2 = k2.astype(BF16)
            kk2 = _dot_nt(kb2, kb2)
            qk2 = _dot_nt(q2.astype(BF16), kb2)
            gcol = jnp.concatenate([gc_f[r, cf:cf + 1], gc_b[r, cb:cb + 1]], axis=0)
            grow = jnp.concatenate([gcf_t[cf:cf + 1, r], gcb_t[cb:cb + 1, r]], axis=1)
            bcol = jnp.concatenate([beta[r, bf_:bf_ + 1], beta[r, bb_:bb_ + 1]], axis=0)
            ecol = jnp.concatenate([egc_f[r, cf:cf + 1], egc_b[r, cb:cb + 1]], axis=0)
            ldec = jnp.where(incl, jnp.exp(jnp.where(incl, gcol - grow, 0.0)), 0.0)
            n_list.append(jnp.where(strict, kk2 * bcol * ldec, 0.0))
            rhs_list.append(jnp.concatenate([v2 * bcol, k2 * (bcol * ecol)], axis=1))
            qk_list.append(qk2 * ldec)
            for d, col in ((0, cf), (1, cb)):
                outs[6 * d + 5][ch, h:h + 1, :] = jnp.broadcast_to(etot[ch * c:ch * c + 1, col:col + 1], (1, 128))

    yinv = _tri_inverse_minus_eye(jnp.stack(n_list, axis=0))
    rhs = jnp.stack(rhs_list, axis=0)
    sol = rhs + _bdot(yinv.astype(BF16), rhs.astype(BF16))
    for h in range(GDN_HEADS):
        sl = slice(h * GDN_DV, (h + 1) * GDN_DV)
        for ch in range(cpt):
            r = slice(ch * c, (ch + 1) * c)
            s = sol[h * cpt + ch]
            qk = qk_list[h * cpt + ch]
            outs[0][r, sl] = s[:c, :GDN_DV]
            outs[1][r, sl] = s[:c, GDN_DV:].astype(BF16)
            outs[6][r, sl] = s[c:, :GDN_DV]
            outs[7][r, sl] = s[c:, GDN_DV:].astype(BF16)
            outs[4][r, h * c:(h + 1) * c] = qk[:c, :c].astype(BF16)
            outs[10][r, h * c:(h + 1) * c] = qk[c:, c:].astype(BF16)


def _gdn_prep(qkv, ab, conv_w, alog_row, dtb_row, seq_len):
    nt = qkv.shape[0]
    c = GDN_CHUNK
    tr = GDN_TILE
    cpt = tr // c
    tps = seq_len // tr
    nblk8 = nt // 8
    out_shape, out_specs = [], []
    for _ in range(2):
        for wd, dt in ((GDN_W, F32), (GDN_W, BF16), (GDN_W, BF16), (GDN_W, BF16), (GDN_HEADS * c, BF16)):
            out_shape.append(jax.ShapeDtypeStruct((nt, wd), dt))
            out_specs.append(pl.BlockSpec((tr, wd), lambda i: (i, 0)))
        out_shape.append(jax.ShapeDtypeStruct((nt // c, 8, 128), F32))
        out_specs.append(pl.BlockSpec((cpt, 8, 128), lambda i: (i, 0, 0)))
    return pl.pallas_call(
        functools.partial(_gdn_prep_body, tps=tps),
        grid=(nt // tr,),
        in_specs=[pl.BlockSpec((tr, W_QKV), lambda i: (i, 0)),
                  pl.BlockSpec((8, W_QKV), lambda i: (jnp.maximum(i * (tr // 8) - 1, 0), 0)),
                  pl.BlockSpec((8, W_QKV), lambda i: (jnp.minimum((i + 1) * (tr // 8), nblk8 - 1), 0)),
                  pl.BlockSpec((tr, W_AB), lambda i: (i, 0)),
                  pl.BlockSpec((3, W_QKV), lambda i: (0, 0)),
                  pl.BlockSpec((1, 128), lambda i: (0, 0)),
                  pl.BlockSpec((1, 128), lambda i: (0, 0))],
        out_specs=out_specs,
        out_shape=out_shape,
        compiler_params=_cparams(("parallel",)),
        name="gdn_prep",
    )(qkv, qkv, qkv, ab, conv_w, alog_row, dtb_row)


def _gdn_scan_body(*refs, nch):
    ins = refs[:12]
    s0_ref = refs[12]
    o_refs = refs[13:15]
    sfin_ref = refs[15]
    s_scr = refs[16]
    n = pl.program_id(1)
    c = GDN_CHUNK

    @pl.when(n == 0)
    def _():
        s_scr[...] = s0_ref[0]

    chains = [(d, h) for d in range(2) for h in range(GDN_HEADS)]
    state, prod, vnew = {}, {}, {}
    for d, h in chains:
        w_ref, qd_ref = ins[6 * d + 1], ins[6 * d + 3]
        sl = slice(h * GDN_DV, (h + 1) * GDN_DV)
        state[d, h] = s_scr[d, h]
        wq = jnp.concatenate([w_ref[:, sl], qd_ref[:, sl]], axis=0)
        prod[d, h] = _dot(wq, state[d, h].astype(BF16))
    for d, h in chains:
        u_ref, qk_ref = ins[6 * d], ins[6 * d + 4]
        sl = slice(h * GDN_DV, (h + 1) * GDN_DV)
        r = prod[d, h]
        vnew[d, h] = (u_ref[:, sl] - r[:c]).astype(BF16)
        o_refs[d][:, sl] = r[c:] + _dot(qk_ref[:, h * c:(h + 1) * c], vnew[d, h])
    for d, h in chains:
        kd_ref, eg_ref = ins[6 * d + 2], ins[6 * d + 5]
        sl = slice(h * GDN_DV, (h + 1) * GDN_DV)
        s_scr[d, h] = state[d, h] * eg_ref[0, h:h + 1, :] + _dot_tn(kd_ref[:, sl], vnew[d, h])

    @pl.when(n == nch - 1)
    def _():
        sfin_ref[0] = s_scr[...]


def _gdn_scan(prep, s0, batch, seq_len):
    c = GDN_CHUNK
    nch = seq_len // c
    nt = batch * seq_len
    fwd = lambda b, n: (b * nch + n, 0)
    bwd = lambda b, n: (b * nch + (nch - 1 - n), 0)
    in_specs = []
    for d, im in enumerate((fwd, bwd)):
        for wd in (GDN_W, GDN_W, GDN_W, GDN_W, GDN_HEADS * c):
            in_specs.append(pl.BlockSpec((c, wd), im))
        in_specs.append(pl.BlockSpec((1, 8, 128), (lambda im_: (lambda b, n: im_(b, n) + (0,)))(im)))
    st_spec = pl.BlockSpec((1, 2, GDN_HEADS, GDN_DK, GDN_DV), lambda b, n: (b, 0, 0, 0, 0))
    in_specs.append(st_spec)
    o_f, o_b, s_fin = pl.pallas_call(
        functools.partial(_gdn_scan_body, nch=nch),
        grid=(batch, nch),
        in_specs=in_specs,
        out_specs=[pl.BlockSpec((c, GDN_W), fwd), pl.BlockSpec((c, GDN_W), bwd), st_spec],
        out_shape=[jax.ShapeDtypeStruct((nt, GDN_W), F32), jax.ShapeDtypeStruct((nt, GDN_W), F32),
                   jax.ShapeDtypeStruct(s0.shape, F32)],
        scratch_shapes=[pltpu.VMEM((2, GDN_HEADS, GDN_DK, GDN_DV), F32)],
        compiler_params=_cparams(("arbitrary", "arbitrary")),
        name="gdn_scan",
    )(*prep, s0)
    return o_f, o_b, s_fin


def _group_norm_rope(x, w, bd, cos, sin):
    xx = x * x
    hi, lo = _split_bf16(xx)
    ss = _dot(hi, bd) + _dot(lo, bd)
    xn = x * lax.rsqrt(ss * (1.0 / ATT_HD) + NORM_EPS) * w
    if cos is not None:
        width = x.shape[1]
        lane = lax.broadcasted_iota(jnp.int32, x.shape, 1)
        from_below = pltpu.roll(xn, ROPE_PAIRS, 1)
        from_above = pltpu.roll(xn, width - ROPE_PAIRS, 1)
        partner = jnp.where((lane % (2 * ROPE_PAIRS)) < ROPE_PAIRS, from_above, from_below)
        xn = xn * cos + partner * sin
    return xn


def _attention_operands(q, kv, cos, sin, qw, kw, bd, qh_ref, kh_ref, vh_ref):
    nk = ATT_KV_HEADS * ATT_HD
    kcos = ksin = None
    if cos is not None:
        kcos, ksin = cos[:, :nk], sin[:, :nk]
    q = _group_norm_rope(q, qw, bd, cos, sin) * (ATT_HD ** -0.5 * math.log2(math.e))
    k = _group_norm_rope(kv[:, :nk], kw, bd[:nk, :nk], kcos, ksin)
    v = kv[:, nk:]
    for j in range(ATT_HEADS):
        qh_ref[j] = q[:, j * ATT_HD:(j + 1) * ATT_HD].astype(BF16)
    for j in range(ATT_KV_HEADS):
        kh_ref[j] = k[:, j * ATT_HD:(j + 1) * ATT_HD].astype(BF16)
        vh_ref[j] = v[:, j * ATT_HD:(j + 1) * ATT_HD].astype(BF16)


def _attention_body(q_ref, *refs, src_lens, tk, tq):
    n_src = len(src_lens)
    o_ref = refs[2 * n_src]
    rows = ATT_GROUP * tq
    q = q_ref[...].reshape(rows, ATT_HD)

    def block(sc, v, carry):
        m, l, acc = carry
        m_new = jnp.maximum(m, jnp.max(sc, axis=1, keepdims=True))
        alpha = jnp.exp2(m - m_new)
        p = jnp.exp2(sc - m_new)
        l = alpha * l + jnp.sum(p, axis=1, keepdims=True)
        acc = alpha * acc + _dot(p.astype(BF16), v)
        return m_new, l, acc

    carry = (jnp.full((rows, 1), -jnp.inf, F32), jnp.zeros((rows, 1), F32), jnp.zeros((rows, ATT_HD), F32))
    for s in range(n_src):
        k_ref, v_ref = refs[2 * s], refs[2 * s + 1]
        blk = min(tk, src_lens[s])
        nblk = src_lens[s] // blk
        for j in range(nblk):
            sc = _dot_nt(q, k_ref[0, j * blk:(j + 1) * blk, :])
            carry = block(sc, v_ref[0, j * blk:(j + 1) * blk, :], carry)
    m, l, acc = carry
    o = (acc / l).reshape(ATT_GROUP, tq, ATT_HD)
    o_ref[...] = jnp.concatenate([o[j] for j in range(ATT_GROUP)], axis=1)


def _attention(qh, sources, q_len):
    nq = qh.shape[1]
    tq = 128
    tk = 8192
    tpb = q_len // tq
    in_specs = [pl.BlockSpec((ATT_GROUP, tq, ATT_HD), lambda i, g: (g, i, 0))]
    args = [qh]
    for kh, vh, sl in sources:
        spec = pl.BlockSpec((1, sl, ATT_HD), lambda i, g: (g, i // tpb, 0))
        in_specs += [spec, spec]
        args += [kh, vh]
    return pl.pallas_call(
        functools.partial(_attention_body, src_lens=tuple(s[2] for s in sources), tk=tk, tq=tq),
        grid=(nq // tq, ATT_KV_HEADS),
        in_specs=in_specs,
        out_specs=pl.BlockSpec((tq, ATT_GROUP * ATT_HD), lambda i, g: (i, g)),
        out_shape=jax.ShapeDtypeStruct((nq, W_Q), F32),
        compiler_params=_cparams(("parallel", "parallel")),
        name="attention",
    )(*args)


def _merge_body(of_ref, ob_ref, z_ref, p_ref, pp_ref, pn_ref, oc_ref, gt_ref, h_ref, mod_ref,
                gnw_ref, cw_ref, wb_ref, wo_ref, o_ref, *, tps):
    i = pl.program_id(0)
    first = (i % tps) == 0
    last = (i % tps) == tps - 1
    o = of_ref[...] + ob_ref[...]
    z = z_ref[...]
    parts = []
    for h in range(GDN_HEADS):
        sl = slice(h * GDN_DV, (h + 1) * GDN_DV)
        oh = o[:, sl]
        oh = oh * lax.rsqrt(jnp.mean(oh * oh, axis=-1, keepdims=True) + NORM_EPS) * gnw_ref[...]
        zh = z[:, sl]
        parts.append(oh * (zh * jax.nn.sigmoid(zh)))
    br_a = jnp.concatenate(parts, axis=1)
    p = p_ref[...]
    tm = p.shape[0]
    bg = p[:, :SCONV_W]
    cx = p[:, SCONV_W:2 * SCONV_W] * p[:, 2 * SCONV_W:]
    pp = pp_ref[7:8, :]
    pn = pn_ref[0:1, :]
    prev_row = jnp.where(first, 0.0, pp[:, SCONV_W:2 * SCONV_W] * pp[:, 2 * SCONV_W:])
    next_row = jnp.where(last, 0.0, pn[:, SCONV_W:2 * SCONV_W] * pn[:, 2 * SCONV_W:])
    rows = lax.broadcasted_iota(jnp.int32, cx.shape, 0)
    cprev = jnp.where(rows == 0, prev_row, pltpu.roll(cx, 1, 0))
    cnext = jnp.where(rows == tm - 1, next_row, pltpu.roll(cx, tm - 1, 0))
    br_b = bg * (cw_ref[0:1, :] * cprev + cw_ref[1:2, :] * cx + cw_ref[2:3, :] * cnext)
    br_c = oc_ref[...]
    mixed = None
    for n, br in enumerate((br_a, br_b, br_c)):
        t = _dot(br.astype(BF16), wb_ref[n]) * jax.nn.sigmoid(gt_ref[:, n * D_MODEL:(n + 1) * D_MODEL])
        mixed = t if mixed is None else mixed + t
    y = _dot(mixed.astype(BF16), wo_ref[...])
    o_ref[...] = h_ref[...] + mod_ref[0, 2:3, :] * y


def _merge(o_f, o_b, z, psc, oc, gates, h, mod, mod_idx, gnw_row, conv_b_w, wb, wo, seq_len):
    nt, d = h.shape
    tm = 256
    tps = seq_len // tm
    nblk8 = nt // 8
    row = lambda i: (i, 0)
    return pl.pallas_call(
        functools.partial(_merge_body, tps=tps),
        grid=(nt // tm,),
        in_specs=[pl.BlockSpec((tm, GDN_W), row),
                  pl.BlockSpec((tm, GDN_W), row),
                  pl.BlockSpec((tm, W_Z), row),
                  pl.BlockSpec((tm, W_SC), row),
                  pl.BlockSpec((8, W_SC), lambda i: (jnp.maximum(i * (tm // 8) - 1, 0), 0)),
                  pl.BlockSpec((8, W_SC), lambda i: (jnp.minimum((i + 1) * (tm // 8), nblk8 - 1), 0)),
                  pl.BlockSpec((tm, W_Q), row),
                  pl.BlockSpec((tm, W_G), row),
                  pl.BlockSpec((tm, d), row),
                  pl.BlockSpec((1, 6, d), lambda i: (mod_idx(i, tm), 0, 0)),
                  pl.BlockSpec((1, GDN_DV), lambda i: (0, 0)),
                  pl.BlockSpec((3, SCONV_W), lambda i: (0, 0)),
                  pl.BlockSpec((3, SCONV_W, d), lambda i: (0, 0, 0)),
                  pl.BlockSpec((d, d), lambda i: (0, 0))],
        out_specs=pl.BlockSpec((tm, d), row),
        out_shape=jax.ShapeDtypeStruct((nt, d), F32),
        compiler_params=_cparams(("parallel",)),
        name="merge",
    )(o_f, o_b, z, psc, psc, psc, oc, gates, h, mod, gnw_row, conv_b_w, wb, wo)


PEER_NO_RANK = 64.0


def _oddeven_merge_sort_pairs(n):
    pairs = []
    p = 1
    while p < n:
        k = p
        while k >= 1:
            for j in range(k % p, n - k, 2 * k):
                for i in range(min(k, n - j - k)):
                    if (i + j) // (2 * p) == (i + j + k) // (2 * p):
                        pairs.append((i + j, i + j + k))
            k //= 2
        p *= 2
    return pairs


def _top_rows_sorted(s):
    cnt = PEER_TOPK
    v = [s[8 * i:8 * (i + 1), :] for i in range(s.shape[0] // 8)]
    for i, j in _oddeven_merge_sort_pairs(len(v)):
        v[i], v[j] = jnp.maximum(v[i], v[j]), jnp.minimum(v[i], v[j])
    for shift in (4, 2, 1):
        other = [pltpu.roll(x, shift, 0) for x in v]
        if len(v) == cnt:
            v = [jnp.maximum(v[k], other[cnt - 1 - k]) for k in range(cnt)]
        else:
            v = v + other[::-1]
        d = cnt // 2
        while d >= 1:
            for k in range(cnt):
                if k & d == 0:
                    v[k], v[k + d] = jnp.maximum(v[k], v[k + d]), jnp.minimum(v[k], v[k + d])
            d //= 2
    return [x[0:1, :] for x in v]


def _peer_body(h_ref, mod_ref, nw_ref, fw_ref, wq_ref, sk_ref, u_ref, vt_ref, o_ref,
               xt_scr, out_scr, sc_scr, cnt_scr, ea_scr, rank_scr, eb_scr, cand_scr, gw_scr,
               *, tm, ec, nchunks, final_norm):
    c = pl.program_id(1)
    nk = PEER_NKEYS
    half = nk // 2
    tb_w = PEER_TB
    ntb = tm // tb_w
    nlb = tm // 128
    lb_per_tb = tb_w // 128

    @pl.when(c == 0)
    def _():
        u = _rms_mod(h_ref[...], nw_ref[...], mod_ref[0, 3:4, :], mod_ref[0, 4:5, :])
        ut = u.T
        uh = ut.astype(BF16)
        qt = _dot(wq_ref[...], uh)
        for tb in range(ntb):
            xt_scr[tb] = uh[:, tb * tb_w:(tb + 1) * tb_w]
            out_scr[tb] = jnp.zeros(out_scr.shape[1:], F32)
        for h in range(PEER_HEADS):
            for p in range(2):
                qs = qt[h * nk + p * half:h * nk + (p + 1) * half, :]
                s = _dot(sk_ref[h, p], qs.astype(BF16))
                for lb in range(nlb):
                    sc_scr[p, h * nlb + lb] = s[:, lb * 128:(lb + 1) * 128]

        def select_one(hl):
            s0 = sc_scr[0, hl]
            s1 = sc_scr[1, hl]
            top0 = _top_rows_sorted(s0)
            top1 = _top_rows_sorted(s1)
            rank1 = jnp.full(s1.shape, PEER_NO_RANK, F32)
            for l in reversed(range(PEER_TOPK)):
                rank1 = jnp.where(s1 >= top1[l], float(l + 1), rank1)
            pairs = [(k, l) for k in range(PEER_TOPK) for l in range(PEER_TOPK) if (k + 1) * (l + 1) <= PEER_TOPK]
            cand = cand_scr.at[hl % PEER_SELECT_GROUP]
            cand[...] = jnp.full(cand.shape, -jnp.inf, F32)
            sums = {}
            for r, (k, l) in enumerate(pairs):
                sums[k, l] = top0[k] + top1[l]
                cand[r:r + 1, :] = sums[k, l]
            best = _top_rows_sorted(cand[...])
            tau = best[PEER_TOPK - 1]
            zsum = jnp.zeros_like(tau)
            for b_ in best:
                zsum = zsum + jnp.exp(b_ - best[0])
            cnt = jnp.zeros(s0.shape, F32)
            for k in range(PEER_TOPK):
                n_sel = jnp.zeros_like(tau)
                for l in range(PEER_TOPK // (k + 1)):
                    n_sel = n_sel + jnp.where(sums[k, l] >= tau, 1.0, 0.0)
                cnt = jnp.where(s0 == top0[k], n_sel, cnt)
            cnt_scr[hl] = cnt.reshape(nk // 8, 8, 128)
            ea_scr[hl] = (jnp.exp(s0 - top0[0]) / zsum).reshape(nk // 8, 8, 128)
            rank_scr[hl] = rank1.astype(BF16)
            eb_scr[hl] = jnp.exp(s1 - top1[0]).astype(BF16)

        def select(grp, carry):
            for k in range(PEER_SELECT_GROUP):
                select_one(PEER_SELECT_GROUP * grp + k)
            return carry

        lax.fori_loop(0, PEER_HEADS * nlb // PEER_SELECT_GROUP, select, 0)

    nblk = ec // nk
    piece = 16
    npiece = nk // piece

    def half_dots(lhs_ref, rhs):
        hm = lhs_ref.shape[0] // 2
        return [_dot(lhs_ref[0:hm, :], rhs), _dot(lhs_ref[hm:2 * hm, :], rhs)]

    def gated_activation(tb, acts):
        hm = ec // 2
        for sub in range(lb_per_tb):
            lb = tb * lb_per_tb + sub
            ls = slice(sub * 128, (sub + 1) * 128)
            for ii in range(nblk):
                wm = [None] * npiece
                for h in range(PEER_HEADS):
                    cnt_row = jnp.broadcast_to(cnt_scr[h * nlb + lb, c * (nblk // 8) + ii // 8, ii % 8:ii % 8 + 1, :], (piece, 128)).astype(BF16)
                    ea_row = jnp.broadcast_to(ea_scr[h * nlb + lb, c * (nblk // 8) + ii // 8, ii % 8:ii % 8 + 1, :], (piece, 128)).astype(BF16)
                    for jp in range(npiece):
                        js = slice(jp * piece, (jp + 1) * piece)
                        sel = rank_scr[h * nlb + lb, js, :] <= cnt_row
                        w = eb_scr[h * nlb + lb, js, :] * ea_row
                        w = jnp.where(sel, w, jnp.zeros_like(w))
                        wm[jp] = w if h == 0 else wm[jp] + w
                for jp in range(npiece):
                    r0 = ii * nk + jp * piece
                    ap = acts[r0 // hm][r0 % hm:r0 % hm + piece, ls]
                    gl = 0.5 * ap * (1.0 + lax.erf(ap * (0.5 ** 0.5)))
                    gw_scr[tb, r0:r0 + piece, ls] = gl.astype(BF16) * wm[jp]

    acts = [half_dots(u_ref, xt_scr[tb]) for tb in range(ntb)]
    for tb in range(ntb):
        gated_activation(tb, acts[tb])
        outs = half_dots(vt_ref.at[0], gw_scr[tb])
        hd = out_scr.shape[1] // 2
        out_scr[tb, 0:hd, :] += outs[0]
        out_scr[tb, hd:2 * hd, :] += outs[1]

    @pl.when(c == nchunks - 1)
    def _():
        out = jnp.concatenate([out_scr[tb] for tb in range(ntb)], axis=1)
        hn = h_ref[...] + mod_ref[0, 5:6, :] * out.T
        if final_norm:
            hn = hn * lax.rsqrt(jnp.mean(hn * hn, axis=-1, keepdims=True) + NORM_EPS) * fw_ref[...]
        o_ref[...] = hn


def _peer(h, mod, mod_idx, nw, final_w, apply_final, wq_t, sk, u_tab, vt_tab):
    nt, d = h.shape
    tm = PEER_TM
    ec = PEER_EC
    ne = u_tab.shape[0]
    nchunks = ne // ec
    nk = PEER_NKEYS
    ntb = tm // PEER_TB
    nlb = tm // 128
    const2 = lambda i, c: (0, 0)
    per_head = lambda dt: pltpu.VMEM((PEER_HEADS * nlb, nk, 128), dt)
    per_head_rows = pltpu.VMEM((PEER_HEADS * nlb, nk // 8, 8, 128), F32)
    return pl.pallas_call(
        functools.partial(_peer_body, tm=tm, ec=ec, nchunks=nchunks, final_norm=apply_final),
        grid=(nt // tm, nchunks),
        in_specs=[pl.BlockSpec((tm, d), lambda i, c: (i, 0)),
                  pl.BlockSpec((1, 6, d), lambda i, c: (mod_idx(i, tm), 0, 0)),
                  pl.BlockSpec((1, d), const2),
                  pl.BlockSpec((1, d), const2),
                  pl.BlockSpec((d, d), const2),
                  pl.BlockSpec((PEER_HEADS, 2, nk, nk // 2), lambda i, c: (0, 0, 0, 0)),
                  pl.BlockSpec((ec, d), lambda i, c: (c, 0)),
                  pl.BlockSpec((1, d, ec), lambda i, c: (c, 0, 0))],
        out_specs=pl.BlockSpec((tm, d), lambda i, c: (i, 0)),
        out_shape=jax.ShapeDtypeStruct((nt, d), F32),
        scratch_shapes=[pltpu.VMEM((ntb, d, PEER_TB), BF16),
                        pltpu.VMEM((ntb, d, PEER_TB), F32),
                        pltpu.VMEM((2, PEER_HEADS * nlb, nk, 128), F32),
                        per_head_rows,
                        per_head_rows,
                        per_head(BF16),
                        per_head(BF16),
                        pltpu.VMEM((PEER_SELECT_GROUP, 64, 128), F32),
                        pltpu.VMEM((ntb, ec, PEER_TB), BF16)],
        compiler_params=_cparams(("parallel", "arbitrary")),
        name="peer",
    )(h, mod, nw.reshape(1, d), final_w.reshape(1, d), wq_t, sk, u_tab, vt_tab)


def _rope_tables(seq):
    t = jnp.arange(seq)
    row = (t // GRID_W).astype(F32)
    col = (t % GRID_W).astype(F32)
    freqs = ROPE_THETA ** (-jnp.arange(ROPE_PAIRS, dtype=F32) / ROPE_PAIRS)
    ar = row[:, None] * freqs
    ac = col[:, None] * freqs
    cos = jnp.concatenate([jnp.cos(ar), jnp.cos(ar), jnp.cos(ac), jnp.cos(ac)], axis=1)
    sin = jnp.concatenate([-jnp.sin(ar), jnp.sin(ar), -jnp.sin(ac), jnp.sin(ac)], axis=1)
    return jnp.tile(cos, (1, ATT_HEADS)), jnp.tile(sin, (1, ATT_HEADS))


def _reorder_w_in(w):
    sizes = (W_QKV, W_Z, 4 * GDN_HEADS, W_SC, W_Q, W_KV, W_G)
    parts, start = [], 0
    for s in sizes:
        parts.append(w[:, start:start + s])
        start += s
    qkv, z, ab, sc, q, kv, g = parts
    ab = jnp.pad(ab, ((0, 0), (0, W_AB - ab.shape[1])))
    return jnp.concatenate([qkv, z, sc, q, kv, g, ab], axis=1).astype(BF16)


def _pad_row(v, n=128):
    v = v.reshape(1, -1)
    return jnp.pad(v, ((0, 0), (0, n - v.shape[1])))


def kernel(x, c, ctx, c_ctx, w_mod, b_mod, norm1_w, w_in, conv_a_w, a_log, dt_bias, gdn_norm_w, conv_b_w,
           q_norm_w, k_norm_w, w_branch, w_out, norm2_w, w_query, sub_keys, expert_u, expert_v, final_norm_w):
    batch, seq, d = x.shape
    ctx_len = ctx.shape[1]
    depth = w_mod.shape[0]
    h = x.reshape(batch * seq, d)
    hc = ctx.reshape(batch * ctx_len, d)

    mod_rows = -(-(batch + 1) // 8) * 8
    c_all = jnp.zeros((mod_rows, d), F32).at[:batch].set(c).at[batch].set(c_ctx)
    cos, sin = _rope_tables(seq)
    hd_i = jnp.arange(W_Q) // ATT_HD
    bd = (hd_i[:, None] == hd_i[None, :]).astype(BF16)
    lat_idx = lambda i, tm: i // (seq // tm)
    ctx_idx = lambda i, tm: batch

    for layer in range(depth):
        last = layer == depth - 1
        mod = _modulation(c_all, w_mod[layer], b_mod[layer]).reshape(mod_rows, 6, d)
        w_in_r = _reorder_w_in(w_in[layer])
        alog_row = _pad_row(a_log[layer])
        dtb_row = _pad_row(dt_bias[layer])
        qw_row = jnp.tile(q_norm_w[layer], ATT_HEADS).reshape(1, W_Q)
        kw_row = jnp.tile(k_norm_w[layer], ATT_KV_HEADS).reshape(1, ATT_KV_HEADS * ATT_HD)
        gnw_row = gdn_norm_w[layer].reshape(1, GDN_DV)
        wb = w_branch[layer].astype(BF16)
        wo = w_out[layer].astype(BF16)
        wq_t = w_query[layer].T.astype(BF16)
        sk = sub_keys[layer].astype(BF16)
        u_tab = expert_u[layer].astype(BF16)
        vt_tab = expert_v[layer].astype(BF16).reshape(-1, PEER_EC, d).transpose(0, 2, 1)

        qkv_l, z_l, sc_l, g_l, ab_l, qh_l, kh_l, vh_l = _inproj(h, mod, lat_idx, norm1_w[layer], w_in_r, qw_row, kw_row, bd,
                                                                cos, sin, seq)
        qkv_c, z_c, sc_c, g_c, ab_c, qh_c, kh_c, vh_c = _inproj(hc, mod, ctx_idx, norm1_w[layer], w_in_r, qw_row, kw_row, bd)

        prep_c = _gdn_prep(qkv_c, ab_c, conv_a_w[layer], alog_row, dtb_row, ctx_len)
        prep_l = _gdn_prep(qkv_l, ab_l, conv_a_w[layer], alog_row, dtb_row, seq)
        s_zero = jnp.zeros((batch, 2, GDN_HEADS, GDN_DK, GDN_DV), F32)
        ocf, ocb, s_ctx = _gdn_scan(prep_c, s_zero, batch, ctx_len)
        olf, olb, _ = _gdn_scan(prep_l, s_ctx, batch, seq)

        att_l = _attention(qh_l, [(kh_l, vh_l, seq), (kh_c, vh_c, ctx_len)], seq)

        h = _merge(olf, olb, z_l, sc_l, att_l, g_l, h, mod, lat_idx, gnw_row, conv_b_w[layer],
                   wb, wo, seq)
        h = _peer(h, mod, lat_idx, norm2_w[layer], final_norm_w, last, wq_t, sk, u_tab, vt_tab)
        if not last:
            att_c = _attention(qh_c, [(kh_c, vh_c, ctx_len)], ctx_len)
            hc = _merge(ocf, ocb, z_c, sc_c, att_c, g_c, hc, mod, ctx_idx, gnw_row, conv_b_w[layer],
                        wb, wo, ctx_len)
            hc = _peer(hc, mod, ctx_idx, norm2_w[layer], final_norm_w, False, wq_t, sk, u_tab, vt_tab)
    return h.reshape(batch, seq, d)
```

```python
import functools
import math

import jax
import jax.numpy as jnp
from jax import lax
from jax.experimental import pallas as pl
from jax.experimental.pallas import tpu as pltpu

F32 = jnp.float32
BF16 = jnp.bfloat16

D_MODEL = 1024
GRID_W = 64
GDN_HEADS = 4
GDN_DK = 128
GDN_DV = 128
GDN_CHUNK = 64
GDN_TILE = 256
GDN_W = GDN_HEADS * GDN_DV
SCONV_W = 512
ATT_HEADS = 8
ATT_KV_HEADS = 2
ATT_GROUP = 4
ATT_HD = 64
ROPE_THETA = 10000.0
ROPE_PAIRS = 16
PEER_HEADS = 8
PEER_NKEYS = 128
PEER_TOPK = 16
PEER_TB = 256
PEER_TM = 512
PEER_EC = 2048
PEER_SELECT_GROUP = 4
NORM_EPS = 1e-6

W_QKV = 3 * GDN_W
W_Z = GDN_W
W_SC = 3 * SCONV_W
W_Q = ATT_HEADS * ATT_HD
W_KV = 2 * ATT_KV_HEADS * ATT_HD
W_G = 3 * D_MODEL
W_AB = 128
INPROJ_WIDTHS = (W_QKV, W_Z, W_SC, W_Q, W_KV, W_G, W_AB)

VMEM_LIMIT = 56 * 1024 * 1024


def _cparams(sem):
    return pltpu.CompilerParams(dimension_semantics=sem, vmem_limit_bytes=VMEM_LIMIT)


def _split_bf16(x):
    hi = x.astype(BF16)
    lo = (x - hi.astype(F32)).astype(BF16)
    return hi, lo


def _dot(a, b):
    return jnp.dot(a, b, preferred_element_type=F32)


def _dot_nt(a, b):
    return lax.dot_general(a, b, (((1,), (1,)), ((), ())), preferred_element_type=F32)


def _dot_tn(a, b):
    return lax.dot_general(a, b, (((0,), (0,)), ((), ())), preferred_element_type=F32)


def _dot3(a, b):
    ah, al = _split_bf16(a)
    bh, bl = _split_bf16(b)
    return _dot(ah, bh) + _dot(ah, bl) + _dot(al, bh)


def _rms_mod(x, nw, shift, scale):
    ms = jnp.mean(x * x, axis=-1, keepdims=True)
    return (x * lax.rsqrt(ms + NORM_EPS) * nw) * (1.0 + scale) + shift


def _mod_body(c_ref, w_ref, b_ref, o_ref):
    c = c_ref[...]
    a = c * jax.nn.sigmoid(c)
    o_ref[...] = _dot3(a, w_ref[...]) + b_ref[...]


def _modulation(c_all, w_mod, b_mod):
    rows, d = c_all.shape
    n = w_mod.shape[1]
    tn = 1536
    return pl.pallas_call(
        _mod_body,
        grid=(n // tn,),
        in_specs=[pl.BlockSpec((rows, d), lambda j: (0, 0)),
                  pl.BlockSpec((d, tn), lambda j: (0, j)),
                  pl.BlockSpec((1, tn), lambda j: (0, j))],
        out_specs=pl.BlockSpec((rows, tn), lambda j: (0, j)),
        out_shape=jax.ShapeDtypeStruct((rows, n), F32),
        compiler_params=_cparams(("parallel",)),
        name="modulation",
    )(c_all, w_mod, b_mod.reshape(1, n))


def _inproj_body(*refs, rope):
    h_ref, mod_ref, nw_ref, w_ref = refs[:4]
    rest = refs[4:]
    cos = sin = None
    if rope:
        cos, sin = rest[0][...], rest[1][...]
        rest = rest[2:]
    qw_ref, kw_ref, bd_ref, qkv_ref, z_ref, sc_ref, g_ref, ab_ref, qh_ref, kh_ref, vh_ref = rest
    u = _rms_mod(h_ref[...], nw_ref[...], mod_ref[0, 0:1, :], mod_ref[0, 1:2, :])
    ub = u.astype(BF16)
    offs = [sum(INPROJ_WIDTHS[:i]) for i in range(len(INPROJ_WIDTHS))]
    proj = lambda i: _dot(ub, w_ref[:, offs[i]:offs[i] + INPROJ_WIDTHS[i]])
    _attention_operands(proj(3), proj(4), cos, sin, qw_ref[...], kw_ref[...], bd_ref[...], qh_ref, kh_ref, vh_ref)
    qkv_ref[...] = proj(0)
    z_ref[...] = proj(1)
    sc_ref[...] = proj(2)
    g_ref[...] = proj(5)
    ab_ref[...] = proj(6)


def _inproj(h, mod, mod_idx, nw, w, qw_row, kw_row, bd, cos=None, sin=None, seq_len=None):
    nt, d = h.shape
    tm = 256
    wtot = w.shape[1]
    rope = cos is not None
    row = lambda i: (i, 0)
    const = lambda i: (0, 0)
    in_specs = [pl.BlockSpec((tm, d), row),
                pl.BlockSpec((1, 6, d), lambda i: (mod_idx(i, tm), 0, 0)),
                pl.BlockSpec((1, d), const),
                pl.BlockSpec((d, wtot), const, pipeline_mode=pl.Buffered(1))]
    args = [h, mod, nw.reshape(1, d), w]
    if rope:
        tps = seq_len // tm
        in_specs += [pl.BlockSpec((tm, W_Q), lambda i: (i % tps, 0))] * 2
        args += [cos, sin]
    in_specs += [pl.BlockSpec((1, W_Q), const),
                 pl.BlockSpec((1, ATT_KV_HEADS * ATT_HD), const),
                 pl.BlockSpec((W_Q, W_Q), const)]
    args += [qw_row, kw_row, bd]
    plain = (W_QKV, W_Z, W_SC, W_G, W_AB)
    heads = (ATT_HEADS, ATT_KV_HEADS, ATT_KV_HEADS)
    return pl.pallas_call(
        functools.partial(_inproj_body, rope=rope),
        grid=(nt // tm,),
        in_specs=in_specs,
        out_specs=[pl.BlockSpec((tm, wd), row) for wd in plain]
        + [pl.BlockSpec((nh, tm, ATT_HD), lambda i: (0, i, 0)) for nh in heads],
        out_shape=[jax.ShapeDtypeStruct((nt, wd), F32) for wd in plain]
        + [jax.ShapeDtypeStruct((nh, nt, ATT_HD), BF16) for nh in heads],
        compiler_params=_cparams(("parallel",)),
        name="inproj",
    )(*args)


def _bdot(a, b):
    return lax.dot_general(a, b, (((2,), (1,)), ((0,), (0,))), preferred_element_type=F32)


def _tri_inverse_minus_eye(n):
    nb = n.astype(BF16)
    p = _bdot(nb, nb)
    y = -n
    for step in range(5):
        y = y + p + _bdot(y.astype(BF16), p.astype(BF16))
        if step < 4:
            pb = p.astype(BF16)
            p = _bdot(pb, pb)
    return y


def _gdn_prep_body(x_ref, xp_ref, xn_ref, ab_ref, cw_ref, alog_ref, dtb_ref, *outs, tps):
    i = pl.program_id(0)
    first = (i % tps) == 0
    last = (i % tps) == tps - 1
    x = x_ref[...]
    nrow = x.shape[0]
    c = GDN_CHUNK
    cpt = nrow // c
    rows = lax.broadcasted_iota(jnp.int32, x.shape, 0)
    prev_row = jnp.where(first, 0.0, xp_ref[7:8, :])
    next_row = jnp.where(last, 0.0, xn_ref[0:1, :])
    xprev = jnp.where(rows == 0, prev_row, pltpu.roll(x, 1, 0))
    xnext = jnp.where(rows == nrow - 1, next_row, pltpu.roll(x, nrow - 1, 0))
    y = cw_ref[0:1, :] * xprev + cw_ref[1:2, :] * x + cw_ref[2:3, :] * xnext
    y = y * jax.nn.sigmoid(y)

    ab = ab_ref[...]
    zz = ab + dtb_ref[...]
    sp = jnp.maximum(zz, 0.0) + jnp.log1p(jnp.exp(-jnp.abs(zz)))
    g = -jnp.exp(alog_ref[...]) * sp
    beta = jax.nn.sigmoid(ab)

    ri = lax.broadcasted_iota(jnp.int32, (nrow, nrow), 0)
    ci = lax.broadcasted_iota(jnp.int32, (nrow, nrow), 1)
    same = (ri // c) == (ci // c)
    lower = (same & (ri >= ci)).astype(BF16)
    ones_blk = same.astype(BF16)
    g1 = g.astype(BF16)
    r1 = g - g1.astype(F32)
    g2 = r1.astype(BF16)
    g3 = (r1 - g2.astype(F32)).astype(BF16)
    gc_f = _dot(lower, g1) + _dot(lower, g2) + _dot(lower, g3)
    tot = _dot(ones_blk, g1) + _dot(ones_blk, g2) + _dot(ones_blk, g3)
    gc_b = tot - gc_f + g
    gcf_t = gc_f.T
    gcb_t = gc_b.T
    egc_f = jnp.exp(gc_f)
    egc_b = jnp.exp(gc_b)
    kdec_f = jnp.exp(tot - gc_f)
    kdec_b = jnp.exp(tot - gc_b)
    etot = jnp.exp(tot)

    pi = lax.broadcasted_iota(jnp.int32, (2 * c, 2 * c), 0)
    pj = lax.broadcasted_iota(jnp.int32, (2 * c, 2 * c), 1)
    top = (pi < c) & (pj < c)
    bot = (pi >= c) & (pj >= c)
    incl = (top & (pi >= pj)) | (bot & (pi <= pj))
    strict = (top & (pi > pj)) | (bot & (pi < pj))

    for d in range(2):
        outs[6 * d + 5][...] = jnp.zeros(outs[6 * d + 5].shape, F32)

    n_list, rhs_list, qk_list = [], [], []
    for h in range(GDN_HEADS):
        q = y[:, h * GDN_DK:(h + 1) * GDN_DK]
        k = y[:, GDN_W + h * GDN_DK:GDN_W + (h + 1) * GDN_DK]
        v = y[:, 2 * GDN_W + h * GDN_DV:2 * GDN_W + (h + 1) * GDN_DV]
        q = q * lax.rsqrt(jnp.sum(q * q, axis=-1, keepdims=True) + NORM_EPS) * (GDN_DK ** -0.5)
        k = k * lax.rsqrt(jnp.sum(k * k, axis=-1, keepdims=True) + NORM_EPS)
        cf, cb = h, GDN_HEADS + h
        bf_, bb_ = 2 * GDN_HEADS + h, 3 * GDN_HEADS + h
        sl = slice(h * GDN_DV, (h + 1) * GDN_DV)
        outs[2][:, sl] = (k * kdec_f[:, cf:cf + 1]).astype(BF16)
        outs[8][:, sl] = (k * kdec_b[:, cb:cb + 1]).astype(BF16)
        outs[3][:, sl] = (q * egc_f[:, cf:cf + 1]).astype(BF16)
        outs[9][:, sl] = (q * egc_b[:, cb:cb + 1]).astype(BF16)
        for ch in range(cpt):
            r = slice(ch * c, (ch + 1) * c)
            k2 = jnp.concatenate([k[r], k[r]], axis=0)
            q2 = jnp.concatenate([q[r], q[r]], axis=0)
            v2 = jnp.concatenate([v[r], v[r]], axis=0)
            kb2 = k2.astype(BF16)
            kk2 = _dot_nt(kb2, kb2)
            qk2 = _dot_nt(q2.astype(BF16), kb2)
            gcol = jnp.concatenate([gc_f[r, cf:cf + 1], gc_b[r, cb:cb + 1]], axis=0)
            grow = jnp.concatenate([gcf_t[cf:cf + 1, r], gcb_t[cb:cb + 1, r]], axis=1)
            bcol = jnp.concatenate([beta[r, bf_:bf_ + 1], beta[r, bb_:bb_ + 1]], axis=0)
            ecol = jnp.concatenate([egc_f[r, cf:cf + 1], egc_b[r, cb:cb + 1]], axis=0)
            ldec = jnp.where(incl, jnp.exp(jnp.where(incl, gcol - grow, 0.0)), 0.0)
            n_list.append(jnp.where(strict, kk2 * bcol * ldec, 0.0))
            rhs_list.append(jnp.concatenate([v2 * bcol, k2 * (bcol * ecol)], axis=1))
            qk_list.append(qk2 * ldec)
            for d, col in ((0, cf), (1, cb)):
                outs[6 * d + 5][ch, h:h + 1, :] = jnp.broadcast_to(etot[ch * c:ch * c + 1, col:col + 1], (1, 128))

    yinv = _tri_inverse_minus_eye(jnp.stack(n_list, axis=0))
    rhs = jnp.stack(rhs_list, axis=0)
    sol = rhs + _bdot(yinv.astype(BF16), rhs.astype(BF16))
    for h in range(GDN_HEADS):
        sl = slice(h * GDN_DV, (h + 1) * GDN_DV)
        for ch in range(cpt):
            r = slice(ch * c, (ch + 1) * c)
            s = sol[h * cpt + ch]
            qk = qk_list[h * cpt + ch]
            outs[0][r, sl] = s[:c, :GDN_DV]
            outs[1][r, sl] = s[:c, GDN_DV:].astype(BF16)
            outs[6][r, sl] = s[c:, :GDN_DV]
            outs[7][r, sl] = s[c:, GDN_DV:].astype(BF16)
            outs[4][r, h * c:(h + 1) * c] = qk[:c, :c].astype(BF16)
            outs[10][r, h * c:(h + 1) * c] = qk[c:, c:].astype(BF16)


def _gdn_prep(qkv, ab, conv_w, alog_row, dtb_row, seq_len):
    nt = qkv.shape[0]
    c = GDN_CHUNK
    tr = GDN_TILE
    cpt = tr // c
    tps = seq_len // tr
    nblk8 = nt // 8
    out_shape, out_specs = [], []
    for _ in range(2):
        for wd, dt in ((GDN_W, F32), (GDN_W, BF16), (GDN_W, BF16), (GDN_W, BF16), (GDN_HEADS * c, BF16)):
            out_shape.append(jax.ShapeDtypeStruct((nt, wd), dt))
            out_specs.append(pl.BlockSpec((tr, wd), lambda i: (i, 0)))
        out_shape.append(jax.ShapeDtypeStruct((nt // c, 8, 128), F32))
        out_specs.append(pl.BlockSpec((cpt, 8, 128), lambda i: (i, 0, 0)))
    return pl.pallas_call(
        functools.partial(_gdn_prep_body, tps=tps),
        grid=(nt // tr,),
        in_specs=[pl.BlockSpec((tr, W_QKV), lambda i: (i, 0)),
                  pl.BlockSpec((8, W_QKV), lambda i: (jnp.maximum(i * (tr // 8) - 1, 0), 0)),
                  pl.BlockSpec((8, W_QKV), lambda i: (jnp.minimum((i + 1) * (tr // 8), nblk8 - 1), 0)),
                  pl.BlockSpec((tr, W_AB), lambda i: (i, 0)),
                  pl.BlockSpec((3, W_QKV), lambda i: (0, 0)),
                  pl.BlockSpec((1, 128), lambda i: (0, 0)),
                  pl.BlockSpec((1, 128), lambda i: (0, 0))],
        out_specs=out_specs,
        out_shape=out_shape,
        compiler_params=_cparams(("parallel",)),
        name="gdn_prep",
    )(qkv, qkv, qkv, ab, conv_w, alog_row, dtb_row)


def _gdn_scan_body(*refs, nch):
    ins = refs[:12]
    s0_ref = refs[12]
    o_refs = refs[13:15]
    sfin_ref = refs[15]
    s_scr = refs[16]
    n = pl.program_id(1)
    c = GDN_CHUNK

    @pl.when(n == 0)
    def _():
        s_scr[...] = s0_ref[0]

    chains = [(d, h) for d in range(2) for h in range(GDN_HEADS)]
    state, prod, vnew = {}, {}, {}
    for d, h in chains:
        w_ref, qd_ref = ins[6 * d + 1], ins[6 * d + 3]
        sl = slice(h * GDN_DV, (h + 1) * GDN_DV)
        state[d, h] = s_scr[d, h]
        wq = jnp.concatenate([w_ref[:, sl], qd_ref[:, sl]], axis=0)
        prod[d, h] = _dot(wq, state[d, h].astype(BF16))
    for d, h in chains:
        u_ref, qk_ref = ins[6 * d], ins[6 * d + 4]
        sl = slice(h * GDN_DV, (h + 1) * GDN_DV)
        r = prod[d, h]
        vnew[d, h] = (u_ref[:, sl] - r[:c]).astype(BF16)
        o_refs[d][:, sl] = r[c:] + _dot(qk_ref[:, h * c:(h + 1) * c], vnew[d, h])
    for d, h in chains:
        kd_ref, eg_ref = ins[6 * d + 2], ins[6 * d + 5]
        sl = slice(h * GDN_DV, (h + 1) * GDN_DV)
        s_scr[d, h] = state[d, h] * eg_ref[0, h:h + 1, :] + _dot_tn(kd_ref[:, sl], vnew[d, h])

    @pl.when(n == nch - 1)
    def _():
        sfin_ref[0] = s_scr[...]


def _gdn_scan(prep, s0, batch, seq_len):
    c = GDN_CHUNK
    nch = seq_len // c
    nt = batch * seq_len
    fwd = lambda b, n: (b * nch + n, 0)
    bwd = lambda b, n: (b * nch + (nch - 1 - n), 0)
    in_specs = []
    for d, im in enumerate((fwd, bwd)):
        for wd in (GDN_W, GDN_W, GDN_W, GDN_W, GDN_HEADS * c):
            in_specs.append(pl.BlockSpec((c, wd), im))
        in_specs.append(pl.BlockSpec((1, 8, 128), (lambda im_: (lambda b, n: im_(b, n) + (0,)))(im)))
    st_spec = pl.BlockSpec((1, 2, GDN_HEADS, GDN_DK, GDN_DV), lambda b, n: (b, 0, 0, 0, 0))
    in_specs.append(st_spec)
    o_f, o_b, s_fin = pl.pallas_call(
        functools.partial(_gdn_scan_body, nch=nch),
        grid=(batch, nch),
        in_specs=in_specs,
        out_specs=[pl.BlockSpec((c, GDN_W), fwd), pl.BlockSpec((c, GDN_W), bwd), st_spec],
        out_shape=[jax.ShapeDtypeStruct((nt, GDN_W), F32), jax.ShapeDtypeStruct((nt, GDN_W), F32),
                   jax.ShapeDtypeStruct(s0.shape, F32)],
        scratch_shapes=[pltpu.VMEM((2, GDN_HEADS, GDN_DK, GDN_DV), F32)],
        compiler_params=_cparams(("arbitrary", "arbitrary")),
        name="gdn_scan",
    )(*prep, s0)
    return o_f, o_b, s_fin


def _group_norm_rope(x, w, bd, cos, sin):
    xx = x * x
    hi, lo = _split_bf16(xx)
    ss = _dot(hi, bd) + _dot(lo, bd)
    xn = x * lax.rsqrt(ss * (1.0 / ATT_HD) + NORM_EPS) * w
    if cos is not None:
        width = x.shape[1]
        lane = lax.broadcasted_iota(jnp.int32, x.shape, 1)
        from_below = pltpu.roll(xn, ROPE_PAIRS, 1)
        from_above = pltpu.roll(xn, width - ROPE_PAIRS, 1)
        partner = jnp.where((lane % (2 * ROPE_PAIRS)) < ROPE_PAIRS, from_above, from_below)
        xn = xn * cos + partner * sin
    return xn


def _attention_operands(q, kv, cos, sin, qw, kw, bd, qh_ref, kh_ref, vh_ref):
    nk = ATT_KV_HEADS * ATT_HD
    kcos = ksin = None
    if cos is not None:
        kcos, ksin = cos[:, :nk], sin[:, :nk]
    q = _group_norm_rope(q, qw, bd, cos, sin) * (ATT_HD ** -0.5 * math.log2(math.e))
    k = _group_norm_rope(kv[:, :nk], kw, bd[:nk, :nk], kcos, ksin)
    v = kv[:, nk:]
    for j in range(ATT_HEADS):
        qh_ref[j] = q[:, j * ATT_HD:(j + 1) * ATT_HD].astype(BF16)
    for j in range(ATT_KV_HEADS):
        kh_ref[j] = k[:, j * ATT_HD:(j + 1) * ATT_HD].astype(BF16)
        vh_ref[j] = v[:, j * ATT_HD:(j + 1) * ATT_HD].astype(BF16)


def _attention_body(q_ref, *refs, src_lens, tk, tq):
    n_src = len(src_lens)
    o_ref = refs[2 * n_src]
    rows = ATT_GROUP * tq
    q = q_ref[...].reshape(rows, ATT_HD)

    def block(sc, v, carry):
        m, l, acc = carry
        m_new = jnp.maximum(m, jnp.max(sc, axis=1, keepdims=True))
        alpha = jnp.exp2(m - m_new)
        p = jnp.exp2(sc - m_new)
        l = alpha * l + jnp.sum(p, axis=1, keepdims=True)
        acc = alpha * acc + _dot(p.astype(BF16), v)
        return m_new, l, acc

    carry = (jnp.full((rows, 1), -jnp.inf, F32), jnp.zeros((rows, 1), F32), jnp.zeros((rows, ATT_HD), F32))
    for s in range(n_src):
        k_ref, v_ref = refs[2 * s], refs[2 * s + 1]
        blk = min(tk, src_lens[s])
        nblk = src_lens[s] // blk
        for j in range(nblk):
            sc = _dot_nt(q, k_ref[0, j * blk:(j + 1) * blk, :])
            carry = block(sc, v_ref[0, j * blk:(j + 1) * blk, :], carry)
    m, l, acc = carry
    o = (acc / l).reshape(ATT_GROUP, tq, ATT_HD)
    o_ref[...] = jnp.concatenate([o[j] for j in range(ATT_GROUP)], axis=1)


def _attention(qh, sources, q_len):
    nq = qh.shape[1]
    tq = 128
    tk = 8192
    tpb = q_len // tq
    in_specs = [pl.BlockSpec((ATT_GROUP, tq, ATT_HD), lambda i, g: (g, i, 0))]
    args = [qh]
    for kh, vh, sl in sources:
        spec = pl.BlockSpec((1, sl, ATT_HD), lambda i, g: (g, i // tpb, 0))
        in_specs += [spec, spec]
        args += [kh, vh]
    return pl.pallas_call(
        functools.partial(_attention_body, src_lens=tuple(s[2] for s in sources), tk=tk, tq=tq),
        grid=(nq // tq, ATT_KV_HEADS),
        in_specs=in_specs,
        out_specs=pl.BlockSpec((tq, ATT_GROUP * ATT_HD), lambda i, g: (i, g)),
        out_shape=jax.ShapeDtypeStruct((nq, W_Q), F32),
        compiler_params=_cparams(("parallel", "parallel")),
        name="attention",
    )(*args)


def _merge_body(of_ref, ob_ref, z_ref, p_ref, pp_ref, pn_ref, oc_ref, gt_ref, h_ref, mod_ref,
                gnw_ref, cw_ref, wb_ref, wo_ref, o_ref, *, tps):
    i = pl.program_id(0)
    first = (i % tps) == 0
    last = (i % tps) == tps - 1
    o = of_ref[...] + ob_ref[...]
    z = z_ref[...]
    parts = []
    for h in range(GDN_HEADS):
        sl = slice(h * GDN_DV, (h + 1) * GDN_DV)
        oh = o[:, sl]
        oh = oh * lax.rsqrt(jnp.mean(oh * oh, axis=-1, keepdims=True) + NORM_EPS) * gnw_ref[...]
        zh = z[:, sl]
        parts.append(oh * (zh * jax.nn.sigmoid(zh)))
    br_a = jnp.concatenate(parts, axis=1)
    p = p_ref[...]
    tm = p.shape[0]
    bg = p[:, :SCONV_W]
    cx = p[:, SCONV_W:2 * SCONV_W] * p[:, 2 * SCONV_W:]
    pp = pp_ref[7:8, :]
    pn = pn_ref[0:1, :]
    prev_row = jnp.where(first, 0.0, pp[:, SCONV_W:2 * SCONV_W] * pp[:, 2 * SCONV_W:])
    next_row = jnp.where(last, 0.0, pn[:, SCONV_W:2 * SCONV_W] * pn[:, 2 * SCONV_W:])
    rows = lax.broadcasted_iota(jnp.int32, cx.shape, 0)
    cprev = jnp.where(rows == 0, prev_row, pltpu.roll(cx, 1, 0))
    cnext = jnp.where(rows == tm - 1, next_row, pltpu.roll(cx, tm - 1, 0))
    br_b = bg * (cw_ref[0:1, :] * cprev + cw_ref[1:2, :] * cx + cw_ref[2:3, :] * cnext)
    br_c = oc_ref[...]
    mixed = None
    for n, br in enumerate((br_a, br_b, br_c)):
        t = _dot(br.astype(BF16), wb_ref[n]) * jax.nn.sigmoid(gt_ref[:, n * D_MODEL:(n + 1) * D_MODEL])
        mixed = t if mixed is None else mixed + t
    y = _dot(mixed.astype(BF16), wo_ref[...])
    o_ref[...] = h_ref[...] + mod_ref[0, 2:3, :] * y


def _merge(o_f, o_b, z, psc, oc, gates, h, mod, mod_idx, gnw_row, conv_b_w, wb, wo, seq_len):
    nt, d = h.shape
    tm = 256
    tps = seq_len // tm
    nblk8 = nt // 8
    row = lambda i: (i, 0)
    return pl.pallas_call(
        functools.partial(_merge_body, tps=tps),
        grid=(nt // tm,),
        in_specs=[pl.BlockSpec((tm, GDN_W), row),
                  pl.BlockSpec((tm, GDN_W), row),
                  pl.BlockSpec((tm, W_Z), row),
                  pl.BlockSpec((tm, W_SC), row),
                  pl.BlockSpec((8, W_SC), lambda i: (jnp.maximum(i * (tm // 8) - 1, 0), 0)),
                  pl.BlockSpec((8, W_SC), lambda i: (jnp.minimum((i + 1) * (tm // 8), nblk8 - 1), 0)),
                  pl.BlockSpec((tm, W_Q), row),
                  pl.BlockSpec((tm, W_G), row),
                  pl.BlockSpec((tm, d), row),
                  pl.BlockSpec((1, 6, d), lambda i: (mod_idx(i, tm), 0, 0)),
                  pl.BlockSpec((1, GDN_DV), lambda i: (0, 0)),
                  pl.BlockSpec((3, SCONV_W), lambda i: (0, 0)),
                  pl.BlockSpec((3, SCONV_W, d), lambda i: (0, 0, 0)),
                  pl.BlockSpec((d, d), lambda i: (0, 0))],
        out_specs=pl.BlockSpec((tm, d), row),
        out_shape=jax.ShapeDtypeStruct((nt, d), F32),
        compiler_params=_cparams(("parallel",)),
        name="merge",
    )(o_f, o_b, z, psc, psc, psc, oc, gates, h, mod, gnw_row, conv_b_w, wb, wo)


PEER_NO_RANK = 64.0


def _oddeven_merge_sort_pairs(n):
    pairs = []
    p = 1
    while p < n:
        k = p
        while k >= 1:
            for j in range(k % p, n - k, 2 * k):
                for i in range(min(k, n - j - k)):
                    if (i + j) // (2 * p) == (i + j + k) // (2 * p):
                        pairs.append((i + j, i + j + k))
            k //= 2
        p *= 2
    return pairs


def _top_rows_sorted(s):
    cnt = PEER_TOPK
    v = [s[8 * i:8 * (i + 1), :] for i in range(s.shape[0] // 8)]
    for i, j in _oddeven_merge_sort_pairs(len(v)):
        v[i], v[j] = jnp.maximum(v[i], v[j]), jnp.minimum(v[i], v[j])
    for shift in (4, 2, 1):
        other = [pltpu.roll(x, shift, 0) for x in v]
        if len(v) == cnt:
            v = [jnp.maximum(v[k], other[cnt - 1 - k]) for k in range(cnt)]
        else:
            v = v + other[::-1]
        d = cnt // 2
        while d >= 1:
            for k in range(cnt):
                if k & d == 0:
                    v[k], v[k + d] = jnp.maximum(v[k], v[k + d]), jnp.minimum(v[k], v[k + d])
            d //= 2
    return [x[0:1, :] for x in v]


def _peer_body(h_ref, mod_ref, nw_ref, fw_ref, wq_ref, sk_ref, u_ref, vt_ref, o_ref,
               xt_scr, out_scr, sc_scr, row_scr, col_scr, cand_scr, gw_scr,
               *, tm, ec, nchunks, final_norm):
    c = pl.program_id(1)
    nk = PEER_NKEYS
    half = nk // 2
    tb_w = PEER_TB
    ntb = tm // tb_w
    nlb = tm // 128
    lb_per_tb = tb_w // 128

    @pl.when(c == 0)
    def _():
        u = _rms_mod(h_ref[...], nw_ref[...], mod_ref[0, 3:4, :], mod_ref[0, 4:5, :])
        ut = u.T
        uh = ut.astype(BF16)
        qt = _dot(wq_ref[...], uh)
        for tb in range(ntb):
            xt_scr[tb] = uh[:, tb * tb_w:(tb + 1) * tb_w]
            out_scr[tb] = jnp.zeros(out_scr.shape[1:], F32)
        for h in range(PEER_HEADS):
            for p in range(2):
                qs = qt[h * nk + p * half:h * nk + (p + 1) * half, :]
                s = _dot(sk_ref[h, p], qs.astype(BF16))
                for lb in range(nlb):
                    sc_scr[p, h * nlb + lb] = s[:, lb * 128:(lb + 1) * 128]

        def select_one(hl):
            s0 = sc_scr[0, hl]
            s1 = sc_scr[1, hl]
            top0 = _top_rows_sorted(s0)
            top1 = _top_rows_sorted(s1)
            rank1 = jnp.full(s1.shape, PEER_NO_RANK, F32)
            for l in reversed(range(PEER_TOPK)):
                rank1 = jnp.where(s1 >= top1[l], float(l + 1), rank1)
            pairs = [(k, l) for k in range(PEER_TOPK) for l in range(PEER_TOPK) if (k + 1) * (l + 1) <= PEER_TOPK]
            cand = cand_scr.at[hl % PEER_SELECT_GROUP]
            cand[...] = jnp.full(cand.shape, -jnp.inf, F32)
            sums = {}
            for r, (k, l) in enumerate(pairs):
                sums[k, l] = top0[k] + top1[l]
                cand[r:r + 1, :] = sums[k, l]
            best = _top_rows_sorted(cand[...])
            tau = best[PEER_TOPK - 1]
            zsum = jnp.zeros_like(tau)
            for b_ in best:
                zsum = zsum + jnp.exp(b_ - best[0])
            cnt = jnp.zeros(s0.shape, F32)
            for k in range(PEER_TOPK):
                n_sel = jnp.zeros_like(tau)
                for l in range(PEER_TOPK // (k + 1)):
                    n_sel = n_sel + jnp.where(sums[k, l] >= tau, 1.0, 0.0)
                cnt = jnp.where(s0 == top0[k], n_sel, cnt)
            row_scr[hl, :, 0] = cnt.reshape(nk // 8, 8, 128)
            row_scr[hl, :, 1] = (jnp.exp(s0 - top0[0]) / zsum).reshape(nk // 8, 8, 128)
            col_scr[hl, :, 0] = rank1.astype(BF16).reshape(nk // 16, 16, 128)
            col_scr[hl, :, 1] = jnp.exp(s1 - top1[0]).astype(BF16).reshape(nk // 16, 16, 128)

        def select(grp, carry):
            for k in range(PEER_SELECT_GROUP):
                select_one(PEER_SELECT_GROUP * grp + k)
            return carry

        lax.fori_loop(0, PEER_HEADS * nlb // PEER_SELECT_GROUP, select, 0)

    nblk = ec // nk
    piece = 16
    npiece = nk // piece

    def half_dots(lhs_ref, rhs):
        hm = lhs_ref.shape[0] // 2
        return [_dot(lhs_ref[0:hm, :], rhs), _dot(lhs_ref[hm:2 * hm, :], rhs)]

    def gated_activation(tb, acts):
        hm = ec // 2
        for sub in range(lb_per_tb):
            lb = tb * lb_per_tb + sub
            ls = slice(sub * 128, (sub + 1) * 128)
            for ii in range(nblk):
                wm = [None] * npiece
                for h in range(PEER_HEADS):
                    cnt_row = jnp.broadcast_to(row_scr[h * nlb + lb, c * (nblk // 8) + ii // 8, 0, ii % 8:ii % 8 + 1, :], (piece, 128)).astype(BF16)
                    ea_row = jnp.broadcast_to(row_scr[h * nlb + lb, c * (nblk // 8) + ii // 8, 1, ii % 8:ii % 8 + 1, :], (piece, 128)).astype(BF16)
                    for jp in range(npiece):
                        sel = col_scr[h * nlb + lb, jp, 0] <= cnt_row
                        w = col_scr[h * nlb + lb, jp, 1] * ea_row
                        w = jnp.where(sel, w, jnp.zeros_like(w))
                        wm[jp] = w if h == 0 else wm[jp] + w
                for jp in range(npiece):
                    r0 = ii * nk + jp * piece
                    ap = acts[r0 // hm][r0 % hm:r0 % hm + piece, ls]
                    gl = 0.5 * ap * (1.0 + lax.erf(ap * (0.5 ** 0.5)))
                    gw_scr[tb, r0:r0 + piece, ls] = gl.astype(BF16) * wm[jp]

    acts = [half_dots(u_ref, xt_scr[tb]) for tb in range(ntb)]
    for tb in range(ntb):
        gated_activation(tb, acts[tb])
        outs = half_dots(vt_ref.at[0], gw_scr[tb])
        hd = out_scr.shape[1] // 2
        out_scr[tb, 0:hd, :] += outs[0]
        out_scr[tb, hd:2 * hd, :] += outs[1]

    @pl.when(c == nchunks - 1)
    def _():
        out = jnp.concatenate([out_scr[tb] for tb in range(ntb)], axis=1)
        hn = h_ref[...] + mod_ref[0, 5:6, :] * out.T
        if final_norm:
            hn = hn * lax.rsqrt(jnp.mean(hn * hn, axis=-1, keepdims=True) + NORM_EPS) * fw_ref[...]
        o_ref[...] = hn


def _peer(h, mod, mod_idx, nw, final_w, apply_final, wq_t, sk, u_tab, vt_tab):
    nt, d = h.shape
    tm = PEER_TM
    ec = PEER_EC
    ne = u_tab.shape[0]
    nchunks = ne // ec
    nk = PEER_NKEYS
    ntb = tm // PEER_TB
    nlb = tm // 128
    const2 = lambda i, c: (0, 0)
    return pl.pallas_call(
        functools.partial(_peer_body, tm=tm, ec=ec, nchunks=nchunks, final_norm=apply_final),
        grid=(nt // tm, nchunks),
        in_specs=[pl.BlockSpec((tm, d), lambda i, c: (i, 0)),
                  pl.BlockSpec((1, 6, d), lambda i, c: (mod_idx(i, tm), 0, 0)),
                  pl.BlockSpec((1, d), const2),
                  pl.BlockSpec((1, d), const2),
                  pl.BlockSpec((d, d), const2),
                  pl.BlockSpec((PEER_HEADS, 2, nk, nk // 2), lambda i, c: (0, 0, 0, 0)),
                  pl.BlockSpec((ec, d), lambda i, c: (c, 0)),
                  pl.BlockSpec((1, d, ec), lambda i, c: (c, 0, 0))],
        out_specs=pl.BlockSpec((tm, d), lambda i, c: (i, 0)),
        out_shape=jax.ShapeDtypeStruct((nt, d), F32),
        scratch_shapes=[pltpu.VMEM((ntb, d, PEER_TB), BF16),
                        pltpu.VMEM((ntb, d, PEER_TB), F32),
                        pltpu.VMEM((2, PEER_HEADS * nlb, nk, 128), F32),
                        pltpu.VMEM((PEER_HEADS * nlb, nk // 8, 2, 8, 128), F32),
                        pltpu.VMEM((PEER_HEADS * nlb, nk // 16, 2, 16, 128), BF16),
                        pltpu.VMEM((PEER_SELECT_GROUP, 64, 128), F32),
                        pltpu.VMEM((ntb, ec, PEER_TB), BF16)],
        compiler_params=_cparams(("parallel", "arbitrary")),
        name="peer",
    )(h, mod, nw.reshape(1, d), final_w.reshape(1, d), wq_t, sk, u_tab, vt_tab)


def _rope_tables(seq):
    t = jnp.arange(seq)
    row = (t // GRID_W).astype(F32)
    col = (t % GRID_W).astype(F32)
    freqs = ROPE_THETA ** (-jnp.arange(ROPE_PAIRS, dtype=F32) / ROPE_PAIRS)
    ar = row[:, None] * freqs
    ac = col[:, None] * freqs
    cos = jnp.concatenate([jnp.cos(ar), jnp.cos(ar), jnp.cos(ac), jnp.cos(ac)], axis=1)
    sin = jnp.concatenate([-jnp.sin(ar), jnp.sin(ar), -jnp.sin(ac), jnp.sin(ac)], axis=1)
    return jnp.tile(cos, (1, ATT_HEADS)), jnp.tile(sin, (1, ATT_HEADS))


def _reorder_w_in(w):
    sizes = (W_QKV, W_Z, 4 * GDN_HEADS, W_SC, W_Q, W_KV, W_G)
    parts, start = [], 0
    for s in sizes:
        parts.append(w[:, start:start + s])
        start += s
    qkv, z, ab, sc, q, kv, g = parts
    ab = jnp.pad(ab, ((0, 0), (0, W_AB - ab.shape[1])))
    return jnp.concatenate([qkv, z, sc, q, kv, g, ab], axis=1).astype(BF16)


def _pad_row(v, n=128):
    v = v.reshape(1, -1)
    return jnp.pad(v, ((0, 0), (0, n - v.shape[1])))


def kernel(x, c, ctx, c_ctx, w_mod, b_mod, norm1_w, w_in, conv_a_w, a_log, dt_bias, gdn_norm_w, conv_b_w,
           q_norm_w, k_norm_w, w_branch, w_out, norm2_w, w_query, sub_keys, expert_u, expert_v, final_norm_w):
    batch, seq, d = x.shape
    ctx_len = ctx.shape[1]
    depth = w_mod.shape[0]
    h = x.reshape(batch * seq, d)
    hc = ctx.reshape(batch * ctx_len, d)

    mod_rows = -(-(batch + 1) // 8) * 8
    c_all = jnp.zeros((mod_rows, d), F32).at[:batch].set(c).at[batch].set(c_ctx)
    cos, sin = _rope_tables(seq)
    hd_i = jnp.arange(W_Q) // ATT_HD
    bd = (hd_i[:, None] == hd_i[None, :]).astype(BF16)
    lat_idx = lambda i, tm: i // (seq // tm)
    ctx_idx = lambda i, tm: batch

    for layer in range(depth):
        last = layer == depth - 1
        mod = _modulation(c_all, w_mod[layer], b_mod[layer]).reshape(mod_rows, 6, d)
        w_in_r = _reorder_w_in(w_in[layer])
        alog_row = _pad_row(a_log[layer])
        dtb_row = _pad_row(dt_bias[layer])
        qw_row = jnp.tile(q_norm_w[layer], ATT_HEADS).reshape(1, W_Q)
        kw_row = jnp.tile(k_norm_w[layer], ATT_KV_HEADS).reshape(1, ATT_KV_HEADS * ATT_HD)
        gnw_row = gdn_norm_w[layer].reshape(1, GDN_DV)
        wb = w_branch[layer].astype(BF16)
        wo = w_out[layer].astype(BF16)
        wq_t = w_query[layer].T.astype(BF16)
        sk = sub_keys[layer].astype(BF16)
        u_tab = expert_u[layer].astype(BF16)
        vt_tab = expert_v[layer].astype(BF16).reshape(-1, PEER_EC, d).transpose(0, 2, 1)

        qkv_l, z_l, sc_l, g_l, ab_l, qh_l, kh_l, vh_l = _inproj(h, mod, lat_idx, norm1_w[layer], w_in_r, qw_row, kw_row, bd,
                                                                cos, sin, seq)
        qkv_c, z_c, sc_c, g_c, ab_c, qh_c, kh_c, vh_c = _inproj(hc, mod, ctx_idx, norm1_w[layer], w_in_r, qw_row, kw_row, bd)

        prep_c = _gdn_prep(qkv_c, ab_c, conv_a_w[layer], alog_row, dtb_row, ctx_len)
        prep_l = _gdn_prep(qkv_l, ab_l, conv_a_w[layer], alog_row, dtb_row, seq)
        s_zero = jnp.zeros((batch, 2, GDN_HEADS, GDN_DK, GDN_DV), F32)
        ocf, ocb, s_ctx = _gdn_scan(prep_c, s_zero, batch, ctx_len)
        olf, olb, _ = _gdn_scan(prep_l, s_ctx, batch, seq)

        att_l = _attention(qh_l, [(kh_l, vh_l, seq), (kh_c, vh_c, ctx_len)], seq)

        h = _merge(olf, olb, z_l, sc_l, att_l, g_l, h, mod, lat_idx, gnw_row, conv_b_w[layer],
                   wb, wo, seq)
        h = _peer(h, mod, lat_idx, norm2_w[layer], final_norm_w, last, wq_t, sk, u_tab, vt_tab)
        if not last:
            att_c = _attention(qh_c, [(kh_c, vh_c, ctx_len)], ctx_len)
            hc = _merge(ocf, ocb, z_c, sc_c, att_c, g_c, hc, mod, ctx_idx, gnw_row, conv_b_w[layer],
                        wb, wo, ctx_len)
            hc = _peer(hc, mod, ctx_idx, norm2_w[layer], final_norm_w, False, wq_t, sk, u_tab, vt_tab)
    return h.reshape(batch, seq, d)
```

```python
import functools
import math

import jax
import jax.numpy as jnp
from jax import lax
from jax.experimental import pallas as pl
from jax.experimental.pallas import tpu as pltpu

F32 = jnp.float32
BF16 = jnp.bfloat16

D_MODEL = 1024
GRID_W = 64
GDN_HEADS = 4
GDN_DK = 128
GDN_DV = 128
GDN_CHUNK = 64
GDN_TILE = 256
GDN_W = GDN_HEADS * GDN_DV
SCONV_W = 512
ATT_HEADS = 8
ATT_KV_HEADS = 2
ATT_GROUP = 4
ATT_HD = 64
ROPE_THETA = 10000.0
ROPE_PAIRS = 16
PEER_HEADS = 8
PEER_NKEYS = 128
PEER_TOPK = 16
PEER_TB = 256
PEER_TM = 512
PEER_EC = 2048
PEER_SELECT_GROUP = 4
NORM_EPS = 1e-6

W_QKV = 3 * GDN_W
W_Z = GDN_W
W_SC = 3 * SCONV_W
W_Q = ATT_HEADS * ATT_HD
W_KV = 2 * ATT_KV_HEADS * ATT_HD
W_G = 3 * D_MODEL
W_AB = 128
INPROJ_WIDTHS = (W_QKV, W_Z, W_SC, W_Q, W_KV, W_G, W_AB)

VMEM_LIMIT = 56 * 1024 * 1024


def _cparams(sem):
    return pltpu.CompilerParams(dimension_semantics=sem, vmem_limit_bytes=VMEM_LIMIT)


def _split_bf16(x):
    hi = x.astype(BF16)
    lo = (x - hi.astype(F32)).astype(BF16)
    return hi, lo


def _dot(a, b):
    return jnp.dot(a, b, preferred_element_type=F32)


def _dot_nt(a, b):
    return lax.dot_general(a, b, (((1,), (1,)), ((), ())), preferred_element_type=F32)


def _dot_tn(a, b):
    return lax.dot_general(a, b, (((0,), (0,)), ((), ())), preferred_element_type=F32)


def _dot3(a, b):
    ah, al = _split_bf16(a)
    bh, bl = _split_bf16(b)
    return _dot(ah, bh) + _dot(ah, bl) + _dot(al, bh)


def _rms_mod(x, nw, shift, scale):
    ms = jnp.mean(x * x, axis=-1, keepdims=True)
    return (x * lax.rsqrt(ms + NORM_EPS) * nw) * (1.0 + scale) + shift


def _mod_body(c_ref, w_ref, b_ref, o_ref):
    c = c_ref[...]
    a = c * jax.nn.sigmoid(c)
    o_ref[...] = _dot3(a, w_ref[...]) + b_ref[...]


def _modulation(c_all, w_mod, b_mod):
    rows, d = c_all.shape
    n = w_mod.shape[1]
    tn = 1536
    return pl.pallas_call(
        _mod_body,
        grid=(n // tn,),
        in_specs=[pl.BlockSpec((rows, d), lambda j: (0, 0)),
                  pl.BlockSpec((d, tn), lambda j: (0, j)),
                  pl.BlockSpec((1, tn), lambda j: (0, j))],
        out_specs=pl.BlockSpec((rows, tn), lambda j: (0, j)),
        out_shape=jax.ShapeDtypeStruct((rows, n), F32),
        compiler_params=_cparams(("parallel",)),
        name="modulation",
    )(c_all, w_mod, b_mod.reshape(1, n))


def _inproj_body(*refs, rope):
    h_ref, mod_ref, nw_ref, w_ref = refs[:4]
    rest = refs[4:]
    cos = sin = None
    if rope:
        cos, sin = rest[0][...], rest[1][...]
        rest = rest[2:]
    qw_ref, kw_ref, bd_ref, qkv_ref, z_ref, sc_ref, g_ref, ab_ref, qh_ref, kh_ref, vh_ref = rest
    u = _rms_mod(h_ref[...], nw_ref[...], mod_ref[0, 0:1, :], mod_ref[0, 1:2, :])
    ub = u.astype(BF16)
    offs = [sum(INPROJ_WIDTHS[:i]) for i in range(len(INPROJ_WIDTHS))]
    proj = lambda i: _dot(ub, w_ref[:, offs[i]:offs[i] + INPROJ_WIDTHS[i]])
    _attention_operands(proj(3), proj(4), cos, sin, qw_ref[...], kw_ref[...], bd_ref[...], qh_ref, kh_ref, vh_ref)
    qkv_ref[...] = proj(0)
    z_ref[...] = proj(1)
    sc_ref[...] = proj(2)
    g_ref[...] = proj(5)
    ab_ref[...] = proj(6)


def _inproj(h, mod, mod_idx, nw, w, qw_row, kw_row, bd, cos=None, sin=None, seq_len=None):
    nt, d = h.shape
    tm = 256
    wtot = w.shape[1]
    rope = cos is not None
    row = lambda i: (i, 0)
    const = lambda i: (0, 0)
    in_specs = [pl.BlockSpec((tm, d), row),
                pl.BlockSpec((1, 6, d), lambda i: (mod_idx(i, tm), 0, 0)),
                pl.BlockSpec((1, d), const),
                pl.BlockSpec((d, wtot), const, pipeline_mode=pl.Buffered(1))]
    args = [h, mod, nw.reshape(1, d), w]
    if rope:
        tps = seq_len // tm
        in_specs += [pl.BlockSpec((tm, W_Q), lambda i: (i % tps, 0))] * 2
        args += [cos, sin]
    in_specs += [pl.BlockSpec((1, W_Q), const),
                 pl.BlockSpec((1, ATT_KV_HEADS * ATT_HD), const),
                 pl.BlockSpec((W_Q, W_Q), const)]
    args += [qw_row, kw_row, bd]
    plain = (W_QKV, W_Z, W_SC, W_G, W_AB)
    heads = (ATT_HEADS, ATT_KV_HEADS, ATT_KV_HEADS)
    return pl.pallas_call(
        functools.partial(_inproj_body, rope=rope),
        grid=(nt // tm,),
        in_specs=in_specs,
        out_specs=[pl.BlockSpec((tm, wd), row) for wd in plain]
        + [pl.BlockSpec((nh, tm, ATT_HD), lambda i: (0, i, 0)) for nh in heads],
        out_shape=[jax.ShapeDtypeStruct((nt, wd), F32) for wd in plain]
        + [jax.ShapeDtypeStruct((nh, nt, ATT_HD), BF16) for nh in heads],
        compiler_params=_cparams(("parallel",)),
        name="inproj",
    )(*args)


def _bdot(a, b):
    return lax.dot_general(a, b, (((2,), (1,)), ((0,), (0,))), preferred_element_type=F32)


def _tri_inverse_minus_eye(n):
    nb = n.astype(BF16)
    p = _bdot(nb, nb)
    y = -n
    for step in range(5):
        y = y + p + _bdot(y.astype(BF16), p.astype(BF16))
        if step < 4:
            pb = p.astype(BF16)
            p = _bdot(pb, pb)
    return y


def _gdn_prep_body(x_ref, xp_ref, xn_ref, ab_ref, cw_ref, alog_ref, dtb_ref, *outs, tps):
    i = pl.program_id(0)
    first = (i % tps) == 0
    last = (i % tps) == tps - 1
    x = x_ref[...]
    nrow = x.shape[0]
    c = GDN_CHUNK
    cpt = nrow // c
    rows = lax.broadcasted_iota(jnp.int32, x.shape, 0)
    prev_row = jnp.where(first, 0.0, xp_ref[7:8, :])
    next_row = jnp.where(last, 0.0, xn_ref[0:1, :])
    xprev = jnp.where(rows == 0, prev_row, pltpu.roll(x, 1, 0))
    xnext = jnp.where(rows == nrow - 1, next_row, pltpu.roll(x, nrow - 1, 0))
    y = cw_ref[0:1, :] * xprev + cw_ref[1:2, :] * x + cw_ref[2:3, :] * xnext
    y = y * jax.nn.sigmoid(y)

    ab = ab_ref[...]
    zz = ab + dtb_ref[...]
    sp = jnp.maximum(zz, 0.0) + jnp.log1p(jnp.exp(-jnp.abs(zz)))
    g = -jnp.exp(alog_ref[...]) * sp
    beta = jax.nn.sigmoid(ab)

    ri = lax.broadcasted_iota(jnp.int32, (nrow, nrow), 0)
    ci = lax.broadcasted_iota(jnp.int32, (nrow, nrow), 1)
    same = (ri // c) == (ci // c)
    lower = (same & (ri >= ci)).astype(BF16)
    ones_blk = same.astype(BF16)
    g1 = g.astype(BF16)
    r1 = g - g1.astype(F32)
    g2 = r1.astype(BF16)
    g3 = (r1 - g2.astype(F32)).astype(BF16)
    gc_f = _dot(lower, g1) + _dot(lower, g2) + _dot(lower, g3)
    tot = _dot(ones_blk, g1) + _dot(ones_blk, g2) + _dot(ones_blk, g3)
    gc_b = tot - gc_f + g
    gcf_t = gc_f.T
    gcb_t = gc_b.T
    egc_f = jnp.exp(gc_f)
    egc_b = jnp.exp(gc_b)
    kdec_f = jnp.exp(tot - gc_f)
    kdec_b = jnp.exp(tot - gc_b)
    etot = jnp.exp(tot)

    pi = lax.broadcasted_iota(jnp.int32, (2 * c, 2 * c), 0)
    pj = lax.broadcasted_iota(jnp.int32, (2 * c, 2 * c), 1)
    top = (pi < c) & (pj < c)
    bot = (pi >= c) & (pj >= c)
    incl = (top & (pi >= pj)) | (bot & (pi <= pj))
    strict = (top & (pi > pj)) | (bot & (pi < pj))

    for d in range(2):
        outs[6 * d + 5][...] = jnp.zeros(outs[6 * d + 5].shape, F32)

    n_list, rhs_list, qk_list = [], [], []
    for h in range(GDN_HEADS):
        q = y[:, h * GDN_DK:(h + 1) * GDN_DK]
        k = y[:, GDN_W + h * GDN_DK:GDN_W + (h + 1) * GDN_DK]
        v = y[:, 2 * GDN_W + h * GDN_DV:2 * GDN_W + (h + 1) * GDN_DV]
        q = q * lax.rsqrt(jnp.sum(q * q, axis=-1, keepdims=True) + NORM_EPS) * (GDN_DK ** -0.5)
        k = k * lax.rsqrt(jnp.sum(k * k, axis=-1, keepdims=True) + NORM_EPS)
        cf, cb = h, GDN_HEADS + h
        bf_, bb_ = 2 * GDN_HEADS + h, 3 * GDN_HEADS + h
        sl = slice(h * GDN_DV, (h + 1) * GDN_DV)
        outs[2][:, sl] = (k * kdec_f[:, cf:cf + 1]).astype(BF16)
        outs[8][:, sl] = (k * kdec_b[:, cb:cb + 1]).astype(BF16)
        outs[3][:, sl] = (q * egc_f[:, cf:cf + 1]).astype(BF16)
        outs[9][:, sl] = (q * egc_b[:, cb:cb + 1]).astype(BF16)
        for ch in range(cpt):
            r = slice(ch * c, (ch + 1) * c)
            k2 = jnp.concatenate([k[r], k[r]], axis=0)
            q2 = jnp.concatenate([q[r], q[r]], axis=0)
            v2 = jnp.concatenate([v[r], v[r]], axis=0)
            kb2 = k2.astype(BF16)
            kk2 = _dot_nt(kb2, kb2)
            qk2 = _dot_nt(q2.astype(BF16), kb2)
            gcol = jnp.concatenate([gc_f[r, cf:cf + 1], gc_b[r, cb:cb + 1]], axis=0)
            grow = jnp.concatenate([gcf_t[cf:cf + 1, r], gcb_t[cb:cb + 1, r]], axis=1)
            bcol = jnp.concatenate([beta[r, bf_:bf_ + 1], beta[r, bb_:bb_ + 1]], axis=0)
            ecol = jnp.concatenate([egc_f[r, cf:cf + 1], egc_b[r, cb:cb + 1]], axis=0)
            ldec = jnp.where(incl, jnp.exp(jnp.where(incl, gcol - grow, 0.0)), 0.0)
            n_list.append(jnp.where(strict, kk2 * bcol * ldec, 0.0))
            rhs_list.append(jnp.concatenate([v2 * bcol, k2 * (bcol * ecol)], axis=1))
            qk_list.append(qk2 * ldec)
            for d, col in ((0, cf), (1, cb)):
                outs[6 * d + 5][ch, h:h + 1, :] = jnp.broadcast_to(etot[ch * c:ch * c + 1, col:col + 1], (1, 128))

    yinv = _tri_inverse_minus_eye(jnp.stack(n_list, axis=0))
    rhs = jnp.stack(rhs_list, axis=0)
    sol = rhs + _bdot(yinv.astype(BF16), rhs.astype(BF16))
    for h in range(GDN_HEADS):
        sl = slice(h * GDN_DV, (h + 1) * GDN_DV)
        for ch in range(cpt):
            r = slice(ch * c, (ch + 1) * c)
            s = sol[h * cpt + ch]
            qk = qk_list[h * cpt + ch]
            outs[0][r, sl] = s[:c, :GDN_DV]
            outs[1][r, sl] = s[:c, GDN_DV:].astype(BF16)
            outs[6][r, sl] = s[c:, :GDN_DV]
            outs[7][r, sl] = s[c:, GDN_DV:].astype(BF16)
            outs[4][r, h * c:(h + 1) * c] = qk[:c, :c].astype(BF16)
            outs[10][r, h * c:(h + 1) * c] = qk[c:, c:].astype(BF16)


def _gdn_prep(qkv, ab, conv_w, alog_row, dtb_row, seq_len):
    nt = qkv.shape[0]
    c = GDN_CHUNK
    tr = GDN_TILE
    cpt = tr // c
    tps = seq_len // tr
    nblk8 = nt // 8
    out_shape, out_specs = [], []
    for _ in range(2):
        for wd, dt in ((GDN_W, F32), (GDN_W, BF16), (GDN_W, BF16), (GDN_W, BF16), (GDN_HEADS * c, BF16)):
            out_shape.append(jax.ShapeDtypeStruct((nt, wd), dt))
            out_specs.append(pl.BlockSpec((tr, wd), lambda i: (i, 0)))
        out_shape.append(jax.ShapeDtypeStruct((nt // c, 8, 128), F32))
        out_specs.append(pl.BlockSpec((cpt, 8, 128), lambda i: (i, 0, 0)))
    return pl.pallas_call(
        functools.partial(_gdn_prep_body, tps=tps),
        grid=(nt // tr,),
        in_specs=[pl.BlockSpec((tr, W_QKV), lambda i: (i, 0)),
                  pl.BlockSpec((8, W_QKV), lambda i: (jnp.maximum(i * (tr // 8) - 1, 0), 0)),
                  pl.BlockSpec((8, W_QKV), lambda i: (jnp.minimum((i + 1) * (tr // 8), nblk8 - 1), 0)),
                  pl.BlockSpec((tr, W_AB), lambda i: (i, 0)),
                  pl.BlockSpec((3, W_QKV), lambda i: (0, 0)),
                  pl.BlockSpec((1, 128), lambda i: (0, 0)),
                  pl.BlockSpec((1, 128), lambda i: (0, 0))],
        out_specs=out_specs,
        out_shape=out_shape,
        compiler_params=_cparams(("parallel",)),
        name="gdn_prep",
    )(qkv, qkv, qkv, ab, conv_w, alog_row, dtb_row)


def _gdn_scan_body(*refs, nch):
    ins = refs[:12]
    s0_ref = refs[12]
    o_refs = refs[13:15]
    sfin_ref = refs[15]
    s_scr = refs[16]
    n = pl.program_id(1)
    c = GDN_CHUNK

    @pl.when(n == 0)
    def _():
        s_scr[...] = s0_ref[0]

    chains = [(d, h) for d in range(2) for h in range(GDN_HEADS)]
    state, prod, vnew = {}, {}, {}
    for d, h in chains:
        w_ref, qd_ref = ins[6 * d + 1], ins[6 * d + 3]
        sl = slice(h * GDN_DV, (h + 1) * GDN_DV)
        state[d, h] = s_scr[d, h]
        wq = jnp.concatenate([w_ref[:, sl], qd_ref[:, sl]], axis=0)
        prod[d, h] = _dot(wq, state[d, h].astype(BF16))
    for d, h in chains:
        u_ref, qk_ref = ins[6 * d], ins[6 * d + 4]
        sl = slice(h * GDN_DV, (h + 1) * GDN_DV)
        r = prod[d, h]
        vnew[d, h] = (u_ref[:, sl] - r[:c]).astype(BF16)
        o_refs[d][:, sl] = r[c:] + _dot(qk_ref[:, h * c:(h + 1) * c], vnew[d, h])
    for d, h in chains:
        kd_ref, eg_ref = ins[6 * d + 2], ins[6 * d + 5]
        sl = slice(h * GDN_DV, (h + 1) * GDN_DV)
        s_scr[d, h] = state[d, h] * eg_ref[0, h:h + 1, :] + _dot_tn(kd_ref[:, sl], vnew[d, h])

    @pl.when(n == nch - 1)
    def _():
        sfin_ref[0] = s_scr[...]


def _gdn_scan(prep, s0, batch, seq_len):
    c = GDN_CHUNK
    nch = seq_len // c
    nt = batch * seq_len
    fwd = lambda b, n: (b * nch + n, 0)
    bwd = lambda b, n: (b * nch + (nch - 1 - n), 0)
    in_specs = []
    for d, im in enumerate((fwd, bwd)):
        for wd in (GDN_W, GDN_W, GDN_W, GDN_W, GDN_HEADS * c):
            in_specs.append(pl.BlockSpec((c, wd), im))
        in_specs.append(pl.BlockSpec((1, 8, 128), (lambda im_: (lambda b, n: im_(b, n) + (0,)))(im)))
    st_spec = pl.BlockSpec((1, 2, GDN_HEADS, GDN_DK, GDN_DV), lambda b, n: (b, 0, 0, 0, 0))
    in_specs.append(st_spec)
    o_f, o_b, s_fin = pl.pallas_call(
        functools.partial(_gdn_scan_body, nch=nch),
        grid=(batch, nch),
        in_specs=in_specs,
        out_specs=[pl.BlockSpec((c, GDN_W), fwd), pl.BlockSpec((c, GDN_W), bwd), st_spec],
        out_shape=[jax.ShapeDtypeStruct((nt, GDN_W), F32), jax.ShapeDtypeStruct((nt, GDN_W), F32),
                   jax.ShapeDtypeStruct(s0.shape, F32)],
        scratch_shapes=[pltpu.VMEM((2, GDN_HEADS, GDN_DK, GDN_DV), F32)],
        compiler_params=_cparams(("arbitrary", "arbitrary")),
        name="gdn_scan",
    )(*prep, s0)
    return o_f, o_b, s_fin


def _group_norm_rope(x, w, bd, cos, sin):
    xx = x * x
    hi, lo = _split_bf16(xx)
    ss = _dot(hi, bd) + _dot(lo, bd)
    xn = x * lax.rsqrt(ss * (1.0 / ATT_HD) + NORM_EPS) * w
    if cos is not None:
        width = x.shape[1]
        lane = lax.broadcasted_iota(jnp.int32, x.shape, 1)
        from_below = pltpu.roll(xn, ROPE_PAIRS, 1)
        from_above = pltpu.roll(xn, width - ROPE_PAIRS, 1)
        partner = jnp.where((lane % (2 * ROPE_PAIRS)) < ROPE_PAIRS, from_above, from_below)
        xn = xn * cos + partner * sin
    return xn


def _attention_operands(q, kv, cos, sin, qw, kw, bd, qh_ref, kh_ref, vh_ref):
    nk = ATT_KV_HEADS * ATT_HD
    kcos = ksin = None
    if cos is not None:
        kcos, ksin = cos[:, :nk], sin[:, :nk]
    q = _group_norm_rope(q, qw, bd, cos, sin) * (ATT_HD ** -0.5 * math.log2(math.e))
    k = _group_norm_rope(kv[:, :nk], kw, bd[:nk, :nk], kcos, ksin)
    v = kv[:, nk:]
    for j in range(ATT_HEADS):
        qh_ref[j] = q[:, j * ATT_HD:(j + 1) * ATT_HD].astype(BF16)
    for j in range(ATT_KV_HEADS):
        kh_ref[j] = k[:, j * ATT_HD:(j + 1) * ATT_HD].astype(BF16)
        vh_ref[j] = v[:, j * ATT_HD:(j + 1) * ATT_HD].astype(BF16)


def _attention_body(q_ref, *refs, src_lens, tk, tq):
    n_src = len(src_lens)
    o_ref = refs[2 * n_src]
    rows = ATT_GROUP * tq
    q = q_ref[...].reshape(rows, ATT_HD)

    def block(sc, v, carry):
        m, l, acc = carry
        m_new = jnp.maximum(m, jnp.max(sc, axis=1, keepdims=True))
        alpha = jnp.exp2(m - m_new)
        p = jnp.exp2(sc - m_new)
        l = alpha * l + jnp.sum(p, axis=1, keepdims=True)
        acc = alpha * acc + _dot(p.astype(BF16), v)
        return m_new, l, acc

    carry = (jnp.full((rows, 1), -jnp.inf, F32), jnp.zeros((rows, 1), F32), jnp.zeros((rows, ATT_HD), F32))
    for s in range(n_src):
        k_ref, v_ref = refs[2 * s], refs[2 * s + 1]
        blk = min(tk, src_lens[s])
        nblk = src_lens[s] // blk
        for j in range(nblk):
            sc = _dot_nt(q, k_ref[0, j * blk:(j + 1) * blk, :])
            carry = block(sc, v_ref[0, j * blk:(j + 1) * blk, :], carry)
    m, l, acc = carry
    o = (acc / l).reshape(ATT_GROUP, tq, ATT_HD)
    o_ref[...] = jnp.concatenate([o[j] for j in range(ATT_GROUP)], axis=1)


def _attention(qh, sources, q_len):
    nq = qh.shape[1]
    tq = 128
    tk = 8192
    tpb = q_len // tq
    in_specs = [pl.BlockSpec((ATT_GROUP, tq, ATT_HD), lambda i, g: (g, i, 0))]
    args = [qh]
    for kh, vh, sl in sources:
        spec = pl.BlockSpec((1, sl, ATT_HD), lambda i, g: (g, i // tpb, 0))
        in_specs += [spec, spec]
        args += [kh, vh]
    return pl.pallas_call(
        functools.partial(_attention_body, src_lens=tuple(s[2] for s in sources), tk=tk, tq=tq),
        grid=(nq // tq, ATT_KV_HEADS),
        in_specs=in_specs,
        out_specs=pl.BlockSpec((tq, ATT_GROUP * ATT_HD), lambda i, g: (i, g)),
        out_shape=jax.ShapeDtypeStruct((nq, W_Q), F32),
        compiler_params=_cparams(("parallel", "parallel")),
        name="attention",
    )(*args)


def _merge_body(of_ref, ob_ref, z_ref, p_ref, pp_ref, pn_ref, oc_ref, gt_ref, h_ref, mod_ref,
                gnw_ref, cw_ref, wb_ref, wo_ref, o_ref, *, tps):
    i = pl.program_id(0)
    first = (i % tps) == 0
    last = (i % tps) == tps - 1
    o = of_ref[...] + ob_ref[...]
    z = z_ref[...]
    parts = []
    for h in range(GDN_HEADS):
        sl = slice(h * GDN_DV, (h + 1) * GDN_DV)
        oh = o[:, sl]
        oh = oh * lax.rsqrt(jnp.mean(oh * oh, axis=-1, keepdims=True) + NORM_EPS) * gnw_ref[...]
        zh = z[:, sl]
        parts.append(oh * (zh * jax.nn.sigmoid(zh)))
    br_a = jnp.concatenate(parts, axis=1)
    p = p_ref[...]
    tm = p.shape[0]
    bg = p[:, :SCONV_W]
    cx = p[:, SCONV_W:2 * SCONV_W] * p[:, 2 * SCONV_W:]
    pp = pp_ref[7:8, :]
    pn = pn_ref[0:1, :]
    prev_row = jnp.where(first, 0.0, pp[:, SCONV_W:2 * SCONV_W] * pp[:, 2 * SCONV_W:])
    next_row = jnp.where(last, 0.0, pn[:, SCONV_W:2 * SCONV_W] * pn[:, 2 * SCONV_W:])
    rows = lax.broadcasted_iota(jnp.int32, cx.shape, 0)
    cprev = jnp.where(rows == 0, prev_row, pltpu.roll(cx, 1, 0))
    cnext = jnp.where(rows == tm - 1, next_row, pltpu.roll(cx, tm - 1, 0))
    br_b = bg * (cw_ref[0:1, :] * cprev + cw_ref[1:2, :] * cx + cw_ref[2:3, :] * cnext)
    br_c = oc_ref[...]
    mixed = None
    for n, br in enumerate((br_a, br_b, br_c)):
        t = _dot(br.astype(BF16), wb_ref[n]) * jax.nn.sigmoid(gt_ref[:, n * D_MODEL:(n + 1) * D_MODEL])
        mixed = t if mixed is None else mixed + t
    y = _dot(mixed.astype(BF16), wo_ref[...])
    o_ref[...] = h_ref[...] + mod_ref[0, 2:3, :] * y


def _merge(o_f, o_b, z, psc, oc, gates, h, mod, mod_idx, gnw_row, conv_b_w, wb, wo, seq_len):
    nt, d = h.shape
    tm = 256
    tps = seq_len // tm
    nblk8 = nt // 8
    row = lambda i: (i, 0)
    return pl.pallas_call(
        functools.partial(_merge_body, tps=tps),
        grid=(nt // tm,),
        in_specs=[pl.BlockSpec((tm, GDN_W), row),
                  pl.BlockSpec((tm, GDN_W), row),
                  pl.BlockSpec((tm, W_Z), row),
                  pl.BlockSpec((tm, W_SC), row),
                  pl.BlockSpec((8, W_SC), lambda i: (jnp.maximum(i * (tm // 8) - 1, 0), 0)),
                  pl.BlockSpec((8, W_SC), lambda i: (jnp.minimum((i + 1) * (tm // 8), nblk8 - 1), 0)),
                  pl.BlockSpec((tm, W_Q), row),
                  pl.BlockSpec((tm, W_G), row),
                  pl.BlockSpec((tm, d), row),
                  pl.BlockSpec((1, 6, d), lambda i: (mod_idx(i, tm), 0, 0)),
                  pl.BlockSpec((1, GDN_DV), lambda i: (0, 0)),
                  pl.BlockSpec((3, SCONV_W), lambda i: (0, 0)),
                  pl.BlockSpec((3, SCONV_W, d), lambda i: (0, 0, 0)),
                  pl.BlockSpec((d, d), lambda i: (0, 0))],
        out_specs=pl.BlockSpec((tm, d), row),
        out_shape=jax.ShapeDtypeStruct((nt, d), F32),
        compiler_params=_cparams(("parallel",)),
        name="merge",
    )(o_f, o_b, z, psc, psc, psc, oc, gates, h, mod, gnw_row, conv_b_w, wb, wo)


PEER_NO_RANK = 64.0


def _oddeven_merge_sort_pairs(n):
    pairs = []
    p = 1
    while p < n:
        k = p
        while k >= 1:
            for j in range(k % p, n - k, 2 * k):
                for i in range(min(k, n - j - k)):
                    if (i + j) // (2 * p) == (i + j + k) // (2 * p):
                        pairs.append((i + j, i + j + k))
            k //= 2
        p *= 2
    return pairs


def _top_rows_sorted(s):
    cnt = PEER_TOPK
    v = [s[8 * i:8 * (i + 1), :] for i in range(s.shape[0] // 8)]
    for i, j in _oddeven_merge_sort_pairs(len(v)):
        v[i], v[j] = jnp.maximum(v[i], v[j]), jnp.minimum(v[i], v[j])
    for shift in (4, 2, 1):
        other = [pltpu.roll(x, shift, 0) for x in v]
        if len(v) == cnt:
            v = [jnp.maximum(v[k], other[cnt - 1 - k]) for k in range(cnt)]
        else:
            v = v + other[::-1]
        d = cnt // 2
        while d >= 1:
            for k in range(cnt):
                if k & d == 0:
                    v[k], v[k + d] = jnp.maximum(v[k], v[k + d]), jnp.minimum(v[k], v[k + d])
            d //= 2
    return [x[0:1, :] for x in v]


def _peer_body(h_ref, mod_ref, nw_ref, fw_ref, wq_ref, sk_ref, u_ref, vt_ref, o_ref,
               xt_scr, out_scr, sc_scr, row_scr, col_scr, cand_scr, gw_scr,
               *, tm, ec, nchunks, final_norm):
    c = pl.program_id(1)
    nk = PEER_NKEYS
    half = nk // 2
    tb_w = PEER_TB
    ntb = tm // tb_w
    nlb = tm // 128
    lb_per_tb = tb_w // 128

    @pl.when(c == 0)
    def _():
        u = _rms_mod(h_ref[...], nw_ref[...], mod_ref[0, 3:4, :], mod_ref[0, 4:5, :])
        ut = u.T
        uh = ut.astype(BF16)
        qt = _dot(wq_ref[...], uh)
        for tb in range(ntb):
            xt_scr[tb] = uh[:, tb * tb_w:(tb + 1) * tb_w]
            out_scr[tb] = jnp.zeros(out_scr.shape[1:], F32)
        for h in range(PEER_HEADS):
            for p in range(2):
                qs = qt[h * nk + p * half:h * nk + (p + 1) * half, :]
                s = _dot(sk_ref[h, p], qs.astype(BF16))
                for lb in range(nlb):
                    sc_scr[p, h * nlb + lb] = s[:, lb * 128:(lb + 1) * 128]

        def select_one(hl):
            s0 = sc_scr[0, hl]
            s1 = sc_scr[1, hl]
            top0 = _top_rows_sorted(s0)
            top1 = _top_rows_sorted(s1)
            rank1 = jnp.full(s1.shape, PEER_NO_RANK, F32)
            for l in reversed(range(PEER_TOPK)):
                rank1 = jnp.where(s1 >= top1[l], float(l + 1), rank1)
            pairs = [(k, l) for k in range(PEER_TOPK) for l in range(PEER_TOPK) if (k + 1) * (l + 1) <= PEER_TOPK]
            cand = cand_scr.at[hl % PEER_SELECT_GROUP]
            cand[...] = jnp.full(cand.shape, -jnp.inf, F32)
            sums = {}
            for r, (k, l) in enumerate(pairs):
                sums[k, l] = top0[k] + top1[l]
                cand[r:r + 1, :] = sums[k, l]
            best = _top_rows_sorted(cand[...])
            tau = best[PEER_TOPK - 1]
            zsum = jnp.zeros_like(tau)
            for b_ in best:
                zsum = zsum + jnp.exp(b_ - best[0])
            cnt = jnp.zeros(s0.shape, F32)
            for k in range(PEER_TOPK):
                n_sel = jnp.zeros_like(tau)
                for l in range(PEER_TOPK // (k + 1)):
                    n_sel = n_sel + jnp.where(sums[k, l] >= tau, 1.0, 0.0)
                cnt = jnp.where(s0 == top0[k], n_sel, cnt)
            hh = hl // nlb
            ll = hl % nlb
            row_scr[ll, :, hh, 0] = cnt.reshape(nk // 8, 8, 128)
            row_scr[ll, :, hh, 1] = (jnp.exp(s0 - top0[0]) / zsum).reshape(nk // 8, 8, 128)
            col_scr[ll, :, hh, 0] = rank1.astype(BF16).reshape(nk // 16, 16, 128)
            col_scr[ll, :, hh, 1] = jnp.exp(s1 - top1[0]).astype(BF16).reshape(nk // 16, 16, 128)

        def select(grp, carry):
            for k in range(PEER_SELECT_GROUP):
                select_one(PEER_SELECT_GROUP * grp + k)
            return carry

        lax.fori_loop(0, PEER_HEADS * nlb // PEER_SELECT_GROUP, select, 0)

    nblk = ec // nk
    piece = 16
    npiece = nk // piece

    def half_dots(lhs_ref, rhs):
        hm = lhs_ref.shape[0] // 2
        return [_dot(lhs_ref[0:hm, :], rhs), _dot(lhs_ref[hm:2 * hm, :], rhs)]

    def gated_activation(tb, acts):
        hm = ec // 2
        for sub in range(lb_per_tb):
            lb = tb * lb_per_tb + sub
            ls = slice(sub * 128, (sub + 1) * 128)
            for ii in range(nblk):
                wm = [None] * npiece
                for h in range(PEER_HEADS):
                    cnt_row = jnp.broadcast_to(row_scr[lb, c * (nblk // 8) + ii // 8, h, 0, ii % 8:ii % 8 + 1, :], (piece, 128)).astype(BF16)
                    ea_row = jnp.broadcast_to(row_scr[lb, c * (nblk // 8) + ii // 8, h, 1, ii % 8:ii % 8 + 1, :], (piece, 128)).astype(BF16)
                    for jp in range(npiece):
                        sel = col_scr[lb, jp, h, 0] <= cnt_row
                        w = col_scr[lb, jp, h, 1] * ea_row
                        w = jnp.where(sel, w, jnp.zeros_like(w))
                        wm[jp] = w if h == 0 else wm[jp] + w
                for jp in range(npiece):
                    r0 = ii * nk + jp * piece
                    ap = acts[r0 // hm][r0 % hm:r0 % hm + piece, ls]
                    gl = 0.5 * ap * (1.0 + lax.erf(ap * (0.5 ** 0.5)))
                    gw_scr[tb, r0:r0 + piece, ls] = gl.astype(BF16) * wm[jp]

    acts = [half_dots(u_ref, xt_scr[tb]) for tb in range(ntb)]
    for tb in range(ntb):
        gated_activation(tb, acts[tb])
        outs = half_dots(vt_ref.at[0], gw_scr[tb])
        hd = out_scr.shape[1] // 2
        out_scr[tb, 0:hd, :] += outs[0]
        out_scr[tb, hd:2 * hd, :] += outs[1]

    @pl.when(c == nchunks - 1)
    def _():
        out = jnp.concatenate([out_scr[tb] for tb in range(ntb)], axis=1)
        hn = h_ref[...] + mod_ref[0, 5:6, :] * out.T
        if final_norm:
            hn = hn * lax.rsqrt(jnp.mean(hn * hn, axis=-1, keepdims=True) + NORM_EPS) * fw_ref[...]
        o_ref[...] = hn


def _peer(h, mod, mod_idx, nw, final_w, apply_final, wq_t, sk, u_tab, vt_tab):
    nt, d = h.shape
    tm = PEER_TM
    ec = PEER_EC
    ne = u_tab.shape[0]
    nchunks = ne // ec
    nk = PEER_NKEYS
    ntb = tm // PEER_TB
    nlb = tm // 128
    const2 = lambda i, c: (0, 0)
    return pl.pallas_call(
        functools.partial(_peer_body, tm=tm, ec=ec, nchunks=nchunks, final_norm=apply_final),
        grid=(nt // tm, nchunks),
        in_specs=[pl.BlockSpec((tm, d), lambda i, c: (i, 0)),
                  pl.BlockSpec((1, 6, d), lambda i, c: (mod_idx(i, tm), 0, 0)),
                  pl.BlockSpec((1, d), const2),
                  pl.BlockSpec((1, d), const2),
                  pl.BlockSpec((d, d), const2),
                  pl.BlockSpec((PEER_HEADS, 2, nk, nk // 2), lambda i, c: (0, 0, 0, 0)),
                  pl.BlockSpec((ec, d), lambda i, c: (c, 0)),
                  pl.BlockSpec((1, d, ec), lambda i, c: (c, 0, 0))],
        out_specs=pl.BlockSpec((tm, d), lambda i, c: (i, 0)),
        out_shape=jax.ShapeDtypeStruct((nt, d), F32),
        scratch_shapes=[pltpu.VMEM((ntb, d, PEER_TB), BF16),
                        pltpu.VMEM((ntb, d, PEER_TB), F32),
                        pltpu.VMEM((2, PEER_HEADS * nlb, nk, 128), F32),
                        pltpu.VMEM((nlb, nk // 8, PEER_HEADS, 2, 8, 128), F32),
                        pltpu.VMEM((nlb, nk // 16, PEER_HEADS, 2, 16, 128), BF16),
                        pltpu.VMEM((PEER_SELECT_GROUP, 64, 128), F32),
                        pltpu.VMEM((ntb, ec, PEER_TB), BF16)],
        compiler_params=_cparams(("parallel", "arbitrary")),
        name="peer",
    )(h, mod, nw.reshape(1, d), final_w.reshape(1, d), wq_t, sk, u_tab, vt_tab)


def _rope_tables(seq):
    t = jnp.arange(seq)
    row = (t // GRID_W).astype(F32)
    col = (t % GRID_W).astype(F32)
    freqs = ROPE_THETA ** (-jnp.arange(ROPE_PAIRS, dtype=F32) / ROPE_PAIRS)
    ar = row[:, None] * freqs
    ac = col[:, None] * freqs
    cos = jnp.concatenate([jnp.cos(ar), jnp.cos(ar), jnp.cos(ac), jnp.cos(ac)], axis=1)
    sin = jnp.concatenate([-jnp.sin(ar), jnp.sin(ar), -jnp.sin(ac), jnp.sin(ac)], axis=1)
    return jnp.tile(cos, (1, ATT_HEADS)), jnp.tile(sin, (1, ATT_HEADS))


def _reorder_w_in(w):
    sizes = (W_QKV, W_Z, 4 * GDN_HEADS, W_SC, W_Q, W_KV, W_G)
    parts, start = [], 0
    for s in sizes:
        parts.append(w[:, start:start + s])
        start += s
    qkv, z, ab, sc, q, kv, g = parts
    ab = jnp.pad(ab, ((0, 0), (0, W_AB - ab.shape[1])))
    return jnp.concatenate([qkv, z, sc, q, kv, g, ab], axis=1).astype(BF16)


def _pad_row(v, n=128):
    v = v.reshape(1, -1)
    return jnp.pad(v, ((0, 0), (0, n - v.shape[1])))


def kernel(x, c, ctx, c_ctx, w_mod, b_mod, norm1_w, w_in, conv_a_w, a_log, dt_bias, gdn_norm_w, conv_b_w,
           q_norm_w, k_norm_w, w_branch, w_out, norm2_w, w_query, sub_keys, expert_u, expert_v, final_norm_w):
    batch, seq, d = x.shape
    ctx_len = ctx.shape[1]
    depth = w_mod.shape[0]
    h = x.reshape(batch * seq, d)
    hc = ctx.reshape(batch * ctx_len, d)

    mod_rows = -(-(batch + 1) // 8) * 8
    c_all = jnp.zeros((mod_rows, d), F32).at[:batch].set(c).at[batch].set(c_ctx)
    cos, sin = _rope_tables(seq)
    hd_i = jnp.arange(W_Q) // ATT_HD
    bd = (hd_i[:, None] == hd_i[None, :]).astype(BF16)
    lat_idx = lambda i, tm: i // (seq // tm)
    ctx_idx = lambda i, tm: batch

    for layer in range(depth):
        last = layer == depth - 1
        mod = _modulation(c_all, w_mod[layer], b_mod[layer]).reshape(mod_rows, 6, d)
        w_in_r = _reorder_w_in(w_in[layer])
        alog_row = _pad_row(a_log[layer])
        dtb_row = _pad_row(dt_bias[layer])
        qw_row = jnp.tile(q_norm_w[layer], ATT_HEADS).reshape(1, W_Q)
        kw_row = jnp.tile(k_norm_w[layer], ATT_KV_HEADS).reshape(1, ATT_KV_HEADS * ATT_HD)
        gnw_row = gdn_norm_w[layer].reshape(1, GDN_DV)
        wb = w_branch[layer].astype(BF16)
        wo = w_out[layer].astype(BF16)
        wq_t = w_query[layer].T.astype(BF16)
        sk = sub_keys[layer].astype(BF16)
        u_tab = expert_u[layer].astype(BF16)
        vt_tab = expert_v[layer].astype(BF16).reshape(-1, PEER_EC, d).transpose(0, 2, 1)

        qkv_l, z_l, sc_l, g_l, ab_l, qh_l, kh_l, vh_l = _inproj(h, mod, lat_idx, norm1_w[layer], w_in_r, qw_row, kw_row, bd,
                                                                cos, sin, seq)
        qkv_c, z_c, sc_c, g_c, ab_c, qh_c, kh_c, vh_c = _inproj(hc, mod, ctx_idx, norm1_w[layer], w_in_r, qw_row, kw_row, bd)

        prep_c = _gdn_prep(qkv_c, ab_c, conv_a_w[layer], alog_row, dtb_row, ctx_len)
        prep_l = _gdn_prep(qkv_l, ab_l, conv_a_w[layer], alog_row, dtb_row, seq)
        s_zero = jnp.zeros((batch, 2, GDN_HEADS, GDN_DK, GDN_DV), F32)
        ocf, ocb, s_ctx = _gdn_scan(prep_c, s_zero, batch, ctx_len)
        olf, olb, _ = _gdn_scan(prep_l, s_ctx, batch, seq)

        att_l = _attention(qh_l, [(kh_l, vh_l, seq), (kh_c, vh_c, ctx_len)], seq)

        h = _merge(olf, olb, z_l, sc_l, att_l, g_l, h, mod, lat_idx, gnw_row, conv_b_w[layer],
                   wb, wo, seq)
        h = _peer(h, mod, lat_idx, norm2_w[layer], final_norm_w, last, wq_t, sk, u_tab, vt_tab)
        if not last:
            att_c = _attention(qh_c, [(kh_c, vh_c, ctx_len)], ctx_len)
            hc = _merge(ocf, ocb, z_c, sc_c, att_c, g_c, hc, mod, ctx_idx, gnw_row, conv_b_w[layer],
                        wb, wo, ctx_len)
            hc = _peer(hc, mod, ctx_idx, norm2_w[layer], final_norm_w, False, wq_t, sk, u_tab, vt_tab)
    return h.reshape(batch, seq, d)
```

```python
import functools
import math

import jax
import jax.numpy as jnp
from jax import lax
from jax.experimental import pallas as pl
from jax.experimental.pallas import tpu as pltpu

F32 = jnp.float32
BF16 = jnp.bfloat16

D_MODEL = 1024
GRID_W = 64
GDN_HEADS = 4
GDN_DK = 128
GDN_DV = 128
GDN_CHUNK = 64
GDN_TILE = 256
GDN_W = GDN_HEADS * GDN_DV
SCONV_W = 512
ATT_HEADS = 8
ATT_KV_HEADS = 2
ATT_GROUP = 4
ATT_HD = 64
ROPE_THETA = 10000.0
ROPE_PAIRS = 16
PEER_HEADS = 8
PEER_NKEYS = 128
PEER_TOPK = 16
PEER_TB = 256
PEER_TM = 512
PEER_EC = 2048
PEER_SELECT_GROUP = 4
NORM_EPS = 1e-6

W_QKV = 3 * GDN_W
W_Z = GDN_W
W_SC = 3 * SCONV_W
W_Q = ATT_HEADS * ATT_HD
W_KV = 2 * ATT_KV_HEADS * ATT_HD
W_G = 3 * D_MODEL
W_AB = 128
INPROJ_WIDTHS = (W_QKV, W_Z, W_SC, W_Q, W_KV, W_G, W_AB)

VMEM_LIMIT = 56 * 1024 * 1024


def _cparams(sem):
    return pltpu.CompilerParams(dimension_semantics=sem, vmem_limit_bytes=VMEM_LIMIT)


def _split_bf16(x):
    hi = x.astype(BF16)
    lo = (x - hi.astype(F32)).astype(BF16)
    return hi, lo


def _dot(a, b):
    return jnp.dot(a, b, preferred_element_type=F32)


def _dot_nt(a, b):
    return lax.dot_general(a, b, (((1,), (1,)), ((), ())), preferred_element_type=F32)


def _dot_tn(a, b):
    return lax.dot_general(a, b, (((0,), (0,)), ((), ())), preferred_element_type=F32)


def _dot3(a, b):
    ah, al = _split_bf16(a)
    bh, bl = _split_bf16(b)
    return _dot(ah, bh) + _dot(ah, bl) + _dot(al, bh)


def _rms_mod(x, nw, shift, scale):
    ms = jnp.mean(x * x, axis=-1, keepdims=True)
    return (x * lax.rsqrt(ms + NORM_EPS) * nw) * (1.0 + scale) + shift


def _mod_body(c_ref, w_ref, b_ref, o_ref):
    c = c_ref[...]
    a = c * jax.nn.sigmoid(c)
    o_ref[...] = _dot3(a, w_ref[...]) + b_ref[...]


def _modulation(c_all, w_mod, b_mod):
    rows, d = c_all.shape
    n = w_mod.shape[1]
    tn = 1536
    return pl.pallas_call(
        _mod_body,
        grid=(n // tn,),
        in_specs=[pl.BlockSpec((rows, d), lambda j: (0, 0)),
                  pl.BlockSpec((d, tn), lambda j: (0, j)),
                  pl.BlockSpec((1, tn), lambda j: (0, j))],
        out_specs=pl.BlockSpec((rows, tn), lambda j: (0, j)),
        out_shape=jax.ShapeDtypeStruct((rows, n), F32),
        compiler_params=_cparams(("parallel",)),
        name="modulation",
    )(c_all, w_mod, b_mod.reshape(1, n))


def _inproj_body(*refs, rope):
    h_ref, mod_ref, nw_ref, w_ref = refs[:4]
    rest = refs[4:]
    cos = sin = None
    if rope:
        cos, sin = rest[0][...], rest[1][...]
        rest = rest[2:]
    qw_ref, kw_ref, bd_ref, qkv_ref, z_ref, sc_ref, g_ref, ab_ref, qh_ref, kh_ref, vh_ref = rest
    u = _rms_mod(h_ref[...], nw_ref[...], mod_ref[0, 0:1, :], mod_ref[0, 1:2, :])
    ub = u.astype(BF16)
    offs = [sum(INPROJ_WIDTHS[:i]) for i in range(len(INPROJ_WIDTHS))]
    proj = lambda i: _dot(ub, w_ref[:, offs[i]:offs[i] + INPROJ_WIDTHS[i]])
    _attention_operands(proj(3), proj(4), cos, sin, qw_ref[...], kw_ref[...], bd_ref[...], qh_ref, kh_ref, vh_ref)
    qkv_ref[...] = proj(0)
    z_ref[...] = proj(1)
    sc_ref[...] = proj(2)
    g_ref[...] = proj(5)
    ab_ref[...] = proj(6)


def _inproj(h, mod, mod_idx, nw, w, qw_row, kw_row, bd, cos=None, sin=None, seq_len=None):
    nt, d = h.shape
    tm = 256
    wtot = w.shape[1]
    rope = cos is not None
    row = lambda i: (i, 0)
    const = lambda i: (0, 0)
    in_specs = [pl.BlockSpec((tm, d), row),
                pl.BlockSpec((1, 6, d), lambda i: (mod_idx(i, tm), 0, 0)),
                pl.BlockSpec((1, d), const),
                pl.BlockSpec((d, wtot), const, pipeline_mode=pl.Buffered(1))]
    args = [h, mod, nw.reshape(1, d), w]
    if rope:
        tps = seq_len // tm
        in_specs += [pl.BlockSpec((tm, W_Q), lambda i: (i % tps, 0))] * 2
        args += [cos, sin]
    in_specs += [pl.BlockSpec((1, W_Q), const),
                 pl.BlockSpec((1, ATT_KV_HEADS * ATT_HD), const),
                 pl.BlockSpec((W_Q, W_Q), const)]
    args += [qw_row, kw_row, bd]
    plain = (W_QKV, W_Z, W_SC, W_G, W_AB)
    heads = (ATT_HEADS, ATT_KV_HEADS, ATT_KV_HEADS)
    return pl.pallas_call(
        functools.partial(_inproj_body, rope=rope),
        grid=(nt // tm,),
        in_specs=in_specs,
        out_specs=[pl.BlockSpec((tm, wd), row) for wd in plain]
        + [pl.BlockSpec((nh, tm, ATT_HD), lambda i: (0, i, 0)) for nh in heads],
        out_shape=[jax.ShapeDtypeStruct((nt, wd), F32) for wd in plain]
        + [jax.ShapeDtypeStruct((nh, nt, ATT_HD), BF16) for nh in heads],
        compiler_params=_cparams(("parallel",)),
        name="inproj",
    )(*args)


def _bdot(a, b):
    return lax.dot_general(a, b, (((2,), (1,)), ((0,), (0,))), preferred_element_type=F32)


def _tri_inverse_minus_eye(n):
    nb = n.astype(BF16)
    p = _bdot(nb, nb)
    y = -n
    for step in range(5):
        y = y + p + _bdot(y.astype(BF16), p.astype(BF16))
        if step < 4:
            pb = p.astype(BF16)
            p = _bdot(pb, pb)
    return y


def _gdn_prep_body(x_ref, xp_ref, xn_ref, ab_ref, cw_ref, alog_ref, dtb_ref, *outs, tps):
    i = pl.program_id(0)
    first = (i % tps) == 0
    last = (i % tps) == tps - 1
    x = x_ref[...]
    nrow = x.shape[0]
    c = GDN_CHUNK
    cpt = nrow // c
    rows = lax.broadcasted_iota(jnp.int32, x.shape, 0)
    prev_row = jnp.where(first, 0.0, xp_ref[7:8, :])
    next_row = jnp.where(last, 0.0, xn_ref[0:1, :])
    xprev = jnp.where(rows == 0, prev_row, pltpu.roll(x, 1, 0))
    xnext = jnp.where(rows == nrow - 1, next_row, pltpu.roll(x, nrow - 1, 0))
    y = cw_ref[0:1, :] * xprev + cw_ref[1:2, :] * x + cw_ref[2:3, :] * xnext
    y = y * jax.nn.sigmoid(y)

    ab = ab_ref[...]
    zz = ab + dtb_ref[...]
    sp = jnp.maximum(zz, 0.0) + jnp.log1p(jnp.exp(-jnp.abs(zz)))
    g = -jnp.exp(alog_ref[...]) * sp
    beta = jax.nn.sigmoid(ab)

    ri = lax.broadcasted_iota(jnp.int32, (nrow, nrow), 0)
    ci = lax.broadcasted_iota(jnp.int32, (nrow, nrow), 1)
    same = (ri // c) == (ci // c)
    lower = (same & (ri >= ci)).astype(BF16)
    ones_blk = same.astype(BF16)
    g1 = g.astype(BF16)
    r1 = g - g1.astype(F32)
    g2 = r1.astype(BF16)
    g3 = (r1 - g2.astype(F32)).astype(BF16)
    gc_f = _dot(lower, g1) + _dot(lower, g2) + _dot(lower, g3)
    tot = _dot(ones_blk, g1) + _dot(ones_blk, g2) + _dot(ones_blk, g3)
    gc_b = tot - gc_f + g
    gcf_t = gc_f.T
    gcb_t = gc_b.T
    egc_f = jnp.exp(gc_f)
    egc_b = jnp.exp(gc_b)
    kdec_f = jnp.exp(tot - gc_f)
    kdec_b = jnp.exp(tot - gc_b)
    etot = jnp.exp(tot)

    pi = lax.broadcasted_iota(jnp.int32, (2 * c, 2 * c), 0)
    pj = lax.broadcasted_iota(jnp.int32, (2 * c, 2 * c), 1)
    top = (pi < c) & (pj < c)
    bot = (pi >= c) & (pj >= c)
    incl = (top & (pi >= pj)) | (bot & (pi <= pj))
    strict = (top & (pi > pj)) | (bot & (pi < pj))

    for d in range(2):
        outs[6 * d + 5][...] = jnp.zeros(outs[6 * d + 5].shape, F32)

    n_list, rhs_list, qk_list = [], [], []
    for h in range(GDN_HEADS):
        q = y[:, h * GDN_DK:(h + 1) * GDN_DK]
        k = y[:, GDN_W + h * GDN_DK:GDN_W + (h + 1) * GDN_DK]
        v = y[:, 2 * GDN_W + h * GDN_DV:2 * GDN_W + (h + 1) * GDN_DV]
        q = q * lax.rsqrt(jnp.sum(q * q, axis=-1, keepdims=True) + NORM_EPS) * (GDN_DK ** -0.5)
        k = k * lax.rsqrt(jnp.sum(k * k, axis=-1, keepdims=True) + NORM_EPS)
        cf, cb = h, GDN_HEADS + h
        bf_, bb_ = 2 * GDN_HEADS + h, 3 * GDN_HEADS + h
        sl = slice(h * GDN_DV, (h + 1) * GDN_DV)
        outs[2][:, sl] = (k * kdec_f[:, cf:cf + 1]).astype(BF16)
        outs[8][:, sl] = (k * kdec_b[:, cb:cb + 1]).astype(BF16)
        outs[3][:, sl] = (q * egc_f[:, cf:cf + 1]).astype(BF16)
        outs[9][:, sl] = (q * egc_b[:, cb:cb + 1]).astype(BF16)
        for ch in range(cpt):
            r = slice(ch * c, (ch + 1) * c)
            k2 = jnp.concatenate([k[r], k[r]], axis=0)
            q2 = jnp.concatenate([q[r], q[r]], axis=0)
            v2 = jnp.concatenate([v[r], v[r]], axis=0)
            kb2 = k2.astype(BF16)
            kk2 = _dot_nt(kb2, kb2)
            qk2 = _dot_nt(q2.astype(BF16), kb2)
            gcol = jnp.concatenate([gc_f[r, cf:cf + 1], gc_b[r, cb:cb + 1]], axis=0)
            grow = jnp.concatenate([gcf_t[cf:cf + 1, r], gcb_t[cb:cb + 1, r]], axis=1)
            bcol = jnp.concatenate([beta[r, bf_:bf_ + 1], beta[r, bb_:bb_ + 1]], axis=0)
            ecol = jnp.concatenate([egc_f[r, cf:cf + 1], egc_b[r, cb:cb + 1]], axis=0)
            ldec = jnp.where(incl, jnp.exp(jnp.where(incl, gcol - grow, 0.0)), 0.0)
            n_list.append(jnp.where(strict, kk2 * bcol * ldec, 0.0))
            rhs_list.append(jnp.concatenate([v2 * bcol, k2 * (bcol * ecol)], axis=1))
            qk_list.append(qk2 * ldec)
            for d, col in ((0, cf), (1, cb)):
                outs[6 * d + 5][ch, h:h + 1, :] = jnp.broadcast_to(etot[ch * c:ch * c + 1, col:col + 1], (1, 128))

    yinv = _tri_inverse_minus_eye(jnp.stack(n_list, axis=0))
    rhs = jnp.stack(rhs_list, axis=0)
    sol = rhs + _bdot(yinv.astype(BF16), rhs.astype(BF16))
    for h in range(GDN_HEADS):
        sl = slice(h * GDN_DV, (h + 1) * GDN_DV)
        for ch in range(cpt):
            r = slice(ch * c, (ch + 1) * c)
            s = sol[h * cpt + ch]
            qk = qk_list[h * cpt + ch]
            outs[0][r, sl] = s[:c, :GDN_DV]
            outs[1][r, sl] = s[:c, GDN_DV:].astype(BF16)
            outs[6][r, sl] = s[c:, :GDN_DV]
            outs[7][r, sl] = s[c:, GDN_DV:].astype(BF16)
            outs[4][r, h * c:(h + 1) * c] = qk[:c, :c].astype(BF16)
            outs[10][r, h * c:(h + 1) * c] = qk[c:, c:].astype(BF16)


def _gdn_prep(qkv, ab, conv_w, alog_row, dtb_row, seq_len):
    nt = qkv.shape[0]
    c = GDN_CHUNK
    tr = GDN_TILE
    cpt = tr // c
    tps = seq_len // tr
    nblk8 = nt // 8
    out_shape, out_specs = [], []
    for _ in range(2):
        for wd, dt in ((GDN_W, F32), (GDN_W, BF16), (GDN_W, BF16), (GDN_W, BF16), (GDN_HEADS * c, BF16)):
            out_shape.append(jax.ShapeDtypeStruct((nt, wd), dt))
            out_specs.append(pl.BlockSpec((tr, wd), lambda i: (i, 0)))
        out_shape.append(jax.ShapeDtypeStruct((nt // c, 8, 128), F32))
        out_specs.append(pl.BlockSpec((cpt, 8, 128), lambda i: (i, 0, 0)))
    return pl.pallas_call(
        functools.partial(_gdn_prep_body, tps=tps),
        grid=(nt // tr,),
        in_specs=[pl.BlockSpec((tr, W_QKV), lambda i: (i, 0)),
                  pl.BlockSpec((8, W_QKV), lambda i: (jnp.maximum(i * (tr // 8) - 1, 0), 0)),
                  pl.BlockSpec((8, W_QKV), lambda i: (jnp.minimum((i + 1) * (tr // 8), nblk8 - 1), 0)),
                  pl.BlockSpec((tr, W_AB), lambda i: (i, 0)),
                  pl.BlockSpec((3, W_QKV), lambda i: (0, 0)),
                  pl.BlockSpec((1, 128), lambda i: (0, 0)),
                  pl.BlockSpec((1, 128), lambda i: (0, 0))],
        out_specs=out_specs,
        out_shape=out_shape,
        compiler_params=_cparams(("parallel",)),
        name="gdn_prep",
    )(qkv, qkv, qkv, ab, conv_w, alog_row, dtb_row)


def _gdn_scan_body(*refs, nch):
    ins = refs[:12]
    s0_ref = refs[12]
    o_refs = refs[13:15]
    sfin_ref = refs[15]
    s_scr = refs[16]
    n = pl.program_id(1)
    c = GDN_CHUNK

    @pl.when(n == 0)
    def _():
        s_scr[...] = s0_ref[0]

    chains = [(d, h) for d in range(2) for h in range(GDN_HEADS)]
    state, prod, vnew = {}, {}, {}
    for d, h in chains:
        w_ref, qd_ref = ins[6 * d + 1], ins[6 * d + 3]
        sl = slice(h * GDN_DV, (h + 1) * GDN_DV)
        state[d, h] = s_scr[d, h]
        wq = jnp.concatenate([w_ref[:, sl], qd_ref[:, sl]], axis=0)
        prod[d, h] = _dot(wq, state[d, h].astype(BF16))
    for d, h in chains:
        u_ref, qk_ref = ins[6 * d], ins[6 * d + 4]
        sl = slice(h * GDN_DV, (h + 1) * GDN_DV)
        r = prod[d, h]
        vnew[d, h] = (u_ref[:, sl] - r[:c]).astype(BF16)
        o_refs[d][:, sl] = r[c:] + _dot(qk_ref[:, h * c:(h + 1) * c], vnew[d, h])
    for d, h in chains:
        kd_ref, eg_ref = ins[6 * d + 2], ins[6 * d + 5]
        sl = slice(h * GDN_DV, (h + 1) * GDN_DV)
        s_scr[d, h] = state[d, h] * eg_ref[0, h:h + 1, :] + _dot_tn(kd_ref[:, sl], vnew[d, h])

    @pl.when(n == nch - 1)
    def _():
        sfin_ref[0] = s_scr[...]


def _gdn_scan(prep, s0, batch, seq_len):
    c = GDN_CHUNK
    nch = seq_len // c
    nt = batch * seq_len
    fwd = lambda b, n: (b * nch + n, 0)
    bwd = lambda b, n: (b * nch + (nch - 1 - n), 0)
    in_specs = []
    for d, im in enumerate((fwd, bwd)):
        for wd in (GDN_W, GDN_W, GDN_W, GDN_W, GDN_HEADS * c):
            in_specs.append(pl.BlockSpec((c, wd), im))
        in_specs.append(pl.BlockSpec((1, 8, 128), (lambda im_: (lambda b, n: im_(b, n) + (0,)))(im)))
    st_spec = pl.BlockSpec((1, 2, GDN_HEADS, GDN_DK, GDN_DV), lambda b, n: (b, 0, 0, 0, 0))
    in_specs.append(st_spec)
    o_f, o_b, s_fin = pl.pallas_call(
        functools.partial(_gdn_scan_body, nch=nch),
        grid=(batch, nch),
        in_specs=in_specs,
        out_specs=[pl.BlockSpec((c, GDN_W), fwd), pl.BlockSpec((c, GDN_W), bwd), st_spec],
        out_shape=[jax.ShapeDtypeStruct((nt, GDN_W), F32), jax.ShapeDtypeStruct((nt, GDN_W), F32),
                   jax.ShapeDtypeStruct(s0.shape, F32)],
        scratch_shapes=[pltpu.VMEM((2, GDN_HEADS, GDN_DK, GDN_DV), F32)],
        compiler_params=_cparams(("arbitrary", "arbitrary")),
        name="gdn_scan",
    )(*prep, s0)
    return o_f, o_b, s_fin


def _group_norm_rope(x, w, bd, cos, sin):
    xx = x * x
    hi, lo = _split_bf16(xx)
    ss = _dot(hi, bd) + _dot(lo, bd)
    xn = x * lax.rsqrt(ss * (1.0 / ATT_HD) + NORM_EPS) * w
    if cos is not None:
        width = x.shape[1]
        lane = lax.broadcasted_iota(jnp.int32, x.shape, 1)
        from_below = pltpu.roll(xn, ROPE_PAIRS, 1)
        from_above = pltpu.roll(xn, width - ROPE_PAIRS, 1)
        partner = jnp.where((lane % (2 * ROPE_PAIRS)) < ROPE_PAIRS, from_above, from_below)
        xn = xn * cos + partner * sin
    return xn


def _attention_operands(q, kv, cos, sin, qw, kw, bd, qh_ref, kh_ref, vh_ref):
    nk = ATT_KV_HEADS * ATT_HD
    kcos = ksin = None
    if cos is not None:
        kcos, ksin = cos[:, :nk], sin[:, :nk]
    q = _group_norm_rope(q, qw, bd, cos, sin) * (ATT_HD ** -0.5 * math.log2(math.e))
    k = _group_norm_rope(kv[:, :nk], kw, bd[:nk, :nk], kcos, ksin)
    v = kv[:, nk:]
    for j in range(ATT_HEADS):
        qh_ref[j] = q[:, j * ATT_HD:(j + 1) * ATT_HD].astype(BF16)
    for j in range(ATT_KV_HEADS):
        kh_ref[j] = k[:, j * ATT_HD:(j + 1) * ATT_HD].astype(BF16)
        vh_ref[j] = v[:, j * ATT_HD:(j + 1) * ATT_HD].astype(BF16)


def _attention_body(q_ref, *refs, src_lens, tk, tq):
    n_src = len(src_lens)
    o_ref = refs[2 * n_src]
    rows = ATT_GROUP * tq
    q = q_ref[...].reshape(rows, ATT_HD)

    def block(sc, v, carry):
        m, l, acc = carry
        m_new = jnp.maximum(m, jnp.max(sc, axis=1, keepdims=True))
        alpha = jnp.exp2(m - m_new)
        p = jnp.exp2(sc - m_new)
        l = alpha * l + jnp.sum(p, axis=1, keepdims=True)
        acc = alpha * acc + _dot(p.astype(BF16), v)
        return m_new, l, acc

    carry = (jnp.full((rows, 1), -jnp.inf, F32), jnp.zeros((rows, 1), F32), jnp.zeros((rows, ATT_HD), F32))
    for s in range(n_src):
        k_ref, v_ref = refs[2 * s], refs[2 * s + 1]
        blk = min(tk, src_lens[s])
        nblk = src_lens[s] // blk
        for j in range(nblk):
            sc = _dot_nt(q, k_ref[0, j * blk:(j + 1) * blk, :])
            carry = block(sc, v_ref[0, j * blk:(j + 1) * blk, :], carry)
    m, l, acc = carry
    o = (acc / l).reshape(ATT_GROUP, tq, ATT_HD)
    o_ref[...] = jnp.concatenate([o[j] for j in range(ATT_GROUP)], axis=1)


def _attention(qh, sources, q_len):
    nq = qh.shape[1]
    tq = 128
    tk = 8192
    tpb = q_len // tq
    in_specs = [pl.BlockSpec((ATT_GROUP, tq, ATT_HD), lambda g, i: (g, i, 0))]
    args = [qh]
    for kh, vh, sl in sources:
        spec = pl.BlockSpec((1, sl, ATT_HD), lambda g, i: (g, i // tpb, 0))
        in_specs += [spec, spec]
        args += [kh, vh]
    return pl.pallas_call(
        functools.partial(_attention_body, src_lens=tuple(s[2] for s in sources), tk=tk, tq=tq),
        grid=(ATT_KV_HEADS, nq // tq),
        in_specs=in_specs,
        out_specs=pl.BlockSpec((tq, ATT_GROUP * ATT_HD), lambda g, i: (i, g)),
        out_shape=jax.ShapeDtypeStruct((nq, W_Q), F32),
        compiler_params=_cparams(("parallel", "parallel")),
        name="attention",
    )(*args)


def _merge_body(of_ref, ob_ref, z_ref, p_ref, pp_ref, pn_ref, oc_ref, gt_ref, h_ref, mod_ref,
                gnw_ref, cw_ref, wb_ref, wo_ref, o_ref, *, tps):
    i = pl.program_id(0)
    first = (i % tps) == 0
    last = (i % tps) == tps - 1
    o = of_ref[...] + ob_ref[...]
    z = z_ref[...]
    parts = []
    for h in range(GDN_HEADS):
        sl = slice(h * GDN_DV, (h + 1) * GDN_DV)
        oh = o[:, sl]
        oh = oh * lax.rsqrt(jnp.mean(oh * oh, axis=-1, keepdims=True) + NORM_EPS) * gnw_ref[...]
        zh = z[:, sl]
        parts.append(oh * (zh * jax.nn.sigmoid(zh)))
    br_a = jnp.concatenate(parts, axis=1)
    p = p_ref[...]
    tm = p.shape[0]
    bg = p[:, :SCONV_W]
    cx = p[:, SCONV_W:2 * SCONV_W] * p[:, 2 * SCONV_W:]
    pp = pp_ref[7:8, :]
    pn = pn_ref[0:1, :]
    prev_row = jnp.where(first, 0.0, pp[:, SCONV_W:2 * SCONV_W] * pp[:, 2 * SCONV_W:])
    next_row = jnp.where(last, 0.0, pn[:, SCONV_W:2 * SCONV_W] * pn[:, 2 * SCONV_W:])
    rows = lax.broadcasted_iota(jnp.int32, cx.shape, 0)
    cprev = jnp.where(rows == 0, prev_row, pltpu.roll(cx, 1, 0))
    cnext = jnp.where(rows == tm - 1, next_row, pltpu.roll(cx, tm - 1, 0))
    br_b = bg * (cw_ref[0:1, :] * cprev + cw_ref[1:2, :] * cx + cw_ref[2:3, :] * cnext)
    br_c = oc_ref[...]
    mixed = None
    for n, br in enumerate((br_a, br_b, br_c)):
        t = _dot(br.astype(BF16), wb_ref[n]) * jax.nn.sigmoid(gt_ref[:, n * D_MODEL:(n + 1) * D_MODEL])
        mixed = t if mixed is None else mixed + t
    y = _dot(mixed.astype(BF16), wo_ref[...])
    o_ref[...] = h_ref[...] + mod_ref[0, 2:3, :] * y


def _merge(o_f, o_b, z, psc, oc, gates, h, mod, mod_idx, gnw_row, conv_b_w, wb, wo, seq_len):
    nt, d = h.shape
    tm = 256
    tps = seq_len // tm
    nblk8 = nt // 8
    row = lambda i: (i, 0)
    return pl.pallas_call(
        functools.partial(_merge_body, tps=tps),
        grid=(nt // tm,),
        in_specs=[pl.BlockSpec((tm, GDN_W), row),
                  pl.BlockSpec((tm, GDN_W), row),
                  pl.BlockSpec((tm, W_Z), row),
                  pl.BlockSpec((tm, W_SC), row),
                  pl.BlockSpec((8, W_SC), lambda i: (jnp.maximum(i * (tm // 8) - 1, 0), 0)),
                  pl.BlockSpec((8, W_SC), lambda i: (jnp.minimum((i + 1) * (tm // 8), nblk8 - 1), 0)),
                  pl.BlockSpec((tm, W_Q), row),
                  pl.BlockSpec((tm, W_G), row),
                  pl.BlockSpec((tm, d), row),
                  pl.BlockSpec((1, 6, d), lambda i: (mod_idx(i, tm), 0, 0)),
                  pl.BlockSpec((1, GDN_DV), lambda i: (0, 0)),
                  pl.BlockSpec((3, SCONV_W), lambda i: (0, 0)),
                  pl.BlockSpec((3, SCONV_W, d), lambda i: (0, 0, 0)),
                  pl.BlockSpec((d, d), lambda i: (0, 0))],
        out_specs=pl.BlockSpec((tm, d), row),
        out_shape=jax.ShapeDtypeStruct((nt, d), F32),
        compiler_params=_cparams(("parallel",)),
        name="merge",
    )(o_f, o_b, z, psc, psc, psc, oc, gates, h, mod, gnw_row, conv_b_w, wb, wo)


PEER_NO_RANK = 64.0


def _oddeven_merge_sort_pairs(n):
    pairs = []
    p = 1
    while p < n:
        k = p
        while k >= 1:
            for j in range(k % p, n - k, 2 * k):
                for i in range(min(k, n - j - k)):
                    if (i + j) // (2 * p) == (i + j + k) // (2 * p):
                        pairs.append((i + j, i + j + k))
            k //= 2
        p *= 2
    return pairs


def _top_rows_sorted(s):
    cnt = PEER_TOPK
    v = [s[8 * i:8 * (i + 1), :] for i in range(s.shape[0] // 8)]
    for i, j in _oddeven_merge_sort_pairs(len(v)):
        v[i], v[j] = jnp.maximum(v[i], v[j]), jnp.minimum(v[i], v[j])
    for shift in (4, 2, 1):
        other = [pltpu.roll(x, shift, 0) for x in v]
        if len(v) == cnt:
            v = [jnp.maximum(v[k], other[cnt - 1 - k]) for k in range(cnt)]
        else:
            v = v + other[::-1]
        d = cnt // 2
        while d >= 1:
            for k in range(cnt):
                if k & d == 0:
                    v[k], v[k + d] = jnp.maximum(v[k], v[k + d]), jnp.minimum(v[k], v[k + d])
            d //= 2
    return [x[0:1, :] for x in v]


def _peer_body(h_ref, mod_ref, nw_ref, fw_ref, wq_ref, sk_ref, u_ref, vt_ref, o_ref,
               xt_scr, out_scr, sc_scr, row_scr, col_scr, cand_scr, gw_scr,
               *, tm, ec, nchunks, final_norm):
    c = pl.program_id(1)
    nk = PEER_NKEYS
    half = nk // 2
    tb_w = PEER_TB
    ntb = tm // tb_w
    nlb = tm // 128
    lb_per_tb = tb_w // 128

    @pl.when(c == 0)
    def _():
        u = _rms_mod(h_ref[...], nw_ref[...], mod_ref[0, 3:4, :], mod_ref[0, 4:5, :])
        ut = u.T
        uh = ut.astype(BF16)
        qt = _dot(wq_ref[...], uh)
        for tb in range(ntb):
            xt_scr[tb] = uh[:, tb * tb_w:(tb + 1) * tb_w]
            out_scr[tb] = jnp.zeros(out_scr.shape[1:], F32)
        for h in range(PEER_HEADS):
            for p in range(2):
                qs = qt[h * nk + p * half:h * nk + (p + 1) * half, :]
                s = _dot(sk_ref[h, p], qs.astype(BF16))
                for lb in range(nlb):
                    sc_scr[p, h * nlb + lb] = s[:, lb * 128:(lb + 1) * 128]

        def select_one(hl):
            s0 = sc_scr[0, hl]
            s1 = sc_scr[1, hl]
            top0 = _top_rows_sorted(s0)
            top1 = _top_rows_sorted(s1)
            rank1 = jnp.full(s1.shape, PEER_NO_RANK, F32)
            for l in reversed(range(PEER_TOPK)):
                rank1 = jnp.where(s1 >= top1[l], float(l + 1), rank1)
            pairs = [(k, l) for k in range(PEER_TOPK) for l in range(PEER_TOPK) if (k + 1) * (l + 1) <= PEER_TOPK]
            cand = cand_scr.at[hl % PEER_SELECT_GROUP]
            cand[...] = jnp.full(cand.shape, -jnp.inf, F32)
            sums = {}
            for r, (k, l) in enumerate(pairs):
                sums[k, l] = top0[k] + top1[l]
                cand[r:r + 1, :] = sums[k, l]
            best = _top_rows_sorted(cand[...])
            tau = best[PEER_TOPK - 1]
            zsum = jnp.zeros_like(tau)
            for b_ in best:
                zsum = zsum + jnp.exp(b_ - best[0])
            cnt = jnp.zeros(s0.shape, F32)
            for k in range(PEER_TOPK):
                n_sel = jnp.zeros_like(tau)
                for l in range(PEER_TOPK // (k + 1)):
                    n_sel = n_sel + jnp.where(sums[k, l] >= tau, 1.0, 0.0)
                cnt = jnp.where(s0 == top0[k], n_sel, cnt)
            hh = hl // nlb
            ll = hl % nlb
            row_scr[ll, :, hh, 0] = cnt.reshape(nk // 8, 8, 128)
            row_scr[ll, :, hh, 1] = (jnp.exp(s0 - top0[0]) / zsum).reshape(nk // 8, 8, 128)
            col_scr[ll, :, hh, 0] = rank1.astype(BF16).reshape(nk // 16, 16, 128)
            col_scr[ll, :, hh, 1] = jnp.exp(s1 - top1[0]).astype(BF16).reshape(nk // 16, 16, 128)

        def select(grp, carry):
            for k in range(PEER_SELECT_GROUP):
                select_one(PEER_SELECT_GROUP * grp + k)
            return carry

        lax.fori_loop(0, PEER_HEADS * nlb // PEER_SELECT_GROUP, select, 0)

    nblk = ec // nk
    piece = 16
    npiece = nk // piece

    def half_dots(lhs_ref, rhs):
        hm = lhs_ref.shape[0] // 2
        return [_dot(lhs_ref[0:hm, :], rhs), _dot(lhs_ref[hm:2 * hm, :], rhs)]

    def gated_activation(tb, acts):
        hm = ec // 2
        for sub in range(lb_per_tb):
            lb = tb * lb_per_tb + sub
            ls = slice(sub * 128, (sub + 1) * 128)
            for ii in range(nblk):
                wm = [None] * npiece
                for h in range(PEER_HEADS):
                    cnt_row = jnp.broadcast_to(row_scr[lb, c * (nblk // 8) + ii // 8, h, 0, ii % 8:ii % 8 + 1, :], (piece, 128)).astype(BF16)
                    ea_row = jnp.broadcast_to(row_scr[lb, c * (nblk // 8) + ii // 8, h, 1, ii % 8:ii % 8 + 1, :], (piece, 128)).astype(BF16)
                    for jp in range(npiece):
                        sel = col_scr[lb, jp, h, 0] <= cnt_row
                        w = col_scr[lb, jp, h, 1] * ea_row
                        w = jnp.where(sel, w, jnp.zeros_like(w))
                        wm[jp] = w if h == 0 else wm[jp] + w
                for jp in range(npiece):
                    r0 = ii * nk + jp * piece
                    ap = acts[r0 // hm][r0 % hm:r0 % hm + piece, ls]
                    gl = 0.5 * ap * (1.0 + lax.erf(ap * (0.5 ** 0.5)))
                    gw_scr[tb, r0:r0 + piece, ls] = gl.astype(BF16) * wm[jp]

    acts = [half_dots(u_ref, xt_scr[tb]) for tb in range(ntb)]
    for tb in range(ntb):
        gated_activation(tb, acts[tb])
        outs = half_dots(vt_ref.at[0], gw_scr[tb])
        hd = out_scr.shape[1] // 2
        out_scr[tb, 0:hd, :] += outs[0]
        out_scr[tb, hd:2 * hd, :] += outs[1]

    @pl.when(c == nchunks - 1)
    def _():
        out = jnp.concatenate([out_scr[tb] for tb in range(ntb)], axis=1)
        hn = h_ref[...] + mod_ref[0, 5:6, :] * out.T
        if final_norm:
            hn = hn * lax.rsqrt(jnp.mean(hn * hn, axis=-1, keepdims=True) + NORM_EPS) * fw_ref[...]
        o_ref[...] = hn


def _peer(h, mod, mod_idx, nw, final_w, apply_final, wq_t, sk, u_tab, vt_tab):
    nt, d = h.shape
    tm = PEER_TM
    ec = PEER_EC
    ne = u_tab.shape[0]
    nchunks = ne // ec
    nk = PEER_NKEYS
    ntb = tm // PEER_TB
    nlb = tm // 128
    const2 = lambda i, c: (0, 0)
    return pl.pallas_call(
        functools.partial(_peer_body, tm=tm, ec=ec, nchunks=nchunks, final_norm=apply_final),
        grid=(nt // tm, nchunks),
        in_specs=[pl.BlockSpec((tm, d), lambda i, c: (i, 0)),
                  pl.BlockSpec((1, 6, d), lambda i, c: (mod_idx(i, tm), 0, 0)),
                  pl.BlockSpec((1, d), const2),
                  pl.BlockSpec((1, d), const2),
                  pl.BlockSpec((d, d), const2),
                  pl.BlockSpec((PEER_HEADS, 2, nk, nk // 2), lambda i, c: (0, 0, 0, 0)),
                  pl.BlockSpec((ec, d), lambda i, c: (c, 0)),
                  pl.BlockSpec((1, d, ec), lambda i, c: (c, 0, 0))],
        out_specs=pl.BlockSpec((tm, d), lambda i, c: (i, 0)),
        out_shape=jax.ShapeDtypeStruct((nt, d), F32),
        scratch_shapes=[pltpu.VMEM((ntb, d, PEER_TB), BF16),
                        pltpu.VMEM((ntb, d, PEER_TB), F32),
                        pltpu.VMEM((2, PEER_HEADS * nlb, nk, 128), F32),
                        pltpu.VMEM((nlb, nk // 8, PEER_HEADS, 2, 8, 128), F32),
                        pltpu.VMEM((nlb, nk // 16, PEER_HEADS, 2, 16, 128), BF16),
                        pltpu.VMEM((PEER_SELECT_GROUP, 64, 128), F32),
                        pltpu.VMEM((ntb, ec, PEER_TB), BF16)],
        compiler_params=_cparams(("parallel", "arbitrary")),
        name="peer",
    )(h, mod, nw.reshape(1, d), final_w.reshape(1, d), wq_t, sk, u_tab, vt_tab)


def _rope_tables(seq):
    t = jnp.arange(seq)
    row = (t // GRID_W).astype(F32)
    col = (t % GRID_W).astype(F32)
    freqs = ROPE_THETA ** (-jnp.arange(ROPE_PAIRS, dtype=F32) / ROPE_PAIRS)
    ar = row[:, None] * freqs
    ac = col[:, None] * freqs
    cos = jnp.concatenate([jnp.cos(ar), jnp.cos(ar), jnp.cos(ac), jnp.cos(ac)], axis=1)
    sin = jnp.concatenate([-jnp.sin(ar), jnp.sin(ar), -jnp.sin(ac), jnp.sin(ac)], axis=1)
    return jnp.tile(cos, (1, ATT_HEADS)), jnp.tile(sin, (1, ATT_HEADS))


def _reorder_w_in(w):
    sizes = (W_QKV, W_Z, 4 * GDN_HEADS, W_SC, W_Q, W_KV, W_G)
    parts, start = [], 0
    for s in sizes:
        parts.append(w[:, start:start + s])
        start += s
    qkv, z, ab, sc, q, kv, g = parts
    ab = jnp.pad(ab, ((0, 0), (0, W_AB - ab.shape[1])))
    return jnp.concatenate([qkv, z, sc, q, kv, g, ab], axis=1).astype(BF16)


def _pad_row(v, n=128):
    v = v.reshape(1, -1)
    return jnp.pad(v, ((0, 0), (0, n - v.shape[1])))


def kernel(x, c, ctx, c_ctx, w_mod, b_mod, norm1_w, w_in, conv_a_w, a_log, dt_bias, gdn_norm_w, conv_b_w,
           q_norm_w, k_norm_w, w_branch, w_out, norm2_w, w_query, sub_keys, expert_u, expert_v, final_norm_w):
    batch, seq, d = x.shape
    ctx_len = ctx.shape[1]
    depth = w_mod.shape[0]
    h = x.reshape(batch * seq, d)
    hc = ctx.reshape(batch * ctx_len, d)

    mod_rows = -(-(batch + 1) // 8) * 8
    c_all = jnp.zeros((mod_rows, d), F32).at[:batch].set(c).at[batch].set(c_ctx)
    cos, sin = _rope_tables(seq)
    hd_i = jnp.arange(W_Q) // ATT_HD
    bd = (hd_i[:, None] == hd_i[None, :]).astype(BF16)
    lat_idx = lambda i, tm: i // (seq // tm)
    ctx_idx = lambda i, tm: batch

    for layer in range(depth):
        last = layer == depth - 1
        mod = _modulation(c_all, w_mod[layer], b_mod[layer]).reshape(mod_rows, 6, d)
        w_in_r = _reorder_w_in(w_in[layer])
        alog_row = _pad_row(a_log[layer])
        dtb_row = _pad_row(dt_bias[layer])
        qw_row = jnp.tile(q_norm_w[layer], ATT_HEADS).reshape(1, W_Q)
        kw_row = jnp.tile(k_norm_w[layer], ATT_KV_HEADS).reshape(1, ATT_KV_HEADS * ATT_HD)
        gnw_row = gdn_norm_w[layer].reshape(1, GDN_DV)
        wb = w_branch[layer].astype(BF16)
        wo = w_out[layer].astype(BF16)
        wq_t = w_query[layer].T.astype(BF16)
        sk = sub_keys[layer].astype(BF16)
        u_tab = expert_u[layer].astype(BF16)
        vt_tab = expert_v[layer].astype(BF16).reshape(-1, PEER_EC, d).transpose(0, 2, 1)

        qkv_l, z_l, sc_l, g_l, ab_l, qh_l, kh_l, vh_l = _inproj(h, mod, lat_idx, norm1_w[layer], w_in_r, qw_row, kw_row, bd,
                                                                cos, sin, seq)
        qkv_c, z_c, sc_c, g_c, ab_c, qh_c, kh_c, vh_c = _inproj(hc, mod, ctx_idx, norm1_w[layer], w_in_r, qw_row, kw_row, bd)

        prep_c = _gdn_prep(qkv_c, ab_c, conv_a_w[layer], alog_row, dtb_row, ctx_len)
        prep_l = _gdn_prep(qkv_l, ab_l, conv_a_w[layer], alog_row, dtb_row, seq)
        s_zero = jnp.zeros((batch, 2, GDN_HEADS, GDN_DK, GDN_DV), F32)
        ocf, ocb, s_ctx = _gdn_scan(prep_c, s_zero, batch, ctx_len)
        olf, olb, _ = _gdn_scan(prep_l, s_ctx, batch, seq)

        att_l = _attention(qh_l, [(kh_l, vh_l, seq), (kh_c, vh_c, ctx_len)], seq)

        h = _merge(olf, olb, z_l, sc_l, att_l, g_l, h, mod, lat_idx, gnw_row, conv_b_w[layer],
                   wb, wo, seq)
        h = _peer(h, mod, lat_idx, norm2_w[layer], final_norm_w, last, wq_t, sk, u_tab, vt_tab)
        if not last:
            att_c = _attention(qh_c, [(kh_c, vh_c, ctx_len)], ctx_len)
            hc = _merge(ocf, ocb, z_c, sc_c, att_c, g_c, hc, mod, ctx_idx, gnw_row, conv_b_w[layer],
                        wb, wo, ctx_len)
            hc = _peer(hc, mod, ctx_idx, norm2_w[layer], final_norm_w, False, wq_t, sk, u_tab, vt_tab)
    return h.reshape(batch, seq, d)
```
